```python
import jax, jax.numpy as jnp
from jax import lax
import numpy as np

D_MODEL = 1024
BATCH = 4
SEQ = 4096
DEPTH = 1

MLSTM_HEADS = 4
MLSTM_QK_DIM = 128
MLSTM_V_DIM = D_MODEL // MLSTM_HEADS
MLSTM_CHUNK = 64
CONV_WIDTH = 4
MLA_HEADS = 8
MLA_NOPE_DIM = 128
MLA_ROPE_DIM = 64
MLA_V_DIM = D_MODEL // MLA_HEADS
MLA_Q_RANK = 384
MLA_KV_RANK = 256
ROPE_THETA = 10000.0
ATTN_BLOCK = 128
D_FF = 4 * D_MODEL
EPS = 1e-6
N_MOD = 6

MA_QK = MLSTM_HEADS * MLSTM_QK_DIM
MA_V = MLSTM_HEADS * MLSTM_V_DIM
SPLIT_SIZES = (MA_QK, MA_QK, MA_V, MA_V, MLSTM_HEADS, MLSTM_HEADS,
               MLA_Q_RANK, MLA_KV_RANK, MLA_ROPE_DIM, D_MODEL, D_MODEL)
D_IN = sum(SPLIT_SIZES)
SPLIT_IDX = tuple(int(v) for v in np.cumsum(SPLIT_SIZES)[:-1])

kernel_name = 'hybrid_mlstm_mla_block'


def rmsnorm(x, w):
    xf = x.astype(jnp.float32)
    y = xf * lax.rsqrt(jnp.mean(xf * xf, axis=-1, keepdims=True) + EPS)
    return (y * w.astype(jnp.float32)).astype(x.dtype)


def causal_dwconv(x, w, b):
    ch = x.shape[-1]
    y = lax.conv_general_dilated(x, w[:, None, :].astype(x.dtype), window_strides=(1,),
                                 padding=[(CONV_WIDTH - 1, 0)],
                                 dimension_numbers=('NWC', 'WIO', 'NWC'),
                                 feature_group_count=ch)
    return y + b.astype(x.dtype)


def rope(x, positions):
    half = x.shape[-1] // 2
    inv_freq = ROPE_THETA ** (-jnp.arange(half, dtype=jnp.float32) / half)
    ang = positions.astype(jnp.float32)[:, None] * inv_freq[None, :]
    cos, sin = jnp.cos(ang), jnp.sin(ang)
    x1, x2 = x[..., :half], x[..., half:]
    return jnp.concatenate([x1 * cos - x2 * sin, x2 * cos + x1 * sin], axis=-1)


def mlstm_branch(q, k, v, o_pre, i_pre, f_pre, head_norm_w):
    bsz, seq, _ = q.shape
    H, L = MLSTM_HEADS, MLSTM_CHUNK
    nc = seq // L
    f32 = jnp.float32

    def chunks(t, d):
        return t.astype(f32).reshape(bsz, nc, L, H, d).transpose(1, 0, 3, 2, 4)

    def gchunks(t):
        return t.astype(f32).reshape(bsz, nc, L, H).transpose(1, 0, 3, 2)

    qc = chunks(q, MLSTM_QK_DIM)
    kc = chunks(k, MLSTM_QK_DIM) * (MLSTM_QK_DIM ** -0.5)
    vc = chunks(v, MLSTM_V_DIM)
    igc = gchunks(i_pre)
    lfc = jax.nn.log_sigmoid(gchunks(f_pre))
    causal = jnp.tril(jnp.ones((L, L), dtype=bool))

    def step(carry, xs):
        C, n, m = carry
        qb, kb, vb, ib, lb = xs
        b = jnp.cumsum(lb, axis=-1)
        a = b + m[..., None]
        D = jnp.where(causal, b[..., :, None] - b[..., None, :] + ib[..., None, :], -jnp.inf)
        m_out = jnp.maximum(a, jnp.max(D, axis=-1))
        s = jnp.einsum('bhtd,bhsd->bhts', qb, kb) * jnp.exp(D - m_out[..., None])
        w_inter = jnp.exp(a - m_out)
        num = (jnp.einsum('bhts,bhsv->bhtv', s, vb)
               + w_inter[..., None] * jnp.einsum('bhvd,bhtd->bhtv', C, qb))
        den = jnp.sum(s, axis=-1) + w_inter * jnp.einsum('bhd,bhtd->bht', n, qb)
        h = num / jnp.maximum(jnp.abs(den), jnp.exp(-m_out))[..., None]
        g_prev = b[..., -1] + m
        g = b[..., -1:] - b + ib
        m_new = jnp.maximum(g_prev, jnp.max(g, axis=-1))
        wk = jnp.exp(g - m_new[..., None])
        decay = jnp.exp(g_prev - m_new)
        C_new = decay[..., None, None] * C + jnp.einsum('bhs,bhsv,bhsd->bhvd', wk, vb, kb)
        n_new = decay[..., None] * n + jnp.einsum('bhs,bhsd->bhd', wk, kb)
        return (C_new, n_new, m_new), h

    init = (jnp.zeros((bsz, H, MLSTM_V_DIM, MLSTM_QK_DIM), f32),
            jnp.zeros((bsz, H, MLSTM_QK_DIM), f32),
            jnp.zeros((bsz, H), f32))
    _, h = lax.scan(step, init, (qc, kc, vc, igc, lfc))
    h = h.transpose(1, 0, 3, 2, 4).reshape(bsz, seq, H, MLSTM_V_DIM)
    h = h * lax.rsqrt(jnp.mean(h * h, axis=-1, keepdims=True) + EPS)
    h = h.reshape(bsz, seq, MA_V) * head_norm_w.astype(f32)
    y = jax.nn.sigmoid(o_pre.astype(f32)) * h
    return y.astype(q.dtype)


def mla_branch(c_q, c_kv, k_rope, q_norm_w, kv_norm_w, w_uq, w_ukv):
    bsz, seq, _ = c_q.shape
    H = MLA_HEADS
    f32 = jnp.float32
    positions = jnp.arange(seq, dtype=jnp.int32)
    q = (rmsnorm(c_q, q_norm_w) @ w_uq).reshape(bsz, seq, H, MLA_NOPE_DIM + MLA_ROPE_DIM)
    kv = (rmsnorm(c_kv, kv_norm_w) @ w_ukv).reshape(bsz, seq, H, MLA_NOPE_DIM + MLA_V_DIM)
    q = q.astype(f32).transpose(0, 2, 1, 3)
    kv = kv.astype(f32).transpose(0, 2, 1, 3)
    q_nope, q_pe = q[..., :MLA_NOPE_DIM], rope(q[..., MLA_NOPE_DIM:], positions)
    k_nope, v = kv[..., :MLA_NOPE_DIM], kv[..., MLA_NOPE_DIM:]
    k_pe = rope(k_rope.astype(f32), positions)
    scale = (MLA_NOPE_DIM + MLA_ROPE_DIM) ** -0.5
    outs = []
    for i in range(seq // ATTN_BLOCK):
        start, end = i * ATTN_BLOCK, (i + 1) * ATTN_BLOCK
        sc = (jnp.einsum('bhqd,bhkd->bhqk', q_nope[:, :, start:end], k_nope[:, :, :end])
              + jnp.einsum('bhqr,bkr->bhqk', q_pe[:, :, start:end], k_pe[:, :end])) * scale
        mask = (start + jnp.arange(ATTN_BLOCK))[:, None] >= jnp.arange(end)[None, :]
        p = jax.nn.softmax(jnp.where(mask, sc, -jnp.inf), axis=-1)
        outs.append(jnp.einsum('bhqk,bhkv->bhqv', p, v[:, :, :end]))
    o = jnp.concatenate(outs, axis=2)
    return o.transpose(0, 2, 1, 3).reshape(bsz, seq, H * MLA_V_DIM).astype(c_q.dtype)


def token_mixer(h, w_in, conv_w, conv_b, gate_b, head_norm_w, q_norm_w, kv_norm_w,
                w_uq, w_ukv, w_out):
    (q_a, k_a, v_a, o_a, i_a, f_a, c_q, c_kv, k_pe, g_a, g_b) = jnp.split(h @ w_in, SPLIT_IDX, axis=-1)
    qk = jax.nn.silu(causal_dwconv(jnp.concatenate([q_a, k_a], axis=-1), conv_w, conv_b))
    q_a, k_a = qk[..., :MA_QK], qk[..., MA_QK:]
    i_a = i_a + gate_b[:MLSTM_HEADS].astype(h.dtype)
    f_a = f_a + gate_b[MLSTM_HEADS:].astype(h.dtype)
    y_a = mlstm_branch(q_a, k_a, v_a, o_a, i_a, f_a, head_norm_w)
    y_b = mla_branch(c_q, c_kv, k_pe, q_norm_w, kv_norm_w, w_uq, w_ukv)
    y = jax.nn.sigmoid(g_a) * y_a + jax.nn.sigmoid(g_b) * y_b
    return y @ w_out


def setup_inputs(seed: int = 0) -> dict:
    key = jax.random.key(seed)
    ks = jax.random.split(key, 24)
    nrm = lambda k, shape, s: jax.random.normal(k, shape, jnp.float32) * s
    gain = lambda k, n: 1.0 + nrm(k, (DEPTH, n), 0.05)
    gate_b = jnp.concatenate([nrm(ks[10], (DEPTH, MLSTM_HEADS), 0.1),
                              3.0 + nrm(ks[11], (DEPTH, MLSTM_HEADS), 0.5)], axis=-1)
    return {
        'x': nrm(ks[0], (BATCH, SEQ, D_MODEL), 1.0),
        'c': nrm(ks[1], (BATCH, D_MODEL), 1.0),
        'w_ada': nrm(ks[2], (DEPTH, D_MODEL, N_MOD * D_MODEL), 0.3 * D_MODEL ** -0.5),
        'b_ada': nrm(ks[3], (DEPTH, N_MOD * D_MODEL), 0.02),
        'norm_pre_mix': gain(ks[4], D_MODEL),
        'norm_post_mix': gain(ks[5], D_MODEL),
        'norm_pre_mlp': gain(ks[6], D_MODEL),
        'norm_post_mlp': gain(ks[7], D_MODEL),
        'w_in': nrm(ks[8], (DEPTH, D_MODEL, D_IN), D_MODEL ** -0.5),
        'mlstm_conv_w': nrm(ks[9], (DEPTH, CONV_WIDTH, 2 * MA_QK), CONV_WIDTH ** -0.5),
        'mlstm_conv_b': nrm(ks[12], (DEPTH, 2 * MA_QK), 0.02),
        'mlstm_gate_b': gate_b,
        'mlstm_head_norm': gain(ks[13], MA_V),
        'mla_q_norm': gain(ks[14], MLA_Q_RANK),
        'mla_kv_norm': gain(ks[15], MLA_KV_RANK),
        'w_uq': nrm(ks[16], (DEPTH, MLA_Q_RANK, MLA_HEADS * (MLA_NOPE_DIM + MLA_ROPE_DIM)), MLA_Q_RANK ** -0.5),
        'w_ukv': nrm(ks[17], (DEPTH, MLA_KV_RANK, MLA_HEADS * (MLA_NOPE_DIM + MLA_V_DIM)), MLA_KV_RANK ** -0.5),
        'w_out': nrm(ks[18], (DEPTH, D_MODEL, D_MODEL), D_MODEL ** -0.5),
        'w_ff1': nrm(ks[19], (DEPTH, D_MODEL, D_FF), D_MODEL ** -0.5),
        'w_ff2': nrm(ks[20], (DEPTH, D_FF, D_MODEL), D_FF ** -0.5),
    }


def reference(x, c, w_ada, b_ada, norm_pre_mix, norm_post_mix, norm_pre_mlp, norm_post_mlp,
              w_in, mlstm_conv_w, mlstm_conv_b, mlstm_gate_b, mlstm_head_norm,
              mla_q_norm, mla_kv_norm, w_uq, w_ukv, w_out, w_ff1, w_ff2):
    for l in range(DEPTH):
        mod = jax.nn.silu(c) @ w_ada[l] + b_ada[l]
        sh_m, sc_m, gt_m, sh_f, sc_f, gt_f = [t[:, None, :] for t in jnp.split(mod, N_MOD, axis=-1)]
        h = rmsnorm(x, norm_pre_mix[l]) * (1.0 + sc_m) + sh_m
        y = token_mixer(h, w_in[l], mlstm_conv_w[l], mlstm_conv_b[l], mlstm_gate_b[l],
                        mlstm_head_norm[l], mla_q_norm[l], mla_kv_norm[l],
                        w_uq[l], w_ukv[l], w_out[l])
        x = x + gt_m * rmsnorm(y, norm_post_mix[l])
        h = rmsnorm(x, norm_pre_mlp[l]) * (1.0 + sc_f) + sh_f
        y = jnp.square(jax.nn.relu(h @ w_ff1[l])) @ w_ff2[l]
        x = x + gt_f * rmsnorm(y, norm_post_mlp[l])
    return x
```

```python
import functools

import jax
import jax.numpy as jnp
from jax import lax
from jax.experimental import pallas as pl
from jax.experimental.pallas import tpu as pltpu

F32 = jnp.float32
BF16 = jnp.bfloat16

D_MODEL = 1024
M_HEADS = 4
M_DQK = 128
M_DV = 256
CONV_W = 4
A_HEADS = 8
A_NOPE = 128
A_ROPE = 64
A_DV = 128
A_QRANK = 384
A_KVRANK = 256
ROPE_THETA = 10000.0
D_FF = 4096
EPS = 1e-6
N_MOD = 6
MA_QK = M_HEADS * M_DQK
MA_V = M_HEADS * M_DV

PROJ_TM = 512
HALO = 16
MLSTM_L = 256
ATTN_TQ = 512
ATTN_TK = 256
MLP_TM = 512
FF_CHUNK = 1024
ADA_TN = 1536
VMEM_LIMIT = 56 * 1024 * 1024

NEG = -1e30

_NT = (((1,), (1,)), ((), ()))
_TN = (((0,), (0,)), ((), ()))


def _dot(a, b):
    return jnp.dot(a, b, preferred_element_type=F32)


def _sigmoid(x):
    return 1.0 / (1.0 + jnp.exp(-x))


def _rms(x, w):
    return x * lax.rsqrt(jnp.mean(x * x, axis=-1, keepdims=True) + EPS) * w


def _const_spec(shape):
    nd = len(shape)
    return pl.BlockSpec(shape, lambda *_: (0,) * nd, pipeline_mode=pl.Buffered(1))


def _ada_kernel(c_ref, w_ref, b_ref, o_ref):
    c = c_ref[...]
    a = (c * _sigmoid(c)).astype(BF16)
    o_ref[...] = _dot(a, w_ref[...].astype(BF16)) + b_ref[...]


def _ada(c8, w_ada, b_ada):
    n = w_ada.shape[1]
    return pl.pallas_call(
        _ada_kernel,
        grid=(n // ADA_TN,),
        in_specs=[
            pl.BlockSpec((8, D_MODEL), lambda j: (0, 0)),
            pl.BlockSpec((D_MODEL, ADA_TN), lambda j: (0, j)),
            pl.BlockSpec((1, ADA_TN), lambda j: (0, j)),
        ],
        out_specs=pl.BlockSpec((8, ADA_TN), lambda j: (0, j)),
        out_shape=jax.ShapeDtypeStruct((8, n), F32),
        compiler_params=pltpu.CompilerParams(
            dimension_semantics=("arbitrary",), vmem_limit_bytes=VMEM_LIMIT),
        name="ada",
    )(c8, w_ada, b_ada)


def _proj_kernel(x_ref, xh_ref, mod_ref, npre_ref,
                 wqk_ref, wv_ref, wo_ref, wga_ref, wgb_ref, ws_ref,
                 convw_ref, convb_ref, qks_ref, gbias_ref, qn_ref, kvn_ref,
                 wuq_ref, wukv_ref, cos_ref, sin_ref,
                 qk_out, v_out, og_out, ga_out, gb_out, gates_out,
                 qnope_out, qpe_out, knope_out, vb_out, kpe_out,
                 hext_ref, z_ref, *, tm, tiles_per_seq, q_scale):
    i = pl.program_id(0)
    first = (i % tiles_per_seq) == 0
    shift = mod_ref[0:1, :]
    scale1 = 1.0 + mod_ref[1:2, :]
    w = npre_ref[...]

    def prenorm(xv):
        return _rms(xv, w) * scale1 + shift

    hext_ref[HALO:, :] = prenorm(x_ref[...]).astype(BF16)
    hh = prenorm(xh_ref[...])
    hext_ref[0:HALO, :] = jnp.where(first, 0.0, hh).astype(BF16)

    z_ref[...] = _dot(hext_ref[...], wqk_ref[...])
    acc = convb_ref[...]
    for j in range(CONV_W):
        off = HALO - (CONV_W - 1) + j
        acc = acc + convw_ref[j:j + 1, :] * z_ref[off:off + tm, :]
    qk_out[...] = (acc * _sigmoid(acc) * qks_ref[...]).astype(BF16)

    h = hext_ref[HALO:, :]
    v_out[...] = _dot(h, wv_ref[...]).astype(BF16)
    og_out[...] = _sigmoid(_dot(h, wo_ref[...])).astype(BF16)
    ga_out[...] = _sigmoid(_dot(h, wga_ref[...])).astype(BF16)
    gb_out[...] = _sigmoid(_dot(h, wgb_ref[...])).astype(BF16)

    s = _dot(h, ws_ref[...])
    c_q = s[:, 0:A_QRANK]
    c_kv = s[:, A_QRANK:A_QRANK + A_KVRANK]
    kp = s[:, 640:768]
    kpr = s[:, 768:896]
    g = s[:, 896:1024] + gbias_ref[...]

    lane = lax.broadcasted_iota(jnp.int32, g.shape, 1)
    logsig = jnp.minimum(g, 0.0) - jnp.log1p(jnp.exp(-jnp.abs(g)))
    gates_out[...] = jnp.where(lane < M_HEADS, g, logsig)

    cos = cos_ref[...]
    sin = sin_ref[...]
    kpe_out[...] = (kp * cos + kpr * sin).astype(BF16)

    q = _dot(_rms(c_q, qn_ref[...]).astype(BF16), wuq_ref[...])
    qnope_out[...] = (q[:, 0:1024] * q_scale).astype(BF16)
    cos4 = jnp.concatenate([cos] * 4, axis=1)
    sin4 = jnp.concatenate([sin] * 4, axis=1)
    qpe = q[:, 1024:1536] * cos4 + q[:, 1536:2048] * sin4
    qpe_out[...] = (qpe * q_scale).astype(BF16)

    kv = _dot(_rms(c_kv, kvn_ref[...]).astype(BF16), wukv_ref[...])
    knope_out[...] = kv[:, 0:1024].astype(BF16)
    vb_out[...] = kv[:, 1024:2048].astype(BF16)


def _proj(x2, mod3, npre, wts, convw, convb, qks, gbias, qn, kvn, wuq, wukv,
          cos, sin, seq):
    t = x2.shape[0]
    tm = PROJ_TM
    tiles_per_seq = seq // tm
    hb = tm // HALO
    wqk, wv, wo, wga, wgb, ws = wts
    row = lambda i: (i, 0)
    out_w = [1024, 1024, 1024, 1024, 1024, 128, 1024, 512, 1024, 1024, 128]
    out_dt = [BF16] * 5 + [F32] + [BF16] * 5
    kern = functools.partial(
        _proj_kernel, tm=tm, tiles_per_seq=tiles_per_seq,
        q_scale=float((A_NOPE + A_ROPE) ** -0.5))
    return pl.pallas_call(
        kern,
        grid=(t // tm,),
        in_specs=[
            pl.BlockSpec((tm, D_MODEL), row),
            pl.BlockSpec((HALO, D_MODEL), lambda i: (jnp.maximum(i * hb - 1, 0), 0)),
            pl.BlockSpec((None, N_MOD, D_MODEL), lambda i: (i // tiles_per_seq, 0, 0)),
            _const_spec((1, D_MODEL)),
            _const_spec(wqk.shape), _const_spec(wv.shape), _const_spec(wo.shape),
            _const_spec(wga.shape), _const_spec(wgb.shape), _const_spec(ws.shape),
            _const_spec(convw.shape), _const_spec(convb.shape), _const_spec(qks.shape),
            _const_spec(gbias.shape), _const_spec(qn.shape), _const_spec(kvn.shape),
            _const_spec(wuq.shape), _const_spec(wukv.shape),
            pl.BlockSpec((tm, 128), lambda i: (i % tiles_per_seq, 0)),
            pl.BlockSpec((tm, 128), lambda i: (i % tiles_per_seq, 0)),
        ],
        out_specs=[pl.BlockSpec((tm, n), row) for n in out_w],
        out_shape=[jax.ShapeDtypeStruct((t, n), d) for n, d in zip(out_w, out_dt)],
        scratch_shapes=[
            pltpu.VMEM((tm + HALO, D_MODEL), BF16),
            pltpu.VMEM((tm + HALO, 2 * MA_QK), F32),
        ],
        compiler_params=pltpu.CompilerParams(
            dimension_semantics=("arbitrary",), vmem_limit_bytes=VMEM_LIMIT),
        name="proj",
    )(x2, x2, mod3, npre, wqk, wv, wo, wga, wgb, ws, convw, convb, qks, gbias,
      qn, kvn, wuq, wukv, cos, sin)


def _split3(x):
    hi = x.astype(BF16)
    r = x - hi.astype(F32)
    mid = r.astype(BF16)
    lo = (r - mid.astype(F32)).astype(BF16)
    return hi, mid, lo


def _mlstm_kernel(qk_ref, v_ref, og_ref, gc_ref, gr_ref, hnw_ref, y_ref,
                  ct_ref, n_ref, m_ref, *, L):
    @pl.when(pl.program_id(1) == 0)
    def _():
        ct_ref[...] = jnp.zeros_like(ct_ref)
        n_ref[...] = jnp.zeros_like(n_ref)
        m_ref[...] = jnp.zeros_like(m_ref)

    gc = gc_ref[...]
    gr = gr_ref[...]
    row = lax.broadcasted_iota(jnp.int32, (L, L), 0)
    col = lax.broadcasted_iota(jnp.int32, (L, L), 1)
    causal = col <= row
    tri = causal.astype(BF16)
    bcol_all = sum(_dot(tri, p) for p in _split3(gc))
    brow_all = sum(lax.dot_general(p, tri, _NT, preferred_element_type=F32)
                   for p in _split3(gr))

    for h in range(M_HEADS):
        q = qk_ref[:, h * M_DQK:(h + 1) * M_DQK]
        k = qk_ref[:, MA_QK + h * M_DQK:MA_QK + (h + 1) * M_DQK]
        v = v_ref[:, h * M_DV:(h + 1) * M_DV]
        i_row = gr[h:h + 1, :]
        b_row = brow_all[M_HEADS + h:M_HEADS + h + 1, :]
        i_col = gc[:, h:h + 1]
        b_col = bcol_all[:, M_HEADS + h:M_HEADS + h + 1]
        m_prev = m_ref[h:h + 1, 0:1]

        dmat = jnp.where(causal, b_col - b_row + i_row, NEG)
        a = b_col + m_prev
        m_out = jnp.maximum(a, jnp.max(dmat, axis=1, keepdims=True))
        s = lax.dot_general(q, k, _NT, preferred_element_type=F32) * jnp.exp(dmat - m_out)
        w_inter = jnp.exp(a - m_out)
        ct = ct_ref[h]
        nvec = n_ref[h:h + 1, :]
        num = _dot(s.astype(BF16), v) + w_inter * _dot(q, ct.astype(BF16))
        den = (jnp.sum(s, axis=1, keepdims=True)
               + w_inter * jnp.sum(q.astype(F32) * nvec, axis=1, keepdims=True))
        hh = num / jnp.maximum(jnp.abs(den), jnp.exp(-m_out))
        hh = _rms(hh, hnw_ref[:, h * M_DV:(h + 1) * M_DV])
        og = og_ref[:, h * M_DV:(h + 1) * M_DV].astype(F32)
        y_ref[:, h * M_DV:(h + 1) * M_DV] = (og * hh).astype(BF16)

        b_last = b_col[L - 1:L, :]
        g_prev = b_last + m_prev
        g = b_last - b_col + i_col
        m_new = jnp.maximum(g_prev, jnp.max(g, axis=0, keepdims=True))
        wk = jnp.exp(g - m_new)
        decay = jnp.exp(g_prev - m_new)
        wv = (wk * v.astype(F32)).astype(BF16)
        ct_ref[h] = decay * ct + lax.dot_general(k, wv, _TN, preferred_element_type=F32)
        n_ref[h:h + 1, :] = decay * nvec + jnp.sum(wk * k.astype(F32), axis=0, keepdims=True)
        m_ref[h:h + 1, :] = jnp.broadcast_to(m_new, (1, 128))


def _mlstm(qk, v, og, gc, gr, hnw):
    b, seq, _ = qk.shape
    L = MLSTM_L
    blk = lambda w: pl.BlockSpec((None, L, w), lambda bi, ci: (bi, ci, 0))
    return pl.pallas_call(
        functools.partial(_mlstm_kernel, L=L),
        grid=(b, seq // L),
        in_specs=[
            blk(2 * MA_QK), blk(MA_V), blk(MA_V), blk(128),
            pl.BlockSpec((None, 8, L), lambda bi, ci: (bi, 0, ci)),
            _const_spec(hnw.shape),
        ],
        out_specs=blk(MA_V),
        out_shape=jax.ShapeDtypeStruct((b, seq, MA_V), BF16),
        scratch_shapes=[
            pltpu.VMEM((M_HEADS, M_DQK, M_DV), F32),
            pltpu.VMEM((8, 128), F32),
            pltpu.VMEM((8, 128), F32),
        ],
        compiler_params=pltpu.CompilerParams(
            dimension_semantics=("parallel", "arbitrary"), vmem_limit_bytes=VMEM_LIMIT),
        name="mlstm",
    )(qk, v, og, gc, gr, hnw)


def _attn_kernel(qn_ref, qpe_ref, kn_ref, kpe_ref, v_ref, o_ref,
                 kcat_ref, qcat_ref, m_ref, l_ref, acc_ref, *, tq, tk):
    h = pl.program_id(1)
    qi = pl.program_id(2)

    @pl.when(qi == 0)
    def _():
        kcat_ref[:, 0:A_NOPE] = kn_ref[...]
        kcat_ref[:, A_NOPE:2 * A_NOPE] = kpe_ref[...]

    lane = lax.broadcasted_iota(jnp.int32, (tq, 128), 1)
    own = ((lane // A_ROPE) == (h % 2)).astype(F32)
    qcat_ref[:, 0:A_NOPE] = qn_ref[...]
    qcat_ref[:, A_NOPE:2 * A_NOPE] = (qpe_ref[...].astype(F32) * own).astype(BF16)

    m_ref[...] = jnp.full_like(m_ref, NEG)
    l_ref[...] = jnp.zeros_like(l_ref)
    acc_ref[...] = jnp.zeros_like(acc_ref)

    def step(j, masked):
        start = pl.multiple_of(j * tk, tk)
        kc = kcat_ref[pl.ds(start, tk), :]
        s = lax.dot_general(qcat_ref[...], kc, _NT, preferred_element_type=F32)
        if masked:
            r = lax.broadcasted_iota(jnp.int32, (tq, tk), 0) + qi * tq
            c = lax.broadcasted_iota(jnp.int32, (tq, tk), 1) + j * tk
            s = jnp.where(c <= r, s, NEG)
        m_prev = m_ref[...]
        m_new = jnp.maximum(m_prev, jnp.max(s, axis=1, keepdims=True))
        alpha = jnp.exp(m_prev - m_new)
        p = jnp.exp(s - m_new)
        l_ref[...] = alpha * l_ref[...] + jnp.sum(p, axis=1, keepdims=True)
        acc_ref[...] = alpha * acc_ref[...] + _dot(p.astype(BF16), v_ref[pl.ds(start, tk), :])
        m_ref[...] = m_new

    n_full = qi * (tq // tk)

    def body(j, carry):
        step(j, False)
        return carry

    lax.fori_loop(0, n_full, body, 0)
    for d in range(tq // tk):
        step(n_full + d, True)
    o_ref[...] = (acc_ref[...] / l_ref[...]).astype(BF16)


def _attn(qn, qpe, kn, kpe, v):
    b, seq, _ = qn.shape
    tq, tk = ATTN_TQ, ATTN_TK
    return pl.pallas_call(
        functools.partial(_attn_kernel, tq=tq, tk=tk),
        grid=(b, A_HEADS, seq // tq),
        in_specs=[
            pl.BlockSpec((None, tq, A_NOPE), lambda bi, h, qi: (bi, qi, h)),
            pl.BlockSpec((None, tq, 128), lambda bi, h, qi: (bi, qi, h // 2)),
            pl.BlockSpec((None, seq, A_NOPE), lambda bi, h, qi: (bi, 0, h)),
            pl.BlockSpec((None, seq, 128), lambda bi, h, qi: (bi, 0, 0)),
            pl.BlockSpec((None, seq, A_DV), lambda bi, h, qi: (bi, 0, h)),
        ],
        out_specs=pl.BlockSpec((None, tq, A_DV), lambda bi, h, qi: (bi, qi, h)),
        out_shape=jax.ShapeDtypeStruct((b, seq, A_HEADS * A_DV), BF16),
        scratch_shapes=[
            pltpu.VMEM((seq, 2 * A_NOPE), BF16),
            pltpu.VMEM((tq, 2 * A_NOPE), BF16),
            pltpu.VMEM((tq, 1), F32),
            pltpu.VMEM((tq, 1), F32),
            pltpu.VMEM((tq, A_DV), F32),
        ],
        compiler_params=pltpu.CompilerParams(
            dimension_semantics=("parallel", "parallel", "arbitrary"),
            vmem_limit_bytes=VMEM_LIMIT),
        name="attn",
    )(qn, qpe, kn, kpe, v)


def _mlp_kernel(x_ref, ya_ref, yb_ref, ga_ref, gb_ref, mod_ref,
                npost_ref, npre2_ref, npost2_ref, wout_ref, w1_ref, w2_ref,
                o_ref, acc_ref):
    gate_m = mod_ref[2:3, :]
    shift_f = mod_ref[3:4, :]
    scale_f = 1.0 + mod_ref[4:5, :]
    gate_f = mod_ref[5:6, :]

    y = (ga_ref[...].astype(F32) * ya_ref[...].astype(F32)
         + gb_ref[...].astype(F32) * yb_ref[...].astype(F32))
    yo = _dot(y.astype(BF16), wout_ref[...])
    x1 = x_ref[...] + gate_m * _rms(yo, npost_ref[...])

    h2 = (_rms(x1, npre2_ref[...]) * scale_f + shift_f).astype(BF16)
    for c in range(D_FF // FF_CHUNK):
        u = jnp.maximum(_dot(h2, w1_ref[:, c * FF_CHUNK:(c + 1) * FF_CHUNK]), 0.0)
        part = _dot((u * u).astype(BF16), w2_ref[c * FF_CHUNK:(c + 1) * FF_CHUNK, :])
        if c == 0:
            acc_ref[...] = part
        else:
            acc_ref[...] += part
    o_ref[...] = x1 + gate_f * _rms(acc_ref[...], npost2_ref[...])


def _mlp(x2, ya, yb, ga, gb, mod3, npost, npre2, npost2, wout, w1, w2, seq):
    t = x2.shape[0]
    tm = MLP_TM
    tiles_per_seq = seq // tm
    row = pl.BlockSpec((tm, D_MODEL), lambda i: (i, 0))
    return pl.pallas_call(
        _mlp_kernel,
        grid=(t // tm,),
        in_specs=[
            row, row, row, row, row,
            pl.BlockSpec((None, N_MOD, D_MODEL), lambda i: (i // tiles_per_seq, 0, 0)),
            _const_spec((1, D_MODEL)), _const_spec((1, D_MODEL)), _const_spec((1, D_MODEL)),
            _const_spec(wout.shape), _const_spec(w1.shape), _const_spec(w2.shape),
        ],
        out_specs=row,
        out_shape=jax.ShapeDtypeStruct((t, D_MODEL), F32),
        scratch_shapes=[pltpu.VMEM((tm, D_MODEL), F32)],
        compiler_params=pltpu.CompilerParams(
            dimension_semantics=("arbitrary",), vmem_limit_bytes=VMEM_LIMIT),
        name="mlp",
    )(x2, ya, yb, ga, gb, mod3, npost, npre2, npost2, wout, w1, w2)


def _rot_half_cols(w):
    half = w.shape[-1] // 2
    return jnp.concatenate([-w[..., half:], w[..., :half]], axis=-1)


def _prep_in_weights(w_in):
    o = 0
    parts = []
    for n in (MA_QK, MA_QK, MA_V, MA_V, M_HEADS, M_HEADS, A_QRANK, A_KVRANK, A_ROPE,
              D_MODEL, D_MODEL):
        parts.append(w_in[:, o:o + n])
        o += n
    w_q, w_k, w_v, w_o, w_i, w_f, w_cq, w_ckv, w_kpe, w_ga, w_gb = parts
    kpe_rot = _rot_half_cols(w_kpe)
    pad = jnp.zeros((w_in.shape[0], 128 - 2 * M_HEADS), w_in.dtype)
    w_s = jnp.concatenate([w_cq, w_ckv, w_kpe, w_kpe, kpe_rot, kpe_rot, w_i, w_f, pad], axis=1)
    w_qk = jnp.concatenate([w_q, w_k], axis=1)
    return tuple(t.astype(BF16) for t in (w_qk, w_v, w_o, w_ga, w_gb, w_s))


def _prep_mla_weights(w_uq, w_ukv):
    r = w_uq.reshape(A_QRANK, A_HEADS, A_NOPE + A_ROPE)
    nope = r[:, :, :A_NOPE].reshape(A_QRANK, A_HEADS * A_NOPE)
    pe = r[:, :, A_NOPE:]
    pe_rot = _rot_half_cols(pe).reshape(A_QRANK, A_HEADS * A_ROPE)
    pe = pe.reshape(A_QRANK, A_HEADS * A_ROPE)
    wuq = jnp.concatenate([nope, pe, pe_rot], axis=1).astype(BF16)
    r = w_ukv.reshape(A_KVRANK, A_HEADS, A_NOPE + A_DV)
    wukv = jnp.concatenate([r[:, :, :A_NOPE].reshape(A_KVRANK, -1),
                            r[:, :, A_NOPE:].reshape(A_KVRANK, -1)], axis=1).astype(BF16)
    return wuq, wukv


def _rope_tables(seq):
    half = A_ROPE // 2
    inv_freq = ROPE_THETA ** (-jnp.arange(half, dtype=F32) / half)
    ang = jnp.arange(seq, dtype=jnp.int32).astype(F32)[:, None] * inv_freq[None, :]
    reps = 128 // half
    return jnp.tile(jnp.cos(ang), (1, reps)), jnp.tile(jnp.sin(ang), (1, reps))


def kernel(x, c, w_ada, b_ada, norm_pre_mix, norm_post_mix, norm_pre_mlp, norm_post_mlp,
           w_in, mlstm_conv_w, mlstm_conv_b, mlstm_gate_b, mlstm_head_norm,
           mla_q_norm, mla_kv_norm, w_uq, w_ukv, w_out, w_ff1, w_ff2):
    bsz, seq, d = x.shape
    depth = w_ada.shape[0]
    cos, sin = _rope_tables(seq)
    c8 = jnp.pad(c, ((0, 8 - bsz), (0, 0)))
    qks = jnp.concatenate([jnp.ones((1, MA_QK), F32),
                           jnp.full((1, MA_QK), M_DQK ** -0.5, F32)], axis=1)
    row = lambda a: a.reshape(1, -1)
    x2 = x.reshape(bsz * seq, d)
    for l in range(depth):
        mod = _ada(c8, w_ada[l], row(b_ada[l]))[:bsz]
        mod3 = mod.reshape(bsz, N_MOD, d)
        wts = _prep_in_weights(w_in[l])
        wuq, wukv = _prep_mla_weights(w_uq[l], w_ukv[l])
        gbias = jnp.pad(row(mlstm_gate_b[l]), ((0, 0), (0, 128 - 2 * M_HEADS)))
        (qk, v_a, og, ga, gb, gates, qnope, qpe, knope, v_b, kpe) = _proj(
            x2, mod3, row(norm_pre_mix[l]), wts, mlstm_conv_w[l], row(mlstm_conv_b[l]),
            qks, gbias, row(mla_q_norm[l]), row(mla_kv_norm[l]), wuq, wukv, cos, sin, seq)
        b3 = lambda a: a.reshape(bsz, seq, a.shape[-1])
        gc = b3(gates)
        gr = jnp.transpose(gc[:, :, :8], (0, 2, 1))
        y_a = _mlstm(b3(qk), b3(v_a), b3(og), gc, gr, row(mlstm_head_norm[l]))
        y_b = _attn(b3(qnope), b3(qpe), b3(knope), b3(kpe), b3(v_b))
        x2 = _mlp(x2, y_a.reshape(bsz * seq, d), y_b.reshape(bsz * seq, d), ga, gb, mod3,
                  row(norm_post_mix[l]), row(norm_pre_mlp[l]), row(norm_post_mlp[l]),
                  w_out[l].astype(BF16), w_ff1[l].astype(BF16), w_ff2[l].astype(BF16), seq)
    return x2.reshape(bsz, seq, d)
```

```python
import functools

import jax
import jax.numpy as jnp
from jax import lax
from jax.experimental import pallas as pl
from jax.experimental.pallas import tpu as pltpu

F32 = jnp.float32
BF16 = jnp.bfloat16

D_MODEL = 1024
M_HEADS = 4
M_DQK = 128
M_DV = 256
CONV_W = 4
A_HEADS = 8
A_NOPE = 128
A_ROPE = 64
A_DV = 128
A_QRANK = 384
A_KVRANK = 256
ROPE_THETA = 10000.0
D_FF = 4096
EPS = 1e-6
N_MOD = 6
MA_QK = M_HEADS * M_DQK
MA_V = M_HEADS * M_DV

PROJ_TM = 512
HALO = 16
MLSTM_L = 256
ATTN_TQ = 512
ATTN_TK = 512
MLP_TM = 512
FF_CHUNK = 1024
ADA_TN = 1536
VMEM_LIMIT = 56 * 1024 * 1024

NEG = -1e30
LOG2E = 1.4426950408889634

_NT = (((1,), (1,)), ((), ()))
_TN = (((0,), (0,)), ((), ()))


def _dot(a, b):
    return jnp.dot(a, b, preferred_element_type=F32)


def _sigmoid(x):
    return 1.0 / (1.0 + jnp.exp(-x))


def _rms(x, w):
    return x * lax.rsqrt(jnp.mean(x * x, axis=-1, keepdims=True) + EPS) * w


def _const_spec(shape):
    nd = len(shape)
    return pl.BlockSpec(shape, lambda *_: (0,) * nd, pipeline_mode=pl.Buffered(1))


def _ada_kernel(c_ref, w_ref, b_ref, o_ref):
    c = c_ref[...]
    a = (c * _sigmoid(c)).astype(BF16)
    o_ref[...] = _dot(a, w_ref[...].astype(BF16)) + b_ref[...]


def _ada(c8, w_ada, b_ada):
    n = w_ada.shape[1]
    return pl.pallas_call(
        _ada_kernel,
        grid=(n // ADA_TN,),
        in_specs=[
            pl.BlockSpec((8, D_MODEL), lambda j: (0, 0)),
            pl.BlockSpec((D_MODEL, ADA_TN), lambda j: (0, j)),
            pl.BlockSpec((1, ADA_TN), lambda j: (0, j)),
        ],
        out_specs=pl.BlockSpec((8, ADA_TN), lambda j: (0, j)),
        out_shape=jax.ShapeDtypeStruct((8, n), F32),
        compiler_params=pltpu.CompilerParams(
            dimension_semantics=("arbitrary",), vmem_limit_bytes=VMEM_LIMIT),
        name="ada",
    )(c8, w_ada, b_ada)


def _proj_kernel(x_ref, xh_ref, mod_ref, npre_ref,
                 wqk_ref, wv_ref, wo_ref, wga_ref, wgb_ref, ws_ref,
                 convw_ref, convb_ref, qks_ref, gbias_ref, qn_ref, kvn_ref,
                 wuq_ref, wukv_ref, cos_ref, sin_ref,
                 qk_out, v_out, og_out, ga_out, gb_out, gates_out,
                 qnope_out, qpe_out, knope_out, vb_out, kpe_out,
                 hext_ref, z_ref, *, tm, tiles_per_seq, q_scale):
    i = pl.program_id(0)
    first = (i % tiles_per_seq) == 0
    shift = mod_ref[0:1, :]
    scale1 = 1.0 + mod_ref[1:2, :]
    w = npre_ref[...]

    def prenorm(xv):
        return _rms(xv, w) * scale1 + shift

    hext_ref[HALO:, :] = prenorm(x_ref[...]).astype(BF16)
    hh = prenorm(xh_ref[...])
    hext_ref[0:HALO, :] = jnp.where(first, 0.0, hh).astype(BF16)

    z_ref[...] = _dot(hext_ref[...], wqk_ref[...])
    acc = convb_ref[...]
    for j in range(CONV_W):
        off = HALO - (CONV_W - 1) + j
        acc = acc + convw_ref[j:j + 1, :] * z_ref[off:off + tm, :]
    qk_out[...] = (acc * _sigmoid(acc) * qks_ref[...]).astype(BF16)

    h = hext_ref[HALO:, :]
    v_out[...] = _dot(h, wv_ref[...]).astype(BF16)
    og_out[...] = _sigmoid(_dot(h, wo_ref[...])).astype(BF16)
    ga_out[...] = _sigmoid(_dot(h, wga_ref[...])).astype(BF16)
    gb_out[...] = _sigmoid(_dot(h, wgb_ref[...])).astype(BF16)

    s = _dot(h, ws_ref[...])
    c_q = s[:, 0:A_QRANK]
    c_kv = s[:, A_QRANK:A_QRANK + A_KVRANK]
    kp = s[:, 640:768]
    kpr = s[:, 768:896]
    g = s[:, 896:1024] + gbias_ref[...]

    lane = lax.broadcasted_iota(jnp.int32, g.shape, 1)
    logsig = jnp.minimum(g, 0.0) - jnp.log1p(jnp.exp(-jnp.abs(g)))
    gates_out[...] = jnp.where(lane < M_HEADS, g, logsig)

    cos = cos_ref[...]
    sin = sin_ref[...]
    kpe_out[...] = (kp * cos + kpr * sin).astype(BF16)

    q = _dot(_rms(c_q, qn_ref[...]).astype(BF16), wuq_ref[...])
    qnope_out[...] = (q[:, 0:1024] * q_scale).astype(BF16)
    cos4 = jnp.concatenate([cos] * 4, axis=1)
    sin4 = jnp.concatenate([sin] * 4, axis=1)
    qpe = q[:, 1024:1536] * cos4 + q[:, 1536:2048] * sin4
    qpe_out[...] = (qpe * q_scale).astype(BF16)

    kv = _dot(_rms(c_kv, kvn_ref[...]).astype(BF16), wukv_ref[...])
    knope_out[...] = kv[:, 0:1024].astype(BF16)
    vb_out[...] = kv[:, 1024:2048].astype(BF16)


def _proj(x2, mod3, npre, wts, convw, convb, qks, gbias, qn, kvn, wuq, wukv,
          cos, sin, seq):
    t = x2.shape[0]
    tm = PROJ_TM
    tiles_per_seq = seq // tm
    hb = tm // HALO
    wqk, wv, wo, wga, wgb, ws = wts
    row = lambda i: (i, 0)
    out_w = [1024, 1024, 1024, 1024, 1024, 128, 1024, 512, 1024, 1024, 128]
    out_dt = [BF16] * 5 + [F32] + [BF16] * 5
    kern = functools.partial(
        _proj_kernel, tm=tm, tiles_per_seq=tiles_per_seq,
        q_scale=float((A_NOPE + A_ROPE) ** -0.5 * LOG2E))
    return pl.pallas_call(
        kern,
        grid=(t // tm,),
        in_specs=[
            pl.BlockSpec((tm, D_MODEL), row),
            pl.BlockSpec((HALO, D_MODEL), lambda i: (jnp.maximum(i * hb - 1, 0), 0)),
            pl.BlockSpec((None, N_MOD, D_MODEL), lambda i: (i // tiles_per_seq, 0, 0)),
            _const_spec((1, D_MODEL)),
            _const_spec(wqk.shape), _const_spec(wv.shape), _const_spec(wo.shape),
            _const_spec(wga.shape), _const_spec(wgb.shape), _const_spec(ws.shape),
            _const_spec(convw.shape), _const_spec(convb.shape), _const_spec(qks.shape),
            _const_spec(gbias.shape), _const_spec(qn.shape), _const_spec(kvn.shape),
            _const_spec(wuq.shape), _const_spec(wukv.shape),
            pl.BlockSpec((tm, 128), lambda i: (i % tiles_per_seq, 0)),
            pl.BlockSpec((tm, 128), lambda i: (i % tiles_per_seq, 0)),
        ],
        out_specs=[pl.BlockSpec((tm, n), row) for n in out_w],
        out_shape=[jax.ShapeDtypeStruct((t, n), d) for n, d in zip(out_w, out_dt)],
        scratch_shapes=[
            pltpu.VMEM((tm + HALO, D_MODEL), BF16),
            pltpu.VMEM((tm + HALO, 2 * MA_QK), F32),
        ],
        compiler_params=pltpu.CompilerParams(
            dimension_semantics=("arbitrary",), vmem_limit_bytes=VMEM_LIMIT),
        name="proj",
    )(x2, x2, mod3, npre, wqk, wv, wo, wga, wgb, ws, convw, convb, qks, gbias,
      qn, kvn, wuq, wukv, cos, sin)


def _split3(x):
    hi = x.astype(BF16)
    r = x - hi.astype(F32)
    mid = r.astype(BF16)
    lo = (r - mid.astype(F32)).astype(BF16)
    return hi, mid, lo


def _mlstm_kernel(qk_ref, v_ref, og_ref, gc_ref, gr_ref, hnw_ref, y_ref,
                  ct_ref, n_ref, m_ref, *, L):
    @pl.when(pl.program_id(1) == 0)
    def _():
        ct_ref[...] = jnp.zeros_like(ct_ref)
        n_ref[...] = jnp.zeros_like(n_ref)
        m_ref[...] = jnp.zeros_like(m_ref)

    gc = gc_ref[...]
    gr = gr_ref[...]
    row = lax.broadcasted_iota(jnp.int32, (L, L), 0)
    col = lax.broadcasted_iota(jnp.int32, (L, L), 1)
    causal = col <= row
    tri = causal.astype(BF16)
    bcol_all = sum(_dot(tri, p) for p in _split3(gc))
    brow_all = sum(lax.dot_general(p, tri, _NT, preferred_element_type=F32)
                   for p in _split3(gr))

    for h in range(M_HEADS):
        q = qk_ref[:, h * M_DQK:(h + 1) * M_DQK]
        k = qk_ref[:, MA_QK + h * M_DQK:MA_QK + (h + 1) * M_DQK]
        v = v_ref[:, h * M_DV:(h + 1) * M_DV]
        i_row = gr[h:h + 1, :]
        b_row = brow_all[M_HEADS + h:M_HEADS + h + 1, :]
        i_col = gc[:, h:h + 1]
        b_col = bcol_all[:, M_HEADS + h:M_HEADS + h + 1]
        m_prev = m_ref[h:h + 1, 0:1]

        dmat = jnp.where(causal, b_col - b_row + i_row, NEG)
        a = b_col + m_prev
        m_out = jnp.maximum(a, jnp.max(dmat, axis=1, keepdims=True))
        s = lax.dot_general(q, k, _NT, preferred_element_type=F32) * jnp.exp(dmat - m_out)
        w_inter = jnp.exp(a - m_out)
        ct = ct_ref[h]
        nvec = n_ref[h:h + 1, :]
        num = _dot(s.astype(BF16), v) + w_inter * _dot(q, ct.astype(BF16))
        den = (jnp.sum(s, axis=1, keepdims=True)
               + w_inter * jnp.sum(q.astype(F32) * nvec, axis=1, keepdims=True))
        hh = num / jnp.maximum(jnp.abs(den), jnp.exp(-m_out))
        hh = _rms(hh, hnw_ref[:, h * M_DV:(h + 1) * M_DV])
        og = og_ref[:, h * M_DV:(h + 1) * M_DV].astype(F32)
        y_ref[:, h * M_DV:(h + 1) * M_DV] = (og * hh).astype(BF16)

        b_last = b_col[L - 1:L, :]
        g_prev = b_last + m_prev
        g = b_last - b_col + i_col
        m_new = jnp.maximum(g_prev, jnp.max(g, axis=0, keepdims=True))
        wk = jnp.exp(g - m_new)
        decay = jnp.exp(g_prev - m_new)
        wv = (wk * v.astype(F32)).astype(BF16)
        ct_ref[h] = decay * ct + lax.dot_general(k, wv, _TN, preferred_element_type=F32)
        n_ref[h:h + 1, :] = decay * nvec + jnp.sum(wk * k.astype(F32), axis=0, keepdims=True)
        m_ref[h:h + 1, :] = jnp.broadcast_to(m_new, (1, 128))


def _mlstm(qk, v, og, gc, gr, hnw):
    b, seq, _ = qk.shape
    L = MLSTM_L
    blk = lambda w: pl.BlockSpec((None, L, w), lambda bi, ci: (bi, ci, 0))
    return pl.pallas_call(
        functools.partial(_mlstm_kernel, L=L),
        grid=(b, seq // L),
        in_specs=[
            blk(2 * MA_QK), blk(MA_V), blk(MA_V), blk(128),
            pl.BlockSpec((None, 8, L), lambda bi, ci: (bi, 0, ci)),
            _const_spec(hnw.shape),
        ],
        out_specs=blk(MA_V),
        out_shape=jax.ShapeDtypeStruct((b, seq, MA_V), BF16),
        scratch_shapes=[
            pltpu.VMEM((M_HEADS, M_DQK, M_DV), F32),
            pltpu.VMEM((8, 128), F32),
            pltpu.VMEM((8, 128), F32),
        ],
        compiler_params=pltpu.CompilerParams(
            dimension_semantics=("parallel", "arbitrary"), vmem_limit_bytes=VMEM_LIMIT),
        name="mlstm",
    )(qk, v, og, gc, gr, hnw)


def _attn_kernel(qn_ref, qpe_ref, kn_ref, kpe_ref, v_ref, o_ref,
                 kcat_ref, vaug_ref, qcat_ref, m_ref, acc_ref, sa_ref, sb_ref, *, tq, tk):
    assert tq == tk
    h = pl.program_id(1)
    qi = pl.program_id(2)

    @pl.when(qi == 0)
    def _():
        kcat_ref[:, 0:A_NOPE] = kn_ref[...]
        kcat_ref[:, A_NOPE:2 * A_NOPE] = kpe_ref[...]
        vaug_ref[:, 0:A_DV] = v_ref[...]
        vaug_ref[:, A_DV:2 * A_DV] = jnp.ones((v_ref.shape[0], A_DV), BF16)

    lane = lax.broadcasted_iota(jnp.int32, (tq, 128), 1)
    own = ((lane // A_ROPE) == (h % 2)).astype(F32)
    qcat_ref[:, 0:A_NOPE] = qn_ref[...]
    qcat_ref[:, A_NOPE:2 * A_NOPE] = (qpe_ref[...].astype(F32) * own).astype(BF16)

    m_ref[...] = jnp.full_like(m_ref, NEG)
    acc_ref[...] = jnp.zeros_like(acc_ref)
    nt = tk // 128

    def logits(j):
        start = pl.multiple_of(j * tk, tk)
        return lax.dot_general(qcat_ref[...], kcat_ref[pl.ds(start, tk), :], _NT,
                               preferred_element_type=F32)

    def softmax_pv(s_ref, j, masked):
        start = pl.multiple_of(j * tk, tk)
        s = s_ref[...]
        if masked:
            r = lax.broadcasted_iota(jnp.int32, (tq, tk), 0)
            c = lax.broadcasted_iota(jnp.int32, (tq, tk), 1)
            s = jnp.where(c <= r, s, NEG)
        tiles = [s[:, t * 128:(t + 1) * 128] for t in range(nt)]
        mx = functools.reduce(jnp.maximum, tiles)
        m_prev = m_ref[...]
        m_new = jnp.maximum(m_prev, jnp.max(mx, axis=1, keepdims=True))
        alpha = jnp.exp2(m_prev - m_new)
        p = jnp.concatenate([jnp.exp2(t - m_new).astype(BF16) for t in tiles], axis=1)
        pv = _dot(p, vaug_ref[pl.ds(start, tk), :])
        acc_ref[...] = jnp.concatenate([alpha, alpha], axis=1) * acc_ref[...] + pv
        m_ref[...] = m_new

    sa_ref[...] = logits(0)

    def body(i, carry):
        j = 2 * i
        sb_ref[...] = logits(j + 1)
        softmax_pv(sa_ref, j, False)
        sa_ref[...] = logits(j + 2)
        softmax_pv(sb_ref, j + 1, False)
        return carry

    lax.fori_loop(0, qi // 2, body, 0)

    @pl.when(qi % 2 == 1)
    def _():
        sb_ref[...] = logits(qi)
        softmax_pv(sa_ref, qi - 1, False)
        softmax_pv(sb_ref, qi, True)

    @pl.when(qi % 2 == 0)
    def _():
        softmax_pv(sa_ref, qi, True)

    o_ref[...] = (acc_ref[:, 0:A_DV] / acc_ref[:, A_DV:2 * A_DV]).astype(BF16)


def _attn(qn, qpe, kn, kpe, v):
    b, seq, _ = qn.shape
    tq, tk = ATTN_TQ, ATTN_TK
    return pl.pallas_call(
        functools.partial(_attn_kernel, tq=tq, tk=tk),
        grid=(b, A_HEADS, seq // tq),
        in_specs=[
            pl.BlockSpec((None, tq, A_NOPE), lambda bi, h, qi: (bi, qi, h)),
            pl.BlockSpec((None, tq, 128), lambda bi, h, qi: (bi, qi, h // 2)),
            pl.BlockSpec((None, seq, A_NOPE), lambda bi, h, qi: (bi, 0, h)),
            pl.BlockSpec((None, seq, 128), lambda bi, h, qi: (bi, 0, 0)),
            pl.BlockSpec((None, seq, A_DV), lambda bi, h, qi: (bi, 0, h)),
        ],
        out_specs=pl.BlockSpec((None, tq, A_DV), lambda bi, h, qi: (bi, qi, h)),
        out_shape=jax.ShapeDtypeStruct((b, seq, A_HEADS * A_DV), BF16),
        scratch_shapes=[
            pltpu.VMEM((seq, 2 * A_NOPE), BF16),
            pltpu.VMEM((seq, 2 * A_DV), BF16),
            pltpu.VMEM((tq, 2 * A_NOPE), BF16),
            pltpu.VMEM((tq, 128), F32),
            pltpu.VMEM((tq, 2 * A_DV), F32),
            pltpu.VMEM((tq, tk), F32),
            pltpu.VMEM((tq, tk), F32),
        ],
        compiler_params=pltpu.CompilerParams(
            dimension_semantics=("parallel", "parallel", "arbitrary"),
            vmem_limit_bytes=VMEM_LIMIT),
        name="attn",
    )(qn, qpe, kn, kpe, v)


def _mlp_kernel(x_ref, ya_ref, yb_ref, ga_ref, gb_ref, mod_ref,
                npost_ref, npre2_ref, npost2_ref, wout_ref, w1_ref, w2_ref,
                o_ref, acc_ref):
    gate_m = mod_ref[2:3, :]
    shift_f = mod_ref[3:4, :]
    scale_f = 1.0 + mod_ref[4:5, :]
    gate_f = mod_ref[5:6, :]

    y = (ga_ref[...].astype(F32) * ya_ref[...].astype(F32)
         + gb_ref[...].astype(F32) * yb_ref[...].astype(F32))
    yo = _dot(y.astype(BF16), wout_ref[...])
    x1 = x_ref[...] + gate_m * _rms(yo, npost_ref[...])

    h2 = (_rms(x1, npre2_ref[...]) * scale_f + shift_f).astype(BF16)
    for c in range(D_FF // FF_CHUNK):
        u = jnp.maximum(_dot(h2, w1_ref[:, c * FF_CHUNK:(c + 1) * FF_CHUNK]), 0.0)
        part = _dot((u * u).astype(BF16), w2_ref[c * FF_CHUNK:(c + 1) * FF_CHUNK, :])
        if c == 0:
            acc_ref[...] = part
        else:
            acc_ref[...] += part
    o_ref[...] = x1 + gate_f * _rms(acc_ref[...], npost2_ref[...])


def _mlp(x2, ya, yb, ga, gb, mod3, npost, npre2, npost2, wout, w1, w2, seq):
    t = x2.shape[0]
    tm = MLP_TM
    tiles_per_seq = seq // tm
    row = pl.BlockSpec((tm, D_MODEL), lambda i: (i, 0))
    return pl.pallas_call(
        _mlp_kernel,
        grid=(t // tm,),
        in_specs=[
            row, row, row, row, row,
            pl.BlockSpec((None, N_MOD, D_MODEL), lambda i: (i // tiles_per_seq, 0, 0)),
            _const_spec((1, D_MODEL)), _const_spec((1, D_MODEL)), _const_spec((1, D_MODEL)),
            _const_spec(wout.shape), _const_spec(w1.shape), _const_spec(w2.shape),
        ],
        out_specs=row,
        out_shape=jax.ShapeDtypeStruct((t, D_MODEL), F32),
        scratch_shapes=[pltpu.VMEM((tm, D_MODEL), F32)],
        compiler_params=pltpu.CompilerParams(
            dimension_semantics=("arbitrary",), vmem_limit_bytes=VMEM_LIMIT),
        name="mlp",
    )(x2, ya, yb, ga, gb, mod3, npost, npre2, npost2, wout, w1, w2)


def _rot_half_cols(w):
    half = w.shape[-1] // 2
    return jnp.concatenate([-w[..., half:], w[..., :half]], axis=-1)


def _prep_in_weights(w_in):
    o = 0
    parts = []
    for n in (MA_QK, MA_QK, MA_V, MA_V, M_HEADS, M_HEADS, A_QRANK, A_KVRANK, A_ROPE,
              D_MODEL, D_MODEL):
        parts.append(w_in[:, o:o + n])
        o += n
    w_q, w_k, w_v, w_o, w_i, w_f, w_cq, w_ckv, w_kpe, w_ga, w_gb = parts
    kpe_rot = _rot_half_cols(w_kpe)
    pad = jnp.zeros((w_in.shape[0], 128 - 2 * M_HEADS), w_in.dtype)
    w_s = jnp.concatenate([w_cq, w_ckv, w_kpe, w_kpe, kpe_rot, kpe_rot, w_i, w_f, pad], axis=1)
    w_qk = jnp.concatenate([w_q, w_k], axis=1)
    return tuple(t.astype(BF16) for t in (w_qk, w_v, w_o, w_ga, w_gb, w_s))


def _prep_mla_weights(w_uq, w_ukv):
    r = w_uq.reshape(A_QRANK, A_HEADS, A_NOPE + A_ROPE)
    nope = r[:, :, :A_NOPE].reshape(A_QRANK, A_HEADS * A_NOPE)
    pe = r[:, :, A_NOPE:]
    pe_rot = _rot_half_cols(pe).reshape(A_QRANK, A_HEADS * A_ROPE)
    pe = pe.reshape(A_QRANK, A_HEADS * A_ROPE)
    wuq = jnp.concatenate([nope, pe, pe_rot], axis=1).astype(BF16)
    r = w_ukv.reshape(A_KVRANK, A_HEADS, A_NOPE + A_DV)
    wukv = jnp.concatenate([r[:, :, :A_NOPE].reshape(A_KVRANK, -1),
                            r[:, :, A_NOPE:].reshape(A_KVRANK, -1)], axis=1).astype(BF16)
    return wuq, wukv


def _rope_tables(seq):
    half = A_ROPE // 2
    inv_freq = ROPE_THETA ** (-jnp.arange(half, dtype=F32) / half)
    ang = jnp.arange(seq, dtype=jnp.int32).astype(F32)[:, None] * inv_freq[None, :]
    reps = 128 // half
    return jnp.tile(jnp.cos(ang), (1, reps)), jnp.tile(jnp.sin(ang), (1, reps))


def kernel(x, c, w_ada, b_ada, norm_pre_mix, norm_post_mix, norm_pre_mlp, norm_post_mlp,
           w_in, mlstm_conv_w, mlstm_conv_b, mlstm_gate_b, mlstm_head_norm,
           mla_q_norm, mla_kv_norm, w_uq, w_ukv, w_out, w_ff1, w_ff2):
    bsz, seq, d = x.shape
    depth = w_ada.shape[0]
    cos, sin = _rope_tables(seq)
    c8 = jnp.pad(c, ((0, 8 - bsz), (0, 0)))
    qks = jnp.concatenate([jnp.ones((1, MA_QK), F32),
                           jnp.full((1, MA_QK), M_DQK ** -0.5, F32)], axis=1)
    row = lambda a: a.reshape(1, -1)
    x2 = x.reshape(bsz * seq, d)
    for l in range(depth):
        mod = _ada(c8, w_ada[l], row(b_ada[l]))[:bsz]
        mod3 = mod.reshape(bsz, N_MOD, d)
        wts = _prep_in_weights(w_in[l])
        wuq, wukv = _prep_mla_weights(w_uq[l], w_ukv[l])
        gbias = jnp.pad(row(mlstm_gate_b[l]), ((0, 0), (0, 128 - 2 * M_HEADS)))
        (qk, v_a, og, ga, gb, gates, qnope, qpe, knope, v_b, kpe) = _proj(
            x2, mod3, row(norm_pre_mix[l]), wts, mlstm_conv_w[l], row(mlstm_conv_b[l]),
            qks, gbias, row(mla_q_norm[l]), row(mla_kv_norm[l]), wuq, wukv, cos, sin, seq)
        b3 = lambda a: a.reshape(bsz, seq, a.shape[-1])
        gc = b3(gates)
        gr = jnp.transpose(gc[:, :, :8], (0, 2, 1))
        y_a = _mlstm(b3(qk), b3(v_a), b3(og), gc, gr, row(mlstm_head_norm[l]))
        y_b = _attn(b3(qnope), b3(qpe), b3(knope), b3(kpe), b3(v_b))
        x2 = _mlp(x2, y_a.reshape(bsz * seq, d), y_b.reshape(bsz * seq, d), ga, gb, mod3,
                  row(norm_post_mix[l]), row(norm_pre_mlp[l]), row(norm_post_mlp[l]),
                  w_out[l].astype(BF16), w_ff1[l].astype(BF16), w_ff2[l].astype(BF16), seq)
    return x2.reshape(bsz, seq, d)
```

```python
import functools

import jax
import jax.numpy as jnp
from jax import lax
from jax.experimental import pallas as pl
from jax.experimental.pallas import tpu as pltpu

F32 = jnp.float32
BF16 = jnp.bfloat16

D_MODEL = 1024
M_HEADS = 4
M_DQK = 128
M_DV = 256
CONV_W = 4
A_HEADS = 8
A_NOPE = 128
A_ROPE = 64
A_DV = 128
A_QRANK = 384
A_KVRANK = 256
ROPE_THETA = 10000.0
D_FF = 4096
EPS = 1e-6
N_MOD = 6
MA_QK = M_HEADS * M_DQK
MA_V = M_HEADS * M_DV

PROJ_TM = 512
HALO = 16
MLSTM_L = 256
ATTN_TQ = 1024
ATTN_TK = 512
MLP_TM = 512
FF_CHUNK = 1024
ADA_TN = 1536
VMEM_LIMIT = 56 * 1024 * 1024

NEG = -1e30
LOG2E = 1.4426950408889634

_NT = (((1,), (1,)), ((), ()))
_TN = (((0,), (0,)), ((), ()))


def _dot(a, b):
    return jnp.dot(a, b, preferred_element_type=F32)


def _sigmoid(x):
    return 1.0 / (1.0 + jnp.exp(-x))


def _rms(x, w):
    return x * lax.rsqrt(jnp.mean(x * x, axis=-1, keepdims=True) + EPS) * w


def _const_spec(shape):
    nd = len(shape)
    return pl.BlockSpec(shape, lambda *_: (0,) * nd, pipeline_mode=pl.Buffered(1))


def _ada_kernel(c_ref, w_ref, b_ref, o_ref):
    c = c_ref[...]
    a = (c * _sigmoid(c)).astype(BF16)
    o_ref[...] = _dot(a, w_ref[...].astype(BF16)) + b_ref[...]


def _ada(c8, w_ada, b_ada):
    n = w_ada.shape[1]
    return pl.pallas_call(
        _ada_kernel,
        grid=(n // ADA_TN,),
        in_specs=[
            pl.BlockSpec((8, D_MODEL), lambda j: (0, 0)),
            pl.BlockSpec((D_MODEL, ADA_TN), lambda j: (0, j)),
            pl.BlockSpec((1, ADA_TN), lambda j: (0, j)),
        ],
        out_specs=pl.BlockSpec((8, ADA_TN), lambda j: (0, j)),
        out_shape=jax.ShapeDtypeStruct((8, n), F32),
        compiler_params=pltpu.CompilerParams(
            dimension_semantics=("arbitrary",), vmem_limit_bytes=VMEM_LIMIT),
        name="ada",
    )(c8, w_ada, b_ada)


def _proj_kernel(x_ref, xh_ref, mod_ref, npre_ref,
                 wqk_ref, wv_ref, wo_ref, wga_ref, wgb_ref, ws_ref,
                 convw_ref, convb_ref, qks_ref, gbias_ref, qn_ref, kvn_ref,
                 wuq_ref, wukv_ref, cos_ref, sin_ref,
                 qk_out, v_out, og_out, ga_out, gb_out, gates_out,
                 qnope_out, qpe_out, knope_out, vb_out, kpe_out,
                 hext_ref, z_ref, *, tm, tiles_per_seq, q_scale):
    i = pl.program_id(0)
    first = (i % tiles_per_seq) == 0
    shift = mod_ref[0:1, :]
    scale1 = 1.0 + mod_ref[1:2, :]
    w = npre_ref[...]

    def prenorm(xv):
        return _rms(xv, w) * scale1 + shift

    hext_ref[HALO:, :] = prenorm(x_ref[...]).astype(BF16)
    hh = prenorm(xh_ref[...])
    hext_ref[0:HALO, :] = jnp.where(first, 0.0, hh).astype(BF16)

    z_ref[...] = _dot(hext_ref[...], wqk_ref[...])
    acc = convb_ref[...]
    for j in range(CONV_W):
        off = HALO - (CONV_W - 1) + j
        acc = acc + convw_ref[j:j + 1, :] * z_ref[off:off + tm, :]
    qk_out[...] = (acc * _sigmoid(acc) * qks_ref[...]).astype(BF16)

    h = hext_ref[HALO:, :]
    v_out[...] = _dot(h, wv_ref[...]).astype(BF16)
    og_out[...] = _sigmoid(_dot(h, wo_ref[...])).astype(BF16)
    ga_out[...] = _sigmoid(_dot(h, wga_ref[...])).astype(BF16)
    gb_out[...] = _sigmoid(_dot(h, wgb_ref[...])).astype(BF16)

    s = _dot(h, ws_ref[...])
    c_q = s[:, 0:A_QRANK]
    c_kv = s[:, A_QRANK:A_QRANK + A_KVRANK]
    kp = s[:, 640:768]
    kpr = s[:, 768:896]
    g = s[:, 896:1024] + gbias_ref[...]

    lane = lax.broadcasted_iota(jnp.int32, g.shape, 1)
    logsig = jnp.minimum(g, 0.0) - jnp.log1p(jnp.exp(-jnp.abs(g)))
    gates_out[...] = jnp.where(lane < M_HEADS, g, logsig)

    cos = cos_ref[...]
    sin = sin_ref[...]
    kpe_out[...] = (kp * cos + kpr * sin).astype(BF16)

    q = _dot(_rms(c_q, qn_ref[...]).astype(BF16), wuq_ref[...])
    qnope_out[...] = (q[:, 0:1024] * q_scale).astype(BF16)
    cos4 = jnp.concatenate([cos] * 4, axis=1)
    sin4 = jnp.concatenate([sin] * 4, axis=1)
    qpe = q[:, 1024:1536] * cos4 + q[:, 1536:2048] * sin4
    qpe_out[...] = (qpe * q_scale).astype(BF16)

    kv = _dot(_rms(c_kv, kvn_ref[...]).astype(BF16), wukv_ref[...])
    knope_out[...] = kv[:, 0:1024].astype(BF16)
    vb_out[...] = kv[:, 1024:2048].astype(BF16)


def _proj(x2, mod3, npre, wts, convw, convb, qks, gbias, qn, kvn, wuq, wukv,
          cos, sin, seq):
    t = x2.shape[0]
    tm = PROJ_TM
    tiles_per_seq = seq // tm
    hb = tm // HALO
    wqk, wv, wo, wga, wgb, ws = wts
    row = lambda i: (i, 0)
    out_w = [1024, 1024, 1024, 1024, 1024, 128, 1024, 512, 1024, 1024, 128]
    out_dt = [BF16] * 5 + [F32] + [BF16] * 5
    kern = functools.partial(
        _proj_kernel, tm=tm, tiles_per_seq=tiles_per_seq,
        q_scale=float((A_NOPE + A_ROPE) ** -0.5 * LOG2E))
    return pl.pallas_call(
        kern,
        grid=(t // tm,),
        in_specs=[
            pl.BlockSpec((tm, D_MODEL), row),
            pl.BlockSpec((HALO, D_MODEL), lambda i: (jnp.maximum(i * hb - 1, 0), 0)),
            pl.BlockSpec((None, N_MOD, D_MODEL), lambda i: (i // tiles_per_seq, 0, 0)),
            _const_spec((1, D_MODEL)),
            _const_spec(wqk.shape), _const_spec(wv.shape), _const_spec(wo.shape),
            _const_spec(wga.shape), _const_spec(wgb.shape), _const_spec(ws.shape),
            _const_spec(convw.shape), _const_spec(convb.shape), _const_spec(qks.shape),
            _const_spec(gbias.shape), _const_spec(qn.shape), _const_spec(kvn.shape),
            _const_spec(wuq.shape), _const_spec(wukv.shape),
            pl.BlockSpec((tm, 128), lambda i: (i % tiles_per_seq, 0)),
            pl.BlockSpec((tm, 128), lambda i: (i % tiles_per_seq, 0)),
        ],
        out_specs=[pl.BlockSpec((tm, n), row) for n in out_w],
        out_shape=[jax.ShapeDtypeStruct((t, n), d) for n, d in zip(out_w, out_dt)],
        scratch_shapes=[
            pltpu.VMEM((tm + HALO, D_MODEL), BF16),
            pltpu.VMEM((tm + HALO, 2 * MA_QK), F32),
        ],
        compiler_params=pltpu.CompilerParams(
            dimension_semantics=("arbitrary",), vmem_limit_bytes=VMEM_LIMIT),
        name="proj",
    )(x2, x2, mod3, npre, wqk, wv, wo, wga, wgb, ws, convw, convb, qks, gbias,
      qn, kvn, wuq, wukv, cos, sin)


def _split3(x):
    hi = x.astype(BF16)
    r = x - hi.astype(F32)
    mid = r.astype(BF16)
    lo = (r - mid.astype(F32)).astype(BF16)
    return hi, mid, lo


def _mlstm_kernel(qk_ref, v_ref, og_ref, gc_ref, gr_ref, hnw_ref, y_ref,
                  ct_ref, n_ref, m_ref, *, L):
    @pl.when(pl.program_id(1) == 0)
    def _():
        ct_ref[...] = jnp.zeros_like(ct_ref)
        n_ref[...] = jnp.zeros_like(n_ref)
        m_ref[...] = jnp.zeros_like(m_ref)

    gc = gc_ref[...]
    gr = gr_ref[...]
    row = lax.broadcasted_iota(jnp.int32, (L, L), 0)
    col = lax.broadcasted_iota(jnp.int32, (L, L), 1)
    causal = col <= row
    tri = causal.astype(BF16)
    bcol_all = sum(_dot(tri, p) for p in _split3(gc))
    brow_all = sum(lax.dot_general(p, tri, _NT, preferred_element_type=F32)
                   for p in _split3(gr))

    for h in range(M_HEADS):
        q = qk_ref[:, h * M_DQK:(h + 1) * M_DQK]
        k = qk_ref[:, MA_QK + h * M_DQK:MA_QK + (h + 1) * M_DQK]
        v = v_ref[:, h * M_DV:(h + 1) * M_DV]
        i_row = gr[h:h + 1, :]
        b_row = brow_all[M_HEADS + h:M_HEADS + h + 1, :]
        i_col = gc[:, h:h + 1]
        b_col = bcol_all[:, M_HEADS + h:M_HEADS + h + 1]
        m_prev = m_ref[h:h + 1, 0:1]

        dmat = jnp.where(causal, b_col - b_row + i_row, NEG)
        a = b_col + m_prev
        m_out = jnp.maximum(a, jnp.max(dmat, axis=1, keepdims=True))
        s = lax.dot_general(q, k, _NT, preferred_element_type=F32) * jnp.exp(dmat - m_out)
        w_inter = jnp.exp(a - m_out)
        ct = ct_ref[h]
        nvec = n_ref[h:h + 1, :]
        num = _dot(s.astype(BF16), v) + w_inter * _dot(q, ct.astype(BF16))
        den = (jnp.sum(s, axis=1, keepdims=True)
               + w_inter * jnp.sum(q.astype(F32) * nvec, axis=1, keepdims=True))
        hh = num / jnp.maximum(jnp.abs(den), jnp.exp(-m_out))
        hh = _rms(hh, hnw_ref[:, h * M_DV:(h + 1) * M_DV])
        og = og_ref[:, h * M_DV:(h + 1) * M_DV].astype(F32)
        y_ref[:, h * M_DV:(h + 1) * M_DV] = (og * hh).astype(BF16)

        b_last = b_col[L - 1:L, :]
        g_prev = b_last + m_prev
        g = b_last - b_col + i_col
        m_new = jnp.maximum(g_prev, jnp.max(g, axis=0, keepdims=True))
        wk = jnp.exp(g - m_new)
        decay = jnp.exp(g_prev - m_new)
        wv = (wk * v.astype(F32)).astype(BF16)
        ct_ref[h] = decay * ct + lax.dot_general(k, wv, _TN, preferred_element_type=F32)
        n_ref[h:h + 1, :] = decay * nvec + jnp.sum(wk * k.astype(F32), axis=0, keepdims=True)
        m_ref[h:h + 1, :] = jnp.broadcast_to(m_new, (1, 128))


def _mlstm(qk, v, og, gc, gr, hnw):
    b, seq, _ = qk.shape
    L = MLSTM_L
    blk = lambda w: pl.BlockSpec((None, L, w), lambda bi, ci: (bi, ci, 0))
    return pl.pallas_call(
        functools.partial(_mlstm_kernel, L=L),
        grid=(b, seq // L),
        in_specs=[
            blk(2 * MA_QK), blk(MA_V), blk(MA_V), blk(128),
            pl.BlockSpec((None, 8, L), lambda bi, ci: (bi, 0, ci)),
            _const_spec(hnw.shape),
        ],
        out_specs=blk(MA_V),
        out_shape=jax.ShapeDtypeStruct((b, seq, MA_V), BF16),
        scratch_shapes=[
            pltpu.VMEM((M_HEADS, M_DQK, M_DV), F32),
            pltpu.VMEM((8, 128), F32),
            pltpu.VMEM((8, 128), F32),
        ],
        compiler_params=pltpu.CompilerParams(
            dimension_semantics=("parallel", "arbitrary"), vmem_limit_bytes=VMEM_LIMIT),
        name="mlstm",
    )(qk, v, og, gc, gr, hnw)


def _attn_kernel(qn_ref, qpe_ref, kn_ref, kpe_ref, v_ref, o_ref,
                 kcat_ref, vaug_ref, qcat_ref, m_ref, acc_ref, sa_ref, sb_ref, *, tq, tk, seq):
    assert tq == 2 * tk
    h = pl.program_id(1)
    n_tiles = seq // tq
    nt = tk // 128

    kcat_ref[:, 0:A_NOPE] = kn_ref[...]
    kcat_ref[:, A_NOPE:2 * A_NOPE] = kpe_ref[...]
    vaug_ref[:, 0:A_DV] = v_ref[...]
    vaug_ref[:, A_DV:2 * A_DV] = jnp.ones((seq, A_DV), BF16)
    lane = lax.broadcasted_iota(jnp.int32, (seq, 128), 1)
    own = ((lane // A_ROPE) == (h % 2)).astype(F32)
    qcat_ref[:, 0:A_NOPE] = qn_ref[...]
    qcat_ref[:, A_NOPE:2 * A_NOPE] = (qpe_ref[...].astype(F32) * own).astype(BF16)

    def logits(qt, kb, r0=0, nrows=tq):
        q0 = pl.multiple_of(qt * tq + r0, tk)
        k0 = pl.multiple_of(kb * tk, tk)
        return lax.dot_general(qcat_ref[pl.ds(q0, nrows), :], kcat_ref[pl.ds(k0, tk), :],
                               _NT, preferred_element_type=F32)

    def softmax_pv(s, kb, r0, nrows, masked):
        k0 = pl.multiple_of(kb * tk, tk)
        if masked:
            r = lax.broadcasted_iota(jnp.int32, (nrows, tk), 0)
            c = lax.broadcasted_iota(jnp.int32, (nrows, tk), 1)
            s = jnp.where(c <= r, s, NEG)
        tiles = [s[:, t * 128:(t + 1) * 128] for t in range(nt)]
        mx = functools.reduce(jnp.maximum, tiles)
        m_prev = m_ref[r0:r0 + nrows, :]
        m_new = jnp.maximum(m_prev, jnp.max(mx, axis=1, keepdims=True))
        alpha = jnp.exp2(m_prev - m_new)
        p = jnp.concatenate([jnp.exp2(t - m_new).astype(BF16) for t in tiles], axis=1)
        pv = _dot(p, vaug_ref[pl.ds(k0, tk), :])
        acc_ref[r0:r0 + nrows, :] = (jnp.concatenate([alpha, alpha], axis=1)
                                     * acc_ref[r0:r0 + nrows, :] + pv)
        m_ref[r0:r0 + nrows, :] = m_new

    sa_ref[...] = logits(0, 0)

    def tile_body(qt, carry):
        m_ref[...] = jnp.full_like(m_ref, NEG)
        acc_ref[...] = jnp.zeros_like(acc_ref)

        def pair_body(i, c):
            j = 2 * i
            sb_ref[...] = logits(qt, j + 1)
            softmax_pv(sa_ref[...], j, 0, tq, False)
            sa_ref[...] = logits(qt, j + 2)
            softmax_pv(sb_ref[...], j + 1, 0, tq, False)
            return c

        lax.fori_loop(0, qt, pair_body, 0)

        d0 = 2 * qt
        sb_ref[0:tk, :] = logits(qt, d0 + 1, tk, tk)
        softmax_pv(sa_ref[...], d0, 0, tq, True)
        sa_ref[...] = logits(jnp.minimum(qt + 1, n_tiles - 1), 0)
        softmax_pv(sb_ref[0:tk, :], d0 + 1, tk, tk, True)
        o0 = pl.multiple_of(qt * tq, tq)
        o_ref[pl.ds(o0, tq), :] = (acc_ref[:, 0:A_DV] / acc_ref[:, A_DV:2 * A_DV]).astype(BF16)
        return carry

    lax.fori_loop(0, n_tiles, tile_body, 0)


def _attn(qn, qpe, kn, kpe, v):
    b, seq, _ = qn.shape
    tq, tk = ATTN_TQ, ATTN_TK
    return pl.pallas_call(
        functools.partial(_attn_kernel, tq=tq, tk=tk, seq=seq),
        grid=(b, A_HEADS),
        in_specs=[
            pl.BlockSpec((None, seq, A_NOPE), lambda bi, h: (bi, 0, h)),
            pl.BlockSpec((None, seq, 128), lambda bi, h: (bi, 0, h // 2)),
            pl.BlockSpec((None, seq, A_NOPE), lambda bi, h: (bi, 0, h)),
            pl.BlockSpec((None, seq, 128), lambda bi, h: (bi, 0, 0)),
            pl.BlockSpec((None, seq, A_DV), lambda bi, h: (bi, 0, h)),
        ],
        out_specs=pl.BlockSpec((None, seq, A_DV), lambda bi, h: (bi, 0, h)),
        out_shape=jax.ShapeDtypeStruct((b, seq, A_HEADS * A_DV), BF16),
        scratch_shapes=[
            pltpu.VMEM((seq, 2 * A_NOPE), BF16),
            pltpu.VMEM((seq, 2 * A_DV), BF16),
            pltpu.VMEM((seq, 2 * A_NOPE), BF16),
            pltpu.VMEM((tq, 128), F32),
            pltpu.VMEM((tq, 2 * A_DV), F32),
            pltpu.VMEM((tq, tk), F32),
            pltpu.VMEM((tq, tk), F32),
        ],
        compiler_params=pltpu.CompilerParams(
            dimension_semantics=("parallel", "arbitrary"),
            vmem_limit_bytes=VMEM_LIMIT),
        name="attn",
    )(qn, qpe, kn, kpe, v)


def _mlp_kernel(x_ref, ya_ref, yb_ref, ga_ref, gb_ref, mod_ref,
                npost_ref, npre2_ref, npost2_ref, wout_ref, w1_ref, w2_ref,
                o_ref, acc_ref):
    gate_m = mod_ref[2:3, :]
    shift_f = mod_ref[3:4, :]
    scale_f = 1.0 + mod_ref[4:5, :]
    gate_f = mod_ref[5:6, :]

    y = (ga_ref[...].astype(F32) * ya_ref[...].astype(F32)
         + gb_ref[...].astype(F32) * yb_ref[...].astype(F32))
    yo = _dot(y.astype(BF16), wout_ref[...])
    x1 = x_ref[...] + gate_m * _rms(yo, npost_ref[...])

    h2 = (_rms(x1, npre2_ref[...]) * scale_f + shift_f).astype(BF16)
    for c in range(D_FF // FF_CHUNK):
        u = jnp.maximum(_dot(h2, w1_ref[:, c * FF_CHUNK:(c + 1) * FF_CHUNK]), 0.0)
        part = _dot((u * u).astype(BF16), w2_ref[c * FF_CHUNK:(c + 1) * FF_CHUNK, :])
        if c == 0:
            acc_ref[...] = part
        else:
            acc_ref[...] += part
    o_ref[...] = x1 + gate_f * _rms(acc_ref[...], npost2_ref[...])


def _mlp(x2, ya, yb, ga, gb, mod3, npost, npre2, npost2, wout, w1, w2, seq):
    t = x2.shape[0]
    tm = MLP_TM
    tiles_per_seq = seq // tm
    row = pl.BlockSpec((tm, D_MODEL), lambda i: (i, 0))
    return pl.pallas_call(
        _mlp_kernel,
        grid=(t // tm,),
        in_specs=[
            row, row, row, row, row,
            pl.BlockSpec((None, N_MOD, D_MODEL), lambda i: (i // tiles_per_seq, 0, 0)),
            _const_spec((1, D_MODEL)), _const_spec((1, D_MODEL)), _const_spec((1, D_MODEL)),
            _const_spec(wout.shape), _const_spec(w1.shape), _const_spec(w2.shape),
        ],
        out_specs=row,
        out_shape=jax.ShapeDtypeStruct((t, D_MODEL), F32),
        scratch_shapes=[pltpu.VMEM((tm, D_MODEL), F32)],
        compiler_params=pltpu.CompilerParams(
            dimension_semantics=("arbitrary",), vmem_limit_bytes=VMEM_LIMIT),
        name="mlp",
    )(x2, ya, yb, ga, gb, mod3, npost, npre2, npost2, wout, w1, w2)


def _rot_half_cols(w):
    half = w.shape[-1] // 2
    return jnp.concatenate([-w[..., half:], w[..., :half]], axis=-1)


def _prep_in_weights(w_in):
    o = 0
    parts = []
    for n in (MA_QK, MA_QK, MA_V, MA_V, M_HEADS, M_HEADS, A_QRANK, A_KVRANK, A_ROPE,
              D_MODEL, D_MODEL):
        parts.append(w_in[:, o:o + n])
        o += n
    w_q, w_k, w_v, w_o, w_i, w_f, w_cq, w_ckv, w_kpe, w_ga, w_gb = parts
    kpe_rot = _rot_half_cols(w_kpe)
    pad = jnp.zeros((w_in.shape[0], 128 - 2 * M_HEADS), w_in.dtype)
    w_s = jnp.concatenate([w_cq, w_ckv, w_kpe, w_kpe, kpe_rot, kpe_rot, w_i, w_f, pad], axis=1)
    w_qk = jnp.concatenate([w_q, w_k], axis=1)
    return tuple(t.astype(BF16) for t in (w_qk, w_v, w_o, w_ga, w_gb, w_s))


def _prep_mla_weights(w_uq, w_ukv):
    r = w_uq.reshape(A_QRANK, A_HEADS, A_NOPE + A_ROPE)
    nope = r[:, :, :A_NOPE].reshape(A_QRANK, A_HEADS * A_NOPE)
    pe = r[:, :, A_NOPE:]
    pe_rot = _rot_half_cols(pe).reshape(A_QRANK, A_HEADS * A_ROPE)
    pe = pe.reshape(A_QRANK, A_HEADS * A_ROPE)
    wuq = jnp.concatenate([nope, pe, pe_rot], axis=1).astype(BF16)
    r = w_ukv.reshape(A_KVRANK, A_HEADS, A_NOPE + A_DV)
    wukv = jnp.concatenate([r[:, :, :A_NOPE].reshape(A_KVRANK, -1),
                            r[:, :, A_NOPE:].reshape(A_KVRANK, -1)], axis=1).astype(BF16)
    return wuq, wukv


def _rope_tables(seq):
    half = A_ROPE // 2
    inv_freq = ROPE_THETA ** (-jnp.arange(half, dtype=F32) / half)
    ang = jnp.arange(seq, dtype=jnp.int32).astype(F32)[:, None] * inv_freq[None, :]
    reps = 128 // half
    return jnp.tile(jnp.cos(ang), (1, reps)), jnp.tile(jnp.sin(ang), (1, reps))


def kernel(x, c, w_ada, b_ada, norm_pre_mix, norm_post_mix, norm_pre_mlp, norm_post_mlp,
           w_in, mlstm_conv_w, mlstm_conv_b, mlstm_gate_b, mlstm_head_norm,
           mla_q_norm, mla_kv_norm, w_uq, w_ukv, w_out, w_ff1, w_ff2):
    bsz, seq, d = x.shape
    depth = w_ada.shape[0]
    cos, sin = _rope_tables(seq)
    c8 = jnp.pad(c, ((0, 8 - bsz), (0, 0)))
    qks = jnp.concatenate([jnp.ones((1, MA_QK), F32),
                           jnp.full((1, MA_QK), M_DQK ** -0.5, F32)], axis=1)
    row = lambda a: a.reshape(1, -1)
    x2 = x.reshape(bsz * seq, d)
    for l in range(depth):
        mod = _ada(c8, w_ada[l], row(b_ada[l]))[:bsz]
        mod3 = mod.reshape(bsz, N_MOD, d)
        wts = _prep_in_weights(w_in[l])
        wuq, wukv = _prep_mla_weights(w_uq[l], w_ukv[l])
        gbias = jnp.pad(row(mlstm_gate_b[l]), ((0, 0), (0, 128 - 2 * M_HEADS)))
        (qk, v_a, og, ga, gb, gates, qnope, qpe, knope, v_b, kpe) = _proj(
            x2, mod3, row(norm_pre_mix[l]), wts, mlstm_conv_w[l], row(mlstm_conv_b[l]),
            qks, gbias, row(mla_q_norm[l]), row(mla_kv_norm[l]), wuq, wukv, cos, sin, seq)
        b3 = lambda a: a.reshape(bsz, seq, a.shape[-1])
        gc = b3(gates)
        gr = jnp.transpose(gc[:, :, :8], (0, 2, 1))
        y_a = _mlstm(b3(qk), b3(v_a), b3(og), gc, gr, row(mlstm_head_norm[l]))
        y_b = _attn(b3(qnope), b3(qpe), b3(knope), b3(kpe), b3(v_b))
        x2 = _mlp(x2, y_a.reshape(bsz * seq, d), y_b.reshape(bsz * seq, d), ga, gb, mod3,
                  row(norm_post_mix[l]), row(norm_pre_mlp[l]), row(norm_post_mlp[l]),
                  w_out[l].astype(BF16), w_ff1[l].astype(BF16), w_ff2[l].astype(BF16), seq)
    return x2.reshape(bsz, seq, d)
```

```python
import functools

import jax
import jax.numpy as jnp
from jax import lax
from jax.experimental import pallas as pl
from jax.experimental.pallas import tpu as pltpu

F32 = jnp.float32
BF16 = jnp.bfloat16

D_MODEL = 1024
M_HEADS = 4
M_DQK = 128
M_DV = 256
CONV_W = 4
A_HEADS = 8
A_NOPE = 128
A_ROPE = 64
A_DV = 128
A_QRANK = 384
A_KVRANK = 256
ROPE_THETA = 10000.0
D_FF = 4096
EPS = 1e-6
N_MOD = 6
MA_QK = M_HEADS * M_DQK
MA_V = M_HEADS * M_DV

PROJ_TM = 512
HALO = 16
MLSTM_L = 256
ATTN_TQ = 1024
ATTN_TK = 512
MLP_TM = 512
FF_CHUNK = 1024
ADA_TN = 1536
VMEM_LIMIT = 56 * 1024 * 1024

NEG = -1e30
LOG2E = 1.4426950408889634

_NT = (((1,), (1,)), ((), ()))
_TN = (((0,), (0,)), ((), ()))


def _dot(a, b):
    return jnp.dot(a, b, preferred_element_type=F32)


def _sigmoid(x):
    return 1.0 / (1.0 + jnp.exp(-x))


def _rms(x, w):
    return x * lax.rsqrt(jnp.mean(x * x, axis=-1, keepdims=True) + EPS) * w


def _const_spec(shape):
    nd = len(shape)
    return pl.BlockSpec(shape, lambda *_: (0,) * nd, pipeline_mode=pl.Buffered(1))


def _ada_kernel(c_ref, w_ref, b_ref, o_ref):
    c = c_ref[...]
    a = (c * _sigmoid(c)).astype(BF16)
    o_ref[...] = _dot(a, w_ref[...].astype(BF16)) + b_ref[...]


def _ada(c8, w_ada, b_ada):
    n = w_ada.shape[1]
    return pl.pallas_call(
        _ada_kernel,
        grid=(n // ADA_TN,),
        in_specs=[
            pl.BlockSpec((8, D_MODEL), lambda j: (0, 0)),
            pl.BlockSpec((D_MODEL, ADA_TN), lambda j: (0, j)),
            pl.BlockSpec((1, ADA_TN), lambda j: (0, j)),
        ],
        out_specs=pl.BlockSpec((8, ADA_TN), lambda j: (0, j)),
        out_shape=jax.ShapeDtypeStruct((8, n), F32),
        compiler_params=pltpu.CompilerParams(
            dimension_semantics=("arbitrary",), vmem_limit_bytes=VMEM_LIMIT),
        name="ada",
    )(c8, w_ada, b_ada)


def _proj_kernel(x_ref, xh_ref, mod_ref, npre_ref,
                 wqk_ref, wv_ref, wo_ref, wga_ref, wgb_ref, ws_ref,
                 convw_ref, convb_ref, qks_ref, gbias_ref, qn_ref, kvn_ref,
                 wuq_ref, wukv_ref, cos_ref, sin_ref,
                 qk_out, v_out, og_out, ga_out, gb_out, gates_out,
                 qnope_out, qpe_out, knope_out, vb_out, kpe_out,
                 hext_ref, z_ref, *, tm, tiles_per_seq, q_scale):
    i = pl.program_id(0)
    first = (i % tiles_per_seq) == 0
    shift = mod_ref[0:1, :]
    scale1 = 1.0 + mod_ref[1:2, :]
    w = npre_ref[...]

    def prenorm(xv):
        return _rms(xv, w) * scale1 + shift

    hext_ref[HALO:, :] = prenorm(x_ref[...]).astype(BF16)
    hh = prenorm(xh_ref[...])
    hext_ref[0:HALO, :] = jnp.where(first, 0.0, hh).astype(BF16)

    z_ref[...] = _dot(hext_ref[...], wqk_ref[...])
    acc = convb_ref[...]
    for j in range(CONV_W):
        off = HALO - (CONV_W - 1) + j
        acc = acc + convw_ref[j:j + 1, :] * z_ref[off:off + tm, :]
    qk_out[...] = (acc * _sigmoid(acc) * qks_ref[...]).astype(BF16)

    h = hext_ref[HALO:, :]
    v_out[...] = _dot(h, wv_ref[...]).astype(BF16)
    og_out[...] = _sigmoid(_dot(h, wo_ref[...])).astype(BF16)
    ga_out[...] = _sigmoid(_dot(h, wga_ref[...])).astype(BF16)
    gb_out[...] = _sigmoid(_dot(h, wgb_ref[...])).astype(BF16)

    s = _dot(h, ws_ref[...])
    c_q = s[:, 0:A_QRANK]
    c_kv = s[:, A_QRANK:A_QRANK + A_KVRANK]
    kp = s[:, 640:768]
    kpr = s[:, 768:896]
    g = s[:, 896:1024] + gbias_ref[...]

    lane = lax.broadcasted_iota(jnp.int32, g.shape, 1)
    logsig = jnp.minimum(g, 0.0) - jnp.log1p(jnp.exp(-jnp.abs(g)))
    gates_out[...] = jnp.where(lane < M_HEADS, g, logsig)

    cos = cos_ref[...]
    sin = sin_ref[...]
    kpe_out[...] = (kp * cos + kpr * sin).astype(BF16)

    q = _dot(_rms(c_q, qn_ref[...]).astype(BF16), wuq_ref[...])
    qnope_out[...] = (q[:, 0:1024] * q_scale).astype(BF16)
    cos4 = jnp.concatenate([cos] * 4, axis=1)
    sin4 = jnp.concatenate([sin] * 4, axis=1)
    qpe = q[:, 1024:1536] * cos4 + q[:, 1536:2048] * sin4
    qpe_out[...] = (qpe * q_scale).astype(BF16)

    kv = _dot(_rms(c_kv, kvn_ref[...]).astype(BF16), wukv_ref[...])
    knope_out[...] = kv[:, 0:1024].astype(BF16)
    vb_out[...] = kv[:, 1024:2048].astype(BF16)


def _proj(x2, mod3, npre, wts, convw, convb, qks, gbias, qn, kvn, wuq, wukv,
          cos, sin, seq):
    t = x2.shape[0]
    tm = PROJ_TM
    tiles_per_seq = seq // tm
    hb = tm // HALO
    wqk, wv, wo, wga, wgb, ws = wts
    row = lambda i: (i, 0)
    out_w = [1024, 1024, 1024, 1024, 1024, 128, 1024, 512, 1024, 1024, 128]
    out_dt = [BF16] * 5 + [F32] + [BF16] * 5
    kern = functools.partial(
        _proj_kernel, tm=tm, tiles_per_seq=tiles_per_seq,
        q_scale=float((A_NOPE + A_ROPE) ** -0.5 * LOG2E))
    return pl.pallas_call(
        kern,
        grid=(t // tm,),
        in_specs=[
            pl.BlockSpec((tm, D_MODEL), row),
            pl.BlockSpec((HALO, D_MODEL), lambda i: (jnp.maximum(i * hb - 1, 0), 0)),
            pl.BlockSpec((None, N_MOD, D_MODEL), lambda i: (i // tiles_per_seq, 0, 0)),
            _const_spec((1, D_MODEL)),
            _const_spec(wqk.shape), _const_spec(wv.shape), _const_spec(wo.shape),
            _const_spec(wga.shape), _const_spec(wgb.shape), _const_spec(ws.shape),
            _const_spec(convw.shape), _const_spec(convb.shape), _const_spec(qks.shape),
            _const_spec(gbias.shape), _const_spec(qn.shape), _const_spec(kvn.shape),
            _const_spec(wuq.shape), _const_spec(wukv.shape),
            pl.BlockSpec((tm, 128), lambda i: (i % tiles_per_seq, 0)),
            pl.BlockSpec((tm, 128), lambda i: (i % tiles_per_seq, 0)),
        ],
        out_specs=[pl.BlockSpec((tm, n), row) for n in out_w],
        out_shape=[jax.ShapeDtypeStruct((t, n), d) for n, d in zip(out_w, out_dt)],
        scratch_shapes=[
            pltpu.VMEM((tm + HALO, D_MODEL), BF16),
            pltpu.VMEM((tm + HALO, 2 * MA_QK), F32),
        ],
        compiler_params=pltpu.CompilerParams(
            dimension_semantics=("arbitrary",), vmem_limit_bytes=VMEM_LIMIT),
        name="proj",
    )(x2, x2, mod3, npre, wqk, wv, wo, wga, wgb, ws, convw, convb, qks, gbias,
      qn, kvn, wuq, wukv, cos, sin)


def _split3(x):
    hi = x.astype(BF16)
    r = x - hi.astype(F32)
    mid = r.astype(BF16)
    lo = (r - mid.astype(F32)).astype(BF16)
    return hi, mid, lo


def _mlstm_kernel(qk_ref, v_ref, og_ref, gc_ref, gr_ref, hnw_ref, y_ref,
                  ct_ref, m_ref, *, L):
    @pl.when(pl.program_id(1) == 0)
    def _():
        ct_ref[...] = jnp.zeros_like(ct_ref)
        m_ref[...] = jnp.zeros_like(m_ref)

    gc = gc_ref[...] * LOG2E
    gr = gr_ref[...] * LOG2E
    row = lax.broadcasted_iota(jnp.int32, (L, L), 0)
    col = lax.broadcasted_iota(jnp.int32, (L, L), 1)
    causal = col <= row
    tri = causal.astype(BF16)
    bcol_all = sum(_dot(tri, p) for p in _split3(gc))
    brow_all = sum(lax.dot_general(p, tri, _NT, preferred_element_type=F32)
                   for p in _split3(gr))
    ones = jnp.ones((L, 128), BF16)
    nlt = L // 128

    for h in range(M_HEADS):
        q = qk_ref[:, h * M_DQK:(h + 1) * M_DQK]
        k = qk_ref[:, MA_QK + h * M_DQK:MA_QK + (h + 1) * M_DQK]
        v_aug = jnp.concatenate([v_ref[:, h * M_DV:(h + 1) * M_DV], ones], axis=1)
        c_row = gr[h:h + 1, :] - brow_all[M_HEADS + h:M_HEADS + h + 1, :]
        i_rep = jnp.broadcast_to(gc[:, h:h + 1], (L, 128))
        b_rep = jnp.broadcast_to(bcol_all[:, M_HEADS + h:M_HEADS + h + 1], (L, 128))
        m_prev = m_ref[h:h + 1, :]

        d_tiles = [jnp.where(causal[:, t * 128:(t + 1) * 128],
                             b_rep + c_row[:, t * 128:(t + 1) * 128], NEG) for t in range(nlt)]
        a = b_rep + m_prev
        m_intra = jnp.max(functools.reduce(jnp.maximum, d_tiles), axis=1, keepdims=True)
        m_out = jnp.maximum(a, m_intra)
        qk = lax.dot_general(q, k, _NT, preferred_element_type=F32)
        s_tiles = [(qk[:, t * 128:(t + 1) * 128] * jnp.exp2(d_tiles[t] - m_out)).astype(BF16)
                   for t in range(nlt)]
        w_inter = jnp.exp2(a - m_out).astype(BF16)
        ct = ct_ref[h]
        lhs = jnp.concatenate(s_tiles + [w_inter * q], axis=1)
        rhs = jnp.concatenate([v_aug, ct.astype(BF16)], axis=0)
        nd = _dot(lhs, rhs)
        num = nd[:, 0:M_DV]
        den = nd[:, M_DV:M_DV + 128]
        inv = 1.0 / jnp.maximum(jnp.abs(den), jnp.exp2(-m_out))
        ms = jnp.mean(num * num, axis=1, keepdims=True)
        f = inv * lax.rsqrt(inv * inv * ms + EPS)
        hn = num * jnp.concatenate([f, f], axis=1) * hnw_ref[:, h * M_DV:(h + 1) * M_DV]
        y_ref[:, h * M_DV:(h + 1) * M_DV] = og_ref[:, h * M_DV:(h + 1) * M_DV] * hn.astype(BF16)

        b_last = b_rep[L - 1:L, :]
        g_prev = b_last + m_prev
        g = b_last - b_rep + i_rep
        m_new = jnp.maximum(g_prev, jnp.max(g, axis=0, keepdims=True))
        wk = jnp.exp2(g - m_new).astype(BF16)
        decay = jnp.exp2(g_prev - m_new)
        wv = jnp.concatenate([wk] * 3, axis=1) * v_aug
        ct_ref[h] = (jnp.concatenate([decay] * 3, axis=1) * ct
                     + lax.dot_general(k, wv, _TN, preferred_element_type=F32))
        m_ref[h:h + 1, :] = m_new


def _mlstm(qk, v, og, gc, gr, hnw):
    b, seq, _ = qk.shape
    L = MLSTM_L
    blk = lambda w: pl.BlockSpec((None, L, w), lambda bi, ci: (bi, ci, 0))
    return pl.pallas_call(
        functools.partial(_mlstm_kernel, L=L),
        grid=(b, seq // L),
        in_specs=[
            blk(2 * MA_QK), blk(MA_V), blk(MA_V), blk(128),
            pl.BlockSpec((None, 8, L), lambda bi, ci: (bi, 0, ci)),
            _const_spec(hnw.shape),
        ],
        out_specs=blk(MA_V),
        out_shape=jax.ShapeDtypeStruct((b, seq, MA_V), BF16),
        scratch_shapes=[
            pltpu.VMEM((M_HEADS, M_DQK, M_DV + 128), F32),
            pltpu.VMEM((8, 128), F32),
        ],
        compiler_params=pltpu.CompilerParams(
            dimension_semantics=("parallel", "arbitrary"), vmem_limit_bytes=VMEM_LIMIT),
        name="mlstm",
    )(qk, v, og, gc, gr, hnw)


def _attn_kernel(qn_ref, qpe_ref, kn_ref, kpe_ref, v_ref, o_ref,
                 kcat_ref, vaug_ref, qcat_ref, m_ref, acc_ref, sa_ref, sb_ref, *, tq, tk, seq):
    assert tq == 2 * tk
    h = pl.program_id(1)
    n_tiles = seq // tq
    nt = tk // 128

    kcat_ref[:, 0:A_NOPE] = kn_ref[...]
    kcat_ref[:, A_NOPE:2 * A_NOPE] = kpe_ref[...]
    vaug_ref[:, 0:A_DV] = v_ref[...]
    vaug_ref[:, A_DV:2 * A_DV] = jnp.ones((seq, A_DV), BF16)
    lane = lax.broadcasted_iota(jnp.int32, (seq, 128), 1)
    own = ((lane // A_ROPE) == (h % 2)).astype(F32)
    qcat_ref[:, 0:A_NOPE] = qn_ref[...]
    qcat_ref[:, A_NOPE:2 * A_NOPE] = (qpe_ref[...].astype(F32) * own).astype(BF16)

    def logits(qt, kb, r0=0, nrows=tq):
        q0 = pl.multiple_of(qt * tq + r0, tk)
        k0 = pl.multiple_of(kb * tk, tk)
        return lax.dot_general(qcat_ref[pl.ds(q0, nrows), :], kcat_ref[pl.ds(k0, tk), :],
                               _NT, preferred_element_type=F32)

    def softmax_pv(s, kb, r0, nrows, masked):
        k0 = pl.multiple_of(kb * tk, tk)
        if masked:
            r = lax.broadcasted_iota(jnp.int32, (nrows, tk), 0)
            c = lax.broadcasted_iota(jnp.int32, (nrows, tk), 1)
            s = jnp.where(c <= r, s, NEG)
        tiles = [s[:, t * 128:(t + 1) * 128] for t in range(nt)]
        mx = functools.reduce(jnp.maximum, tiles)
        m_prev = m_ref[r0:r0 + nrows, :]
        m_new = jnp.maximum(m_prev, jnp.max(mx, axis=1, keepdims=True))
        alpha = jnp.exp2(m_prev - m_new)
        p = jnp.concatenate([jnp.exp2(t - m_new).astype(BF16) for t in tiles], axis=1)
        pv = _dot(p, vaug_ref[pl.ds(k0, tk), :])
        acc_ref[r0:r0 + nrows, :] = (jnp.concatenate([alpha, alpha], axis=1)
                                     * acc_ref[r0:r0 + nrows, :] + pv)
        m_ref[r0:r0 + nrows, :] = m_new

    sa_ref[...] = logits(0, 0)

    def tile_body(qt, carry):
        m_ref[...] = jnp.full_like(m_ref, NEG)
        acc_ref[...] = jnp.zeros_like(acc_ref)

        def pair_body(i, c):
            j = 2 * i
            sb_ref[...] = logits(qt, j + 1)
            softmax_pv(sa_ref[...], j, 0, tq, False)
            sa_ref[...] = logits(qt, j + 2)
            softmax_pv(sb_ref[...], j + 1, 0, tq, False)
            return c

        lax.fori_loop(0, qt, pair_body, 0)

        d0 = 2 * qt
        sb_ref[0:tk, :] = logits(qt, d0 + 1, tk, tk)
        softmax_pv(sa_ref[...], d0, 0, tq, True)
        sa_ref[...] = logits(jnp.minimum(qt + 1, n_tiles - 1), 0)
        softmax_pv(sb_ref[0:tk, :], d0 + 1, tk, tk, True)
        o0 = pl.multiple_of(qt * tq, tq)
        o_ref[pl.ds(o0, tq), :] = (acc_ref[:, 0:A_DV] / acc_ref[:, A_DV:2 * A_DV]).astype(BF16)
        return carry

    lax.fori_loop(0, n_tiles, tile_body, 0)


def _attn(qn, qpe, kn, kpe, v):
    b, seq, _ = qn.shape
    tq, tk = ATTN_TQ, ATTN_TK
    return pl.pallas_call(
        functools.partial(_attn_kernel, tq=tq, tk=tk, seq=seq),
        grid=(b, A_HEADS),
        in_specs=[
            pl.BlockSpec((None, seq, A_NOPE), lambda bi, h: (bi, 0, h)),
            pl.BlockSpec((None, seq, 128), lambda bi, h: (bi, 0, h // 2)),
            pl.BlockSpec((None, seq, A_NOPE), lambda bi, h: (bi, 0, h)),
            pl.BlockSpec((None, seq, 128), lambda bi, h: (bi, 0, 0)),
            pl.BlockSpec((None, seq, A_DV), lambda bi, h: (bi, 0, h)),
        ],
        out_specs=pl.BlockSpec((None, seq, A_DV), lambda bi, h: (bi, 0, h)),
        out_shape=jax.ShapeDtypeStruct((b, seq, A_HEADS * A_DV), BF16),
        scratch_shapes=[
            pltpu.VMEM((seq, 2 * A_NOPE), BF16),
            pltpu.VMEM((seq, 2 * A_DV), BF16),
            pltpu.VMEM((seq, 2 * A_NOPE), BF16),
            pltpu.VMEM((tq, 128), F32),
            pltpu.VMEM((tq, 2 * A_DV), F32),
            pltpu.VMEM((tq, tk), F32),
            pltpu.VMEM((tq, tk), F32),
        ],
        compiler_params=pltpu.CompilerParams(
            dimension_semantics=("parallel", "arbitrary"),
            vmem_limit_bytes=VMEM_LIMIT),
        name="attn",
    )(qn, qpe, kn, kpe, v)


def _mlp_kernel(x_ref, ya_ref, yb_ref, ga_ref, gb_ref, mod_ref,
                npost_ref, npre2_ref, npost2_ref, wout_ref, w1_ref, w2_ref,
                o_ref, acc_ref):
    gate_m = mod_ref[2:3, :]
    shift_f = mod_ref[3:4, :]
    scale_f = 1.0 + mod_ref[4:5, :]
    gate_f = mod_ref[5:6, :]

    y = (ga_ref[...].astype(F32) * ya_ref[...].astype(F32)
         + gb_ref[...].astype(F32) * yb_ref[...].astype(F32))
    yo = _dot(y.astype(BF16), wout_ref[...])
    x1 = x_ref[...] + gate_m * _rms(yo, npost_ref[...])

    h2 = (_rms(x1, npre2_ref[...]) * scale_f + shift_f).astype(BF16)
    for c in range(D_FF // FF_CHUNK):
        u = jnp.maximum(_dot(h2, w1_ref[:, c * FF_CHUNK:(c + 1) * FF_CHUNK]), 0.0)
        part = _dot((u * u).astype(BF16), w2_ref[c * FF_CHUNK:(c + 1) * FF_CHUNK, :])
        if c == 0:
            acc_ref[...] = part
        else:
            acc_ref[...] += part
    o_ref[...] = x1 + gate_f * _rms(acc_ref[...], npost2_ref[...])


def _mlp(x2, ya, yb, ga, gb, mod3, npost, npre2, npost2, wout, w1, w2, seq):
    t = x2.shape[0]
    tm = MLP_TM
    tiles_per_seq = seq // tm
    row = pl.BlockSpec((tm, D_MODEL), lambda i: (i, 0))
    return pl.pallas_call(
        _mlp_kernel,
        grid=(t // tm,),
        in_specs=[
            row, row, row, row, row,
            pl.BlockSpec((None, N_MOD, D_MODEL), lambda i: (i // tiles_per_seq, 0, 0)),
            _const_spec((1, D_MODEL)), _const_spec((1, D_MODEL)), _const_spec((1, D_MODEL)),
            _const_spec(wout.shape), _const_spec(w1.shape), _const_spec(w2.shape),
        ],
        out_specs=row,
        out_shape=jax.ShapeDtypeStruct((t, D_MODEL), F32),
        scratch_shapes=[pltpu.VMEM((tm, D_MODEL), F32)],
        compiler_params=pltpu.CompilerParams(
            dimension_semantics=("arbitrary",), vmem_limit_bytes=VMEM_LIMIT),
        name="mlp",
    )(x2, ya, yb, ga, gb, mod3, npost, npre2, npost2, wout, w1, w2)


def _rot_half_cols(w):
    half = w.shape[-1] // 2
    return jnp.concatenate([-w[..., half:], w[..., :half]], axis=-1)


def _prep_in_weights(w_in):
    o = 0
    parts = []
    for n in (MA_QK, MA_QK, MA_V, MA_V, M_HEADS, M_HEADS, A_QRANK, A_KVRANK, A_ROPE,
              D_MODEL, D_MODEL):
        parts.append(w_in[:, o:o + n])
        o += n
    w_q, w_k, w_v, w_o, w_i, w_f, w_cq, w_ckv, w_kpe, w_ga, w_gb = parts
    kpe_rot = _rot_half_cols(w_kpe)
    pad = jnp.zeros((w_in.shape[0], 128 - 2 * M_HEADS), w_in.dtype)
    w_s = jnp.concatenate([w_cq, w_ckv, w_kpe, w_kpe, kpe_rot, kpe_rot, w_i, w_f, pad], axis=1)
    w_qk = jnp.concatenate([w_q, w_k], axis=1)
    return tuple(t.astype(BF16) for t in (w_qk, w_v, w_o, w_ga, w_gb, w_s))


def _prep_mla_weights(w_uq, w_ukv):
    r = w_uq.reshape(A_QRANK, A_HEADS, A_NOPE + A_ROPE)
    nope = r[:, :, :A_NOPE].reshape(A_QRANK, A_HEADS * A_NOPE)
    pe = r[:, :, A_NOPE:]
    pe_rot = _rot_half_cols(pe).reshape(A_QRANK, A_HEADS * A_ROPE)
    pe = pe.reshape(A_QRANK, A_HEADS * A_ROPE)
    wuq = jnp.concatenate([nope, pe, pe_rot], axis=1).astype(BF16)
    r = w_ukv.reshape(A_KVRANK, A_HEADS, A_NOPE + A_DV)
    wukv = jnp.concatenate([r[:, :, :A_NOPE].reshape(A_KVRANK, -1),
                            r[:, :, A_NOPE:].reshape(A_KVRANK, -1)], axis=1).astype(BF16)
    return wuq, wukv


def _rope_tables(seq):
    half = A_ROPE // 2
    inv_freq = ROPE_THETA ** (-jnp.arange(half, dtype=F32) / half)
    ang = jnp.arange(seq, dtype=jnp.int32).astype(F32)[:, None] * inv_freq[None, :]
    reps = 128 // half
    return jnp.tile(jnp.cos(ang), (1, reps)), jnp.tile(jnp.sin(ang), (1, reps))


def kernel(x, c, w_ada, b_ada, norm_pre_mix, norm_post_mix, norm_pre_mlp, norm_post_mlp,
           w_in, mlstm_conv_w, mlstm_conv_b, mlstm_gate_b, mlstm_head_norm,
           mla_q_norm, mla_kv_norm, w_uq, w_ukv, w_out, w_ff1, w_ff2):
    bsz, seq, d = x.shape
    depth = w_ada.shape[0]
    cos, sin = _rope_tables(seq)
    c8 = jnp.pad(c, ((0, 8 - bsz), (0, 0)))
    qks = jnp.concatenate([jnp.ones((1, MA_QK), F32),
                           jnp.full((1, MA_QK), M_DQK ** -0.5, F32)], axis=1)
    row = lambda a: a.reshape(1, -1)
    x2 = x.reshape(bsz * seq, d)
    for l in range(depth):
        mod = _ada(c8, w_ada[l], row(b_ada[l]))[:bsz]
        mod3 = mod.reshape(bsz, N_MOD, d)
        wts = _prep_in_weights(w_in[l])
        wuq, wukv = _prep_mla_weights(w_uq[l], w_ukv[l])
        gbias = jnp.pad(row(mlstm_gate_b[l]), ((0, 0), (0, 128 - 2 * M_HEADS)))
        (qk, v_a, og, ga, gb, gates, qnope, qpe, knope, v_b, kpe) = _proj(
            x2, mod3, row(norm_pre_mix[l]), wts, mlstm_conv_w[l], row(mlstm_conv_b[l]),
            qks, gbias, row(mla_q_norm[l]), row(mla_kv_norm[l]), wuq, wukv, cos, sin, seq)
        b3 = lambda a: a.reshape(bsz, seq, a.shape[-1])
        gc = b3(gates)
        gr = jnp.transpose(gc[:, :, :8], (0, 2, 1))
        y_a = _mlstm(b3(qk), b3(v_a), b3(og), gc, gr, row(mlstm_head_norm[l]))
        y_b = _attn(b3(qnope), b3(qpe), b3(knope), b3(kpe), b3(v_b))
        x2 = _mlp(x2, y_a.reshape(bsz * seq, d), y_b.reshape(bsz * seq, d), ga, gb, mod3,
                  row(norm_post_mix[l]), row(norm_pre_mlp[l]), row(norm_post_mlp[l]),
                  w_out[l].astype(BF16), w_ff1[l].astype(BF16), w_ff2[l].astype(BF16), seq)
    return x2.reshape(bsz, seq, d)
```

```python
import functools

import jax
import jax.numpy as jnp
from jax import lax
from jax.experimental import pallas as pl
from jax.experimental.pallas import tpu as pltpu

F32 = jnp.float32
BF16 = jnp.bfloat16

D_MODEL = 1024
M_HEADS = 4
M_DQK = 128
M_DV = 256
CONV_W = 4
A_HEADS = 8
A_NOPE = 128
A_ROPE = 64
A_DV = 128
A_QRANK = 384
A_KVRANK = 256
ROPE_THETA = 10000.0
D_FF = 4096
EPS = 1e-6
N_MOD = 6
MA_QK = M_HEADS * M_DQK
MA_V = M_HEADS * M_DV

PROJ_TM = 512
HALO = 16
MLSTM_L = 256
ATTN_TQ = 1024
ATTN_TK = 512
ONES_ROWS = 16
MLP_TM = 512
FF_CHUNK = 1024
ADA_TN = 1536
VMEM_LIMIT = 56 * 1024 * 1024

NEG = -1e30
LOG2E = 1.4426950408889634

_NT = (((1,), (1,)), ((), ()))
_TN = (((0,), (0,)), ((), ()))


def _dot(a, b):
    return jnp.dot(a, b, preferred_element_type=F32)


def _sigmoid(x):
    return 1.0 / (1.0 + jnp.exp(-x))


def _rms(x, w):
    return x * lax.rsqrt(jnp.mean(x * x, axis=-1, keepdims=True) + EPS) * w


def _const_spec(shape):
    nd = len(shape)
    return pl.BlockSpec(shape, lambda *_: (0,) * nd, pipeline_mode=pl.Buffered(1))


def _ada_kernel(c_ref, w_ref, b_ref, o_ref):
    c = c_ref[...]
    a = (c * _sigmoid(c)).astype(BF16)
    o_ref[...] = _dot(a, w_ref[...].astype(BF16)) + b_ref[...]


def _ada(c8, w_ada, b_ada):
    n = w_ada.shape[1]
    return pl.pallas_call(
        _ada_kernel,
        grid=(n // ADA_TN,),
        in_specs=[
            pl.BlockSpec((8, D_MODEL), lambda j: (0, 0)),
            pl.BlockSpec((D_MODEL, ADA_TN), lambda j: (0, j)),
            pl.BlockSpec((1, ADA_TN), lambda j: (0, j)),
        ],
        out_specs=pl.BlockSpec((8, ADA_TN), lambda j: (0, j)),
        out_shape=jax.ShapeDtypeStruct((8, n), F32),
        compiler_params=pltpu.CompilerParams(
            dimension_semantics=("arbitrary",), vmem_limit_bytes=VMEM_LIMIT),
        name="ada",
    )(c8, w_ada, b_ada)


def _proj_kernel(x_ref, xh_ref, mod_ref, npre_ref,
                 wqk_ref, wv_ref, wo_ref, wga_ref, wgb_ref, ws_ref,
                 convw_ref, convb_ref, qks_ref, gbias_ref, qn_ref, kvn_ref,
                 wuq_ref, wuk_ref, wuvt_ref, cos_ref, sin_ref,
                 qk_out, v_out, og_out, ga_out, gb_out, gates_out,
                 qnope_out, qpe_out, knope_out, kpe_out, vbt_out,
                 hext_ref, z_ref, *, tm, tiles_per_seq, q_scale):
    i = pl.program_id(0)
    first = (i % tiles_per_seq) == 0
    shift = mod_ref[0:1, :]
    scale1 = 1.0 + mod_ref[1:2, :]
    w = npre_ref[...]

    def prenorm(xv):
        return _rms(xv, w) * scale1 + shift

    hext_ref[HALO:, :] = prenorm(x_ref[...]).astype(BF16)
    hh = prenorm(xh_ref[...])
    hext_ref[0:HALO, :] = jnp.where(first, 0.0, hh).astype(BF16)

    z_ref[...] = _dot(hext_ref[...], wqk_ref[...])
    acc = convb_ref[...]
    for j in range(CONV_W):
        off = HALO - (CONV_W - 1) + j
        acc = acc + convw_ref[j:j + 1, :] * z_ref[off:off + tm, :]
    qk_out[...] = (acc * _sigmoid(acc) * qks_ref[...]).astype(BF16)

    h = hext_ref[HALO:, :]
    v_out[...] = _dot(h, wv_ref[...]).astype(BF16)
    og_out[...] = _sigmoid(_dot(h, wo_ref[...])).astype(BF16)
    ga_out[...] = _sigmoid(_dot(h, wga_ref[...])).astype(BF16)
    gb_out[...] = _sigmoid(_dot(h, wgb_ref[...])).astype(BF16)

    s = _dot(h, ws_ref[...])
    c_q = s[:, 0:A_QRANK]
    c_kv = s[:, A_QRANK:A_QRANK + A_KVRANK]
    kp = s[:, 640:768]
    kpr = s[:, 768:896]
    g = s[:, 896:1024] + gbias_ref[...]

    lane = lax.broadcasted_iota(jnp.int32, g.shape, 1)
    logsig = jnp.minimum(g, 0.0) - jnp.log1p(jnp.exp(-jnp.abs(g)))
    gates_out[...] = jnp.where(lane < M_HEADS, g, logsig)

    cos = cos_ref[...]
    sin = sin_ref[...]
    kpe_out[...] = (kp * cos + kpr * sin).astype(BF16)

    q = _dot(_rms(c_q, qn_ref[...]).astype(BF16), wuq_ref[...])
    qnope_out[...] = (q[:, 0:1024] * q_scale).astype(BF16)
    cos4 = jnp.concatenate([cos] * 4, axis=1)
    sin4 = jnp.concatenate([sin] * 4, axis=1)
    qpe = q[:, 1024:1536] * cos4 + q[:, 1536:2048] * sin4
    qpe_out[...] = (qpe * q_scale).astype(BF16)

    ckvn = _rms(c_kv, kvn_ref[...]).astype(BF16)
    knope_out[...] = _dot(ckvn, wuk_ref[...]).astype(BF16)
    vbt_out[...] = lax.dot_general(wuvt_ref[...], ckvn, _NT,
                                   preferred_element_type=F32).astype(BF16)


def _proj(x2, mod3, npre, wts, convw, convb, qks, gbias, qn, kvn, wuq, wuk, wuvt,
          cos, sin, seq):
    t = x2.shape[0]
    tm = PROJ_TM
    tiles_per_seq = seq // tm
    hb = tm // HALO
    wqk, wv, wo, wga, wgb, ws = wts
    row = lambda i: (i, 0)
    out_w = [1024, 1024, 1024, 1024, 1024, 128, 1024, 512, 1024, 128]
    out_dt = [BF16] * 5 + [F32] + [BF16] * 4
    n_vt = A_HEADS * A_DV
    kern = functools.partial(
        _proj_kernel, tm=tm, tiles_per_seq=tiles_per_seq,
        q_scale=float((A_NOPE + A_ROPE) ** -0.5 * LOG2E))
    return pl.pallas_call(
        kern,
        grid=(t // tm,),
        in_specs=[
            pl.BlockSpec((tm, D_MODEL), row),
            pl.BlockSpec((HALO, D_MODEL), lambda i: (jnp.maximum(i * hb - 1, 0), 0)),
            pl.BlockSpec((None, N_MOD, D_MODEL), lambda i: (i // tiles_per_seq, 0, 0)),
            _const_spec((1, D_MODEL)),
            _const_spec(wqk.shape), _const_spec(wv.shape), _const_spec(wo.shape),
            _const_spec(wga.shape), _const_spec(wgb.shape), _const_spec(ws.shape),
            _const_spec(convw.shape), _const_spec(convb.shape), _const_spec(qks.shape),
            _const_spec(gbias.shape), _const_spec(qn.shape), _const_spec(kvn.shape),
            _const_spec(wuq.shape), _const_spec(wuk.shape), _const_spec(wuvt.shape),
            pl.BlockSpec((tm, 128), lambda i: (i % tiles_per_seq, 0)),
            pl.BlockSpec((tm, 128), lambda i: (i % tiles_per_seq, 0)),
        ],
        out_specs=([pl.BlockSpec((tm, n), row) for n in out_w]
                   + [pl.BlockSpec((n_vt, tm), lambda i: (0, i))]),
        out_shape=([jax.ShapeDtypeStruct((t, n), d) for n, d in zip(out_w, out_dt)]
                   + [jax.ShapeDtypeStruct((n_vt, t), BF16)]),
        scratch_shapes=[
            pltpu.VMEM((tm + HALO, D_MODEL), BF16),
            pltpu.VMEM((tm + HALO, 2 * MA_QK), F32),
        ],
        compiler_params=pltpu.CompilerParams(
            dimension_semantics=("arbitrary",), vmem_limit_bytes=VMEM_LIMIT),
        name="proj",
    )(x2, x2, mod3, npre, wqk, wv, wo, wga, wgb, ws, convw, convb, qks, gbias,
      qn, kvn, wuq, wuk, wuvt, cos, sin)


def _split3(x):
    hi = x.astype(BF16)
    r = x - hi.astype(F32)
    mid = r.astype(BF16)
    lo = (r - mid.astype(F32)).astype(BF16)
    return hi, mid, lo


def _mlstm_kernel(qk_ref, v_ref, og_ref, gc_ref, gr_ref, hnw_ref, y_ref,
                  ct_ref, m_ref, *, L):
    @pl.when(pl.program_id(1) == 0)
    def _():
        ct_ref[...] = jnp.zeros_like(ct_ref)
        m_ref[...] = jnp.zeros_like(m_ref)

    gc = gc_ref[...] * LOG2E
    gr = gr_ref[...] * LOG2E
    row = lax.broadcasted_iota(jnp.int32, (L, L), 0)
    col = lax.broadcasted_iota(jnp.int32, (L, L), 1)
    causal = col <= row
    tri = causal.astype(BF16)
    bcol_all = sum(_dot(tri, p) for p in _split3(gc))
    brow_all = sum(lax.dot_general(p, tri, _NT, preferred_element_type=F32)
                   for p in _split3(gr))
    ones = jnp.ones((L, 128), BF16)
    nlt = L // 128

    for h in range(M_HEADS):
        q = qk_ref[:, h * M_DQK:(h + 1) * M_DQK]
        k = qk_ref[:, MA_QK + h * M_DQK:MA_QK + (h + 1) * M_DQK]
        v_aug = jnp.concatenate([v_ref[:, h * M_DV:(h + 1) * M_DV], ones], axis=1)
        c_row = gr[h:h + 1, :] - brow_all[M_HEADS + h:M_HEADS + h + 1, :]
        i_rep = jnp.broadcast_to(gc[:, h:h + 1], (L, 128))
        b_rep = jnp.broadcast_to(bcol_all[:, M_HEADS + h:M_HEADS + h + 1], (L, 128))
        m_prev = m_ref[h:h + 1, :]

        d_tiles = [jnp.where(causal[:, t * 128:(t + 1) * 128],
                             b_rep + c_row[:, t * 128:(t + 1) * 128], NEG) for t in range(nlt)]
        a = b_rep + m_prev
        m_intra = jnp.max(functools.reduce(jnp.maximum, d_tiles), axis=1, keepdims=True)
        m_out = jnp.maximum(a, m_intra)
        qk = lax.dot_general(q, k, _NT, preferred_element_type=F32)
        s_tiles = [(qk[:, t * 128:(t + 1) * 128] * jnp.exp2(d_tiles[t] - m_out)).astype(BF16)
                   for t in range(nlt)]
        w_inter = jnp.exp2(a - m_out).astype(BF16)
        ct = ct_ref[h]
        lhs = jnp.concatenate(s_tiles + [w_inter * q], axis=1)
        rhs = jnp.concatenate([v_aug, ct.astype(BF16)], axis=0)
        nd = _dot(lhs, rhs)
        num = nd[:, 0:M_DV]
        den = nd[:, M_DV:M_DV + 128]
        inv = 1.0 / jnp.maximum(jnp.abs(den), jnp.exp2(-m_out))
        ms = jnp.mean(num * num, axis=1, keepdims=True)
        f = inv * lax.rsqrt(inv * inv * ms + EPS)
        hn = num * jnp.concatenate([f, f], axis=1) * hnw_ref[:, h * M_DV:(h + 1) * M_DV]
        y_ref[:, h * M_DV:(h + 1) * M_DV] = og_ref[:, h * M_DV:(h + 1) * M_DV] * hn.astype(BF16)

        b_last = b_rep[L - 1:L, :]
        g_prev = b_last + m_prev
        g = b_last - b_rep + i_rep
        m_new = jnp.maximum(g_prev, jnp.max(g, axis=0, keepdims=True))
        wk = jnp.exp2(g - m_new).astype(BF16)
        decay = jnp.exp2(g_prev - m_new)
        wv = jnp.concatenate([wk] * 3, axis=1) * v_aug
        ct_ref[h] = (jnp.concatenate([decay] * 3, axis=1) * ct
                     + lax.dot_general(k, wv, _TN, preferred_element_type=F32))
        m_ref[h:h + 1, :] = m_new


def _mlstm(qk, v, og, gc, gr, hnw):
    b, seq, _ = qk.shape
    L = MLSTM_L
    blk = lambda w: pl.BlockSpec((None, L, w), lambda bi, ci: (bi, ci, 0))
    return pl.pallas_call(
        functools.partial(_mlstm_kernel, L=L),
        grid=(b, seq // L),
        in_specs=[
            blk(2 * MA_QK), blk(MA_V), blk(MA_V), blk(128),
            pl.BlockSpec((None, 8, L), lambda bi, ci: (bi, 0, ci)),
            _const_spec(hnw.shape),
        ],
        out_specs=blk(MA_V),
        out_shape=jax.ShapeDtypeStruct((b, seq, MA_V), BF16),
        scratch_shapes=[
            pltpu.VMEM((M_HEADS, M_DQK, M_DV + 128), F32),
            pltpu.VMEM((8, 128), F32),
        ],
        compiler_params=pltpu.CompilerParams(
            dimension_semantics=("parallel", "arbitrary"), vmem_limit_bytes=VMEM_LIMIT),
        name="mlstm",
    )(qk, v, og, gc, gr, hnw)


def _attn_kernel(qn_ref, qpe_ref, kn_ref, kpe_ref, vt_ref, o_ref,
                 kcat_ref, vtaug_ref, qcat_ref, m_ref, acc_ref, sa_ref, sb_ref, *, tq, tk, seq):
    assert tq == 2 * tk
    h = pl.program_id(1)
    n_tiles = seq // tq

    kcat_ref[:, 0:A_NOPE] = kn_ref[...]
    kcat_ref[:, A_NOPE:2 * A_NOPE] = kpe_ref[...]
    vtaug_ref[0:A_DV, :] = vt_ref[...]
    vtaug_ref[A_DV:, :] = jnp.ones((ONES_ROWS, seq), BF16)
    lane = lax.broadcasted_iota(jnp.int32, (seq, 128), 1)
    own = ((lane // A_ROPE) == (h % 2)).astype(F32)
    qcat_ref[:, 0:A_NOPE] = qn_ref[...]
    qcat_ref[:, A_NOPE:2 * A_NOPE] = (qpe_ref[...].astype(F32) * own).astype(BF16)

    def logits(qt, kb, c0=0, ncols=tq):
        q0 = pl.multiple_of(qt * tq + c0, tk)
        k0 = pl.multiple_of(kb * tk, tk)
        return lax.dot_general(kcat_ref[pl.ds(k0, tk), :], qcat_ref[pl.ds(q0, ncols), :],
                               _NT, preferred_element_type=F32)

    def softmax_pv(s, kb, c0, ncols, masked):
        k0 = pl.multiple_of(kb * tk, tk)
        if masked:
            key = lax.broadcasted_iota(jnp.int32, (tk, ncols), 0)
            qry = lax.broadcasted_iota(jnp.int32, (tk, ncols), 1)
            s = jnp.where(key <= qry, s, NEG)
        m_prev = m_ref[:, c0:c0 + ncols]
        m_new = jnp.maximum(m_prev, jnp.max(s, axis=0, keepdims=True))
        alpha = jnp.exp2(m_prev - m_new)
        p = jnp.exp2(s - m_new).astype(BF16)
        pv = _dot(vtaug_ref[:, pl.ds(k0, tk)], p)
        acc_ref[:, c0:c0 + ncols] = alpha * acc_ref[:, c0:c0 + ncols] + pv
        m_ref[:, c0:c0 + ncols] = m_new

    sa_ref[...] = logits(0, 0)

    def tile_body(qt, carry):
        m_ref[...] = jnp.full_like(m_ref, NEG)
        acc_ref[...] = jnp.zeros_like(acc_ref)

        def pair_body(i, c):
            j = 2 * i
            sb_ref[...] = logits(qt, j + 1)
            softmax_pv(sa_ref[...], j, 0, tq, False)
            sa_ref[...] = logits(qt, j + 2)
            softmax_pv(sb_ref[...], j + 1, 0, tq, False)
            return c

        lax.fori_loop(0, qt, pair_body, 0)

        d0 = 2 * qt
        sb_ref[:, 0:tk] = logits(qt, d0 + 1, tk, tk)
        softmax_pv(sa_ref[...], d0, 0, tq, True)
        sa_ref[...] = logits(jnp.minimum(qt + 1, n_tiles - 1), 0)
        softmax_pv(sb_ref[:, 0:tk], d0 + 1, tk, tk, True)
        out_t = acc_ref[0:A_DV, :] / acc_ref[A_DV:A_DV + 1, :]
        o0 = pl.multiple_of(qt * tq, tq)
        o_ref[pl.ds(o0, tq), :] = out_t.T.astype(BF16)
        return carry

    lax.fori_loop(0, n_tiles, tile_body, 0)


def _attn(qn, qpe, kn, kpe, vt):
    b, seq, _ = qn.shape
    tq, tk = ATTN_TQ, ATTN_TK
    return pl.pallas_call(
        functools.partial(_attn_kernel, tq=tq, tk=tk, seq=seq),
        grid=(b, A_HEADS),
        in_specs=[
            pl.BlockSpec((None, seq, A_NOPE), lambda bi, h: (bi, 0, h)),
            pl.BlockSpec((None, seq, 128), lambda bi, h: (bi, 0, h // 2)),
            pl.BlockSpec((None, seq, A_NOPE), lambda bi, h: (bi, 0, h)),
            pl.BlockSpec((None, seq, 128), lambda bi, h: (bi, 0, 0)),
            pl.BlockSpec((A_DV, seq), lambda bi, h: (h, bi)),
        ],
        out_specs=pl.BlockSpec((None, seq, A_DV), lambda bi, h: (bi, 0, h)),
        out_shape=jax.ShapeDtypeStruct((b, seq, A_HEADS * A_DV), BF16),
        scratch_shapes=[
            pltpu.VMEM((seq, 2 * A_NOPE), BF16),
            pltpu.VMEM((A_DV + ONES_ROWS, seq), BF16),
            pltpu.VMEM((seq, 2 * A_NOPE), BF16),
            pltpu.VMEM((1, tq), F32),
            pltpu.VMEM((A_DV + ONES_ROWS, tq), F32),
            pltpu.VMEM((tk, tq), F32),
            pltpu.VMEM((tk, tq), F32),
        ],
        compiler_params=pltpu.CompilerParams(
            dimension_semantics=("parallel", "arbitrary"),
            vmem_limit_bytes=VMEM_LIMIT),
        name="attn",
    )(qn, qpe, kn, kpe, vt)


def _mlp_kernel(x_ref, ya_ref, yb_ref, ga_ref, gb_ref, mod_ref,
                npost_ref, npre2_ref, npost2_ref, wout_ref, w1_ref, w2_ref,
                o_ref, acc_ref):
    gate_m = mod_ref[2:3, :]
    shift_f = mod_ref[3:4, :]
    scale_f = 1.0 + mod_ref[4:5, :]
    gate_f = mod_ref[5:6, :]

    y = (ga_ref[...].astype(F32) * ya_ref[...].astype(F32)
         + gb_ref[...].astype(F32) * yb_ref[...].astype(F32))
    yo = _dot(y.astype(BF16), wout_ref[...])
    x1 = x_ref[...] + gate_m * _rms(yo, npost_ref[...])

    h2 = (_rms(x1, npre2_ref[...]) * scale_f + shift_f).astype(BF16)
    for c in range(D_FF // FF_CHUNK):
        u = jnp.maximum(_dot(h2, w1_ref[:, c * FF_CHUNK:(c + 1) * FF_CHUNK]), 0.0)
        part = _dot((u * u).astype(BF16), w2_ref[c * FF_CHUNK:(c + 1) * FF_CHUNK, :])
        if c == 0:
            acc_ref[...] = part
        else:
            acc_ref[...] += part
    o_ref[...] = x1 + gate_f * _rms(acc_ref[...], npost2_ref[...])


def _mlp(x2, ya, yb, ga, gb, mod3, npost, npre2, npost2, wout, w1, w2, seq):
    t = x2.shape[0]
    tm = MLP_TM
    tiles_per_seq = seq // tm
    row = pl.BlockSpec((tm, D_MODEL), lambda i: (i, 0))
    return pl.pallas_call(
        _mlp_kernel,
        grid=(t // tm,),
        in_specs=[
            row, row, row, row, row,
            pl.BlockSpec((None, N_MOD, D_MODEL), lambda i: (i // tiles_per_seq, 0, 0)),
            _const_spec((1, D_MODEL)), _const_spec((1, D_MODEL)), _const_spec((1, D_MODEL)),
            _const_spec(wout.shape), _const_spec(w1.shape), _const_spec(w2.shape),
        ],
        out_specs=row,
        out_shape=jax.ShapeDtypeStruct((t, D_MODEL), F32),
        scratch_shapes=[pltpu.VMEM((tm, D_MODEL), F32)],
        compiler_params=pltpu.CompilerParams(
            dimension_semantics=("arbitrary",), vmem_limit_bytes=VMEM_LIMIT),
        name="mlp",
    )(x2, ya, yb, ga, gb, mod3, npost, npre2, npost2, wout, w1, w2)


def _rot_half_cols(w):
    half = w.shape[-1] // 2
    return jnp.concatenate([-w[..., half:], w[..., :half]], axis=-1)


def _prep_in_weights(w_in):
    o = 0
    parts = []
    for n in (MA_QK, MA_QK, MA_V, MA_V, M_HEADS, M_HEADS, A_QRANK, A_KVRANK, A_ROPE,
              D_MODEL, D_MODEL):
        parts.append(w_in[:, o:o + n])
        o += n
    w_q, w_k, w_v, w_o, w_i, w_f, w_cq, w_ckv, w_kpe, w_ga, w_gb = parts
    kpe_rot = _rot_half_cols(w_kpe)
    pad = jnp.zeros((w_in.shape[0], 128 - 2 * M_HEADS), w_in.dtype)
    w_s = jnp.concatenate([w_cq, w_ckv, w_kpe, w_kpe, kpe_rot, kpe_rot, w_i, w_f, pad], axis=1)
    w_qk = jnp.concatenate([w_q, w_k], axis=1)
    return tuple(t.astype(BF16) for t in (w_qk, w_v, w_o, w_ga, w_gb, w_s))


def _prep_mla_weights(w_uq, w_ukv):
    r = w_uq.reshape(A_QRANK, A_HEADS, A_NOPE + A_ROPE)
    nope = r[:, :, :A_NOPE].reshape(A_QRANK, A_HEADS * A_NOPE)
    pe = r[:, :, A_NOPE:]
    pe_rot = _rot_half_cols(pe).reshape(A_QRANK, A_HEADS * A_ROPE)
    pe = pe.reshape(A_QRANK, A_HEADS * A_ROPE)
    wuq = jnp.concatenate([nope, pe, pe_rot], axis=1).astype(BF16)
    r = w_ukv.reshape(A_KVRANK, A_HEADS, A_NOPE + A_DV)
    wuk = r[:, :, :A_NOPE].reshape(A_KVRANK, -1).astype(BF16)
    wuvt = r[:, :, A_NOPE:].reshape(A_KVRANK, -1).T.astype(BF16)
    return wuq, wuk, wuvt


def _rope_tables(seq):
    half = A_ROPE // 2
    inv_freq = ROPE_THETA ** (-jnp.arange(half, dtype=F32) / half)
    ang = jnp.arange(seq, dtype=jnp.int32).astype(F32)[:, None] * inv_freq[None, :]
    reps = 128 // half
    return jnp.tile(jnp.cos(ang), (1, reps)), jnp.tile(jnp.sin(ang), (1, reps))


def kernel(x, c, w_ada, b_ada, norm_pre_mix, norm_post_mix, norm_pre_mlp, norm_post_mlp,
           w_in, mlstm_conv_w, mlstm_conv_b, mlstm_gate_b, mlstm_head_norm,
           mla_q_norm, mla_kv_norm, w_uq, w_ukv, w_out, w_ff1, w_ff2):
    bsz, seq, d = x.shape
    depth = w_ada.shape[0]
    cos, sin = _rope_tables(seq)
    c8 = jnp.pad(c, ((0, 8 - bsz), (0, 0)))
    qks = jnp.concatenate([jnp.ones((1, MA_QK), F32),
                           jnp.full((1, MA_QK), M_DQK ** -0.5, F32)], axis=1)
    row = lambda a: a.reshape(1, -1)
    x2 = x.reshape(bsz * seq, d)
    for l in range(depth):
        mod = _ada(c8, w_ada[l], row(b_ada[l]))[:bsz]
        mod3 = mod.reshape(bsz, N_MOD, d)
        wts = _prep_in_weights(w_in[l])
        wuq, wuk, wuvt = _prep_mla_weights(w_uq[l], w_ukv[l])
        gbias = jnp.pad(row(mlstm_gate_b[l]), ((0, 0), (0, 128 - 2 * M_HEADS)))
        (qk, v_a, og, ga, gb, gates, qnope, qpe, knope, kpe, v_bt) = _proj(
            x2, mod3, row(norm_pre_mix[l]), wts, mlstm_conv_w[l], row(mlstm_conv_b[l]),
            qks, gbias, row(mla_q_norm[l]), row(mla_kv_norm[l]), wuq, wuk, wuvt, cos, sin, seq)
        b3 = lambda a: a.reshape(bsz, seq, a.shape[-1])
        gc = b3(gates)
        gr = jnp.transpose(gc[:, :, :8], (0, 2, 1))
        y_a = _mlstm(b3(qk), b3(v_a), b3(og), gc, gr, row(mlstm_head_norm[l]))
        y_b = _attn(b3(qnope), b3(qpe), b3(knope), b3(kpe), v_bt)
        x2 = _mlp(x2, y_a.reshape(bsz * seq, d), y_b.reshape(bsz * seq, d), ga, gb, mod3,
                  row(norm_post_mix[l]), row(norm_pre_mlp[l]), row(norm_post_mlp[l]),
                  w_out[l].astype(BF16), w_ff1[l].astype(BF16), w_ff2[l].astype(BF16), seq)
    return x2.reshape(bsz, seq, d)
```

```python
import functools

import jax
import jax.numpy as jnp
from jax import lax
from jax.experimental import pallas as pl
from jax.experimental.pallas import tpu as pltpu

F32 = jnp.float32
BF16 = jnp.bfloat16

D_MODEL = 1024
M_HEADS = 4
M_DQK = 128
M_DV = 256
CONV_W = 4
A_HEADS = 8
A_NOPE = 128
A_ROPE = 64
A_DV = 128
A_QRANK = 384
A_KVRANK = 256
ROPE_THETA = 10000.0
D_FF = 4096
EPS = 1e-6
N_MOD = 6
MA_QK = M_HEADS * M_DQK
MA_V = M_HEADS * M_DV

PROJ_TM = 512
HALO = 16
MLSTM_L = 256
MLSTM_G = 4
ATTN_TQ = 1024
ATTN_TK = 512
MLP_TM = 512
FF_CHUNK = 1024
ADA_TN = 1536
VMEM_LIMIT = 56 * 1024 * 1024

NEG = -1e30
LOG2E = 1.4426950408889634

_NT = (((1,), (1,)), ((), ()))
_TN = (((0,), (0,)), ((), ()))


def _dot(a, b):
    return jnp.dot(a, b, preferred_element_type=F32)


def _sigmoid(x):
    return 1.0 / (1.0 + jnp.exp(-x))


def _rms(x, w):
    return x * lax.rsqrt(jnp.mean(x * x, axis=-1, keepdims=True) + EPS) * w


def _const_spec(shape):
    nd = len(shape)
    return pl.BlockSpec(shape, lambda *_: (0,) * nd, pipeline_mode=pl.Buffered(1))


def _ada_kernel(c_ref, w_ref, b_ref, o_ref):
    c = c_ref[...]
    a = (c * _sigmoid(c)).astype(BF16)
    o_ref[...] = _dot(a, w_ref[...].astype(BF16)) + b_ref[...]


def _ada(c8, w_ada, b_ada):
    n = w_ada.shape[1]
    return pl.pallas_call(
        _ada_kernel,
        grid=(n // ADA_TN,),
        in_specs=[
            pl.BlockSpec((8, D_MODEL), lambda j: (0, 0)),
            pl.BlockSpec((D_MODEL, ADA_TN), lambda j: (0, j)),
            pl.BlockSpec((1, ADA_TN), lambda j: (0, j)),
        ],
        out_specs=pl.BlockSpec((8, ADA_TN), lambda j: (0, j)),
        out_shape=jax.ShapeDtypeStruct((8, n), F32),
        compiler_params=pltpu.CompilerParams(
            dimension_semantics=("arbitrary",), vmem_limit_bytes=VMEM_LIMIT),
        name="ada",
    )(c8, w_ada, b_ada)


def _proj_kernel(x_ref, xh_ref, mod_ref, npre_ref,
                 wqk_ref, wv_ref, wo_ref, wga_ref, wgb_ref, ws_ref,
                 convw_ref, convb_ref, qks_ref, gbias_ref, qn_ref, kvn_ref,
                 wuq_ref, wukv_ref, cos_ref, sin_ref,
                 qk_out, v_out, og_out, ga_out, gb_out, gates_out,
                 qnope_out, qpe_out, knope_out, vb_out, kpe_out,
                 hext_ref, z_ref, *, tm, tiles_per_seq, q_scale):
    i = pl.program_id(0)
    first = (i % tiles_per_seq) == 0
    shift = mod_ref[0:1, :]
    scale1 = 1.0 + mod_ref[1:2, :]
    w = npre_ref[...]

    def prenorm(xv):
        return _rms(xv, w) * scale1 + shift

    hext_ref[HALO:, :] = prenorm(x_ref[...]).astype(BF16)
    hh = prenorm(xh_ref[...])
    hext_ref[0:HALO, :] = jnp.where(first, 0.0, hh).astype(BF16)

    z_ref[...] = _dot(hext_ref[...], wqk_ref[...])
    acc = convb_ref[...]
    for j in range(CONV_W):
        off = HALO - (CONV_W - 1) + j
        acc = acc + convw_ref[j:j + 1, :] * z_ref[off:off + tm, :]
    qk_out[...] = (acc * _sigmoid(acc) * qks_ref[...]).astype(BF16)

    h = hext_ref[HALO:, :]
    v_out[...] = _dot(h, wv_ref[...]).astype(BF16)
    og_out[...] = _sigmoid(_dot(h, wo_ref[...])).astype(BF16)
    ga_out[...] = _sigmoid(_dot(h, wga_ref[...])).astype(BF16)
    gb_out[...] = _sigmoid(_dot(h, wgb_ref[...])).astype(BF16)

    s = _dot(h, ws_ref[...])
    c_q = s[:, 0:A_QRANK]
    c_kv = s[:, A_QRANK:A_QRANK + A_KVRANK]
    kp = s[:, 640:768]
    kpr = s[:, 768:896]
    g = s[:, 896:1024] + gbias_ref[...]

    lane = lax.broadcasted_iota(jnp.int32, g.shape, 1)
    logsig = jnp.minimum(g, 0.0) - jnp.log1p(jnp.exp(-jnp.abs(g)))
    gates_out[...] = jnp.where(lane < M_HEADS, g, logsig)

    cos = cos_ref[...]
    sin = sin_ref[...]
    kpe_out[...] = (kp * cos + kpr * sin).astype(BF16)

    q = _dot(_rms(c_q, qn_ref[...]).astype(BF16), wuq_ref[...])
    qnope_out[...] = (q[:, 0:1024] * q_scale).astype(BF16)
    cos4 = jnp.concatenate([cos] * 4, axis=1)
    sin4 = jnp.concatenate([sin] * 4, axis=1)
    qpe = q[:, 1024:1536] * cos4 + q[:, 1536:2048] * sin4
    qpe_out[...] = (qpe * q_scale).astype(BF16)

    kv = _dot(_rms(c_kv, kvn_ref[...]).astype(BF16), wukv_ref[...])
    knope_out[...] = kv[:, 0:1024].astype(BF16)
    vb_out[...] = kv[:, 1024:2048].astype(BF16)


def _proj(x2, mod3, npre, wts, convw, convb, qks, gbias, qn, kvn, wuq, wukv,
          cos, sin, seq):
    t = x2.shape[0]
    tm = PROJ_TM
    tiles_per_seq = seq // tm
    hb = tm // HALO
    wqk, wv, wo, wga, wgb, ws = wts
    row = lambda i: (i, 0)
    out_w = [1024, 1024, 1024, 1024, 1024, 128, 1024, 512, 1024, 1024, 128]
    out_dt = [BF16] * 5 + [F32] + [BF16] * 5
    kern = functools.partial(
        _proj_kernel, tm=tm, tiles_per_seq=tiles_per_seq,
        q_scale=float((A_NOPE + A_ROPE) ** -0.5 * LOG2E))
    return pl.pallas_call(
        kern,
        grid=(t // tm,),
        in_specs=[
            pl.BlockSpec((tm, D_MODEL), row),
            pl.BlockSpec((HALO, D_MODEL), lambda i: (jnp.maximum(i * hb - 1, 0), 0)),
            pl.BlockSpec((None, N_MOD, D_MODEL), lambda i: (i // tiles_per_seq, 0, 0)),
            _const_spec((1, D_MODEL)),
            _const_spec(wqk.shape), _const_spec(wv.shape), _const_spec(wo.shape),
            _const_spec(wga.shape), _const_spec(wgb.shape), _const_spec(ws.shape),
            _const_spec(convw.shape), _const_spec(convb.shape), _const_spec(qks.shape),
            _const_spec(gbias.shape), _const_spec(qn.shape), _const_spec(kvn.shape),
            _const_spec(wuq.shape), _const_spec(wukv.shape),
            pl.BlockSpec((tm, 128), lambda i: (i % tiles_per_seq, 0)),
            pl.BlockSpec((tm, 128), lambda i: (i % tiles_per_seq, 0)),
        ],
        out_specs=[pl.BlockSpec((tm, n), row) for n in out_w],
        out_shape=[jax.ShapeDtypeStruct((t, n), d) for n, d in zip(out_w, out_dt)],
        scratch_shapes=[
            pltpu.VMEM((tm + HALO, D_MODEL), BF16),
            pltpu.VMEM((tm + HALO, 2 * MA_QK), F32),
        ],
        compiler_params=pltpu.CompilerParams(
            dimension_semantics=("arbitrary",), vmem_limit_bytes=VMEM_LIMIT),
        name="proj",
    )(x2, x2, mod3, npre, wqk, wv, wo, wga, wgb, ws, convw, convb, qks, gbias,
      qn, kvn, wuq, wukv, cos, sin)


def _split3(x):
    hi = x.astype(BF16)
    r = x - hi.astype(F32)
    mid = r.astype(BF16)
    lo = (r - mid.astype(F32)).astype(BF16)
    return hi, mid, lo


def _mlstm_kernel(qk_ref, v_ref, og_ref, gc_ref, gr_ref, hnw_ref, y_ref,
                  ct_ref, m_ref, *, L):
    @pl.when(pl.program_id(1) == 0)
    def _():
        ct_ref[...] = jnp.zeros_like(ct_ref)
        m_ref[...] = jnp.zeros_like(m_ref)

    n_seq = qk_ref.shape[0]
    nlt = L // 128
    row = lax.broadcasted_iota(jnp.int32, (L, L), 0)
    col = lax.broadcasted_iota(jnp.int32, (L, L), 1)
    causal = col <= row
    tri = causal.astype(BF16)
    ones = jnp.ones((L, 128), BF16)
    chains = [(g, h) for g in range(n_seq) for h in range(M_HEADS)]

    def q_of(g, h):
        return qk_ref[g, :, h * M_DQK:(h + 1) * M_DQK]

    def k_of(g, h):
        return qk_ref[g, :, MA_QK + h * M_DQK:MA_QK + (h + 1) * M_DQK]

    def v_aug_of(g, h):
        return jnp.concatenate([v_ref[g, :, h * M_DV:(h + 1) * M_DV], ones], axis=1)

    gates = []
    for g in range(n_seq):
        gc = gc_ref[g] * LOG2E
        gr = gr_ref[g] * LOG2E
        bcol_all = sum(_dot(tri, p) for p in _split3(gc))
        brow_all = sum(lax.dot_general(p, tri, _NT, preferred_element_type=F32)
                       for p in _split3(gr))
        gates.append((gc, gr, bcol_all, brow_all))

    def qk_product(c):
        return lax.dot_general(q_of(*c), k_of(*c), _NT, preferred_element_type=F32)

    def gate_and_mix(c, qk):
        g, h = c
        gc, gr, bcol_all, brow_all = gates[g]
        c_row = gr[h:h + 1, :] - brow_all[M_HEADS + h:M_HEADS + h + 1, :]
        i_rep = jnp.broadcast_to(gc[:, h:h + 1], (L, 128))
        b_rep = jnp.broadcast_to(bcol_all[:, M_HEADS + h:M_HEADS + h + 1], (L, 128))
        m_prev = m_ref[g, h:h + 1, :]
        d_tiles = [jnp.where(causal[:, t * 128:(t + 1) * 128],
                             b_rep + c_row[:, t * 128:(t + 1) * 128], NEG) for t in range(nlt)]
        a = b_rep + m_prev
        m_intra = jnp.max(functools.reduce(jnp.maximum, d_tiles), axis=1, keepdims=True)
        m_out = jnp.maximum(a, m_intra)
        s_tiles = [(qk[:, t * 128:(t + 1) * 128]
                    * jnp.exp2(d_tiles[t] - m_out)).astype(BF16) for t in range(nlt)]
        w_inter = jnp.exp2(a - m_out).astype(BF16)
        ct = ct_ref[g, h]
        lhs = jnp.concatenate(s_tiles + [w_inter * q_of(g, h)], axis=1)
        rhs = jnp.concatenate([v_aug_of(g, h), ct.astype(BF16)], axis=0)
        return _dot(lhs, rhs), (m_out, b_rep, i_rep, m_prev)

    def finish(c, nd, kept):
        g, h = c
        m_out, b_rep, i_rep, m_prev = kept
        num = nd[:, 0:M_DV]
        den = nd[:, M_DV:M_DV + 128]
        inv = 1.0 / jnp.maximum(jnp.abs(den), jnp.exp2(-m_out))
        ms = jnp.mean(num * num, axis=1, keepdims=True)
        f = inv * lax.rsqrt(inv * inv * ms + EPS)
        hn = num * jnp.concatenate([f, f], axis=1) * hnw_ref[:, h * M_DV:(h + 1) * M_DV]
        y_ref[g, :, h * M_DV:(h + 1) * M_DV] = (og_ref[g, :, h * M_DV:(h + 1) * M_DV]
                                                * hn.astype(BF16))

        b_last = b_rep[L - 1:L, :]
        g_prev = b_last + m_prev
        gl = b_last - b_rep + i_rep
        m_new = jnp.maximum(g_prev, jnp.max(gl, axis=0, keepdims=True))
        wk = jnp.exp2(gl - m_new).astype(BF16)
        decay = jnp.exp2(g_prev - m_new)
        wv = jnp.concatenate([wk] * 3, axis=1) * v_aug_of(g, h)
        ct_ref[g, h] = (jnp.concatenate([decay] * 3, axis=1) * ct_ref[g, h]
                        + lax.dot_general(k_of(g, h), wv, _TN, preferred_element_type=F32))
        m_ref[g, h:h + 1, :] = m_new

    qk = [qk_product(c) for c in chains]
    mixed = [gate_and_mix(c, s) for c, s in zip(chains, qk)]
    for c, (nd, kept) in zip(chains, mixed):
        finish(c, nd, kept)


def _mlstm(qk, v, og, gc, gr, hnw):
    b, seq, _ = qk.shape
    L = MLSTM_L
    G = MLSTM_G
    blk = lambda w: pl.BlockSpec((G, L, w), lambda bi, ci: (bi, ci, 0))
    return pl.pallas_call(
        functools.partial(_mlstm_kernel, L=L),
        grid=(b // G, seq // L),
        in_specs=[
            blk(2 * MA_QK), blk(MA_V), blk(MA_V), blk(128),
            pl.BlockSpec((G, 8, L), lambda bi, ci: (bi, 0, ci)),
            _const_spec(hnw.shape),
        ],
        out_specs=blk(MA_V),
        out_shape=jax.ShapeDtypeStruct((b, seq, MA_V), BF16),
        scratch_shapes=[
            pltpu.VMEM((G, M_HEADS, M_DQK, M_DV + 128), F32),
            pltpu.VMEM((G, 8, 128), F32),
        ],
        compiler_params=pltpu.CompilerParams(
            dimension_semantics=("parallel", "arbitrary"), vmem_limit_bytes=VMEM_LIMIT),
        name="mlstm",
    )(qk, v, og, gc, gr, hnw)


def _attn_kernel(qn_ref, qpe_ref, kn_ref, kpe_ref, v_ref, o_ref,
                 kcat_ref, vaug_ref, qcat_ref, m_ref, acc_ref, sa_ref, sb_ref, *, tq, tk, seq):
    assert tq == 2 * tk
    h = pl.program_id(1)
    n_tiles = seq // tq
    nt = tk // 128

    kcat_ref[:, 0:A_NOPE] = kn_ref[...]
    kcat_ref[:, A_NOPE:2 * A_NOPE] = kpe_ref[...]
    vaug_ref[:, 0:A_DV] = v_ref[...]
    vaug_ref[:, A_DV:2 * A_DV] = jnp.ones((seq, A_DV), BF16)
    lane = lax.broadcasted_iota(jnp.int32, (seq, 128), 1)
    own = ((lane // A_ROPE) == (h % 2)).astype(F32)
    qcat_ref[:, 0:A_NOPE] = qn_ref[...]
    qcat_ref[:, A_NOPE:2 * A_NOPE] = (qpe_ref[...].astype(F32) * own).astype(BF16)

    def logits(qt, kb, r0=0, nrows=tq):
        q0 = pl.multiple_of(qt * tq + r0, tk)
        k0 = pl.multiple_of(kb * tk, tk)
        return lax.dot_general(qcat_ref[pl.ds(q0, nrows), :], kcat_ref[pl.ds(k0, tk), :],
                               _NT, preferred_element_type=F32)

    def softmax_pv(s, kb, r0, nrows, masked):
        k0 = pl.multiple_of(kb * tk, tk)
        if masked:
            r = lax.broadcasted_iota(jnp.int32, (nrows, tk), 0)
            c = lax.broadcasted_iota(jnp.int32, (nrows, tk), 1)
            s = jnp.where(c <= r, s, NEG)
        tiles = [s[:, t * 128:(t + 1) * 128] for t in range(nt)]
        mx = functools.reduce(jnp.maximum, tiles)
        m_prev = m_ref[r0:r0 + nrows, :]
        m_new = jnp.maximum(m_prev, jnp.max(mx, axis=1, keepdims=True))
        alpha = jnp.exp2(m_prev - m_new)
        p = jnp.concatenate([jnp.exp2(t - m_new).astype(BF16) for t in tiles], axis=1)
        pv = _dot(p, vaug_ref[pl.ds(k0, tk), :])
        acc_ref[r0:r0 + nrows, :] = (jnp.concatenate([alpha, alpha], axis=1)
                                     * acc_ref[r0:r0 + nrows, :] + pv)
        m_ref[r0:r0 + nrows, :] = m_new

    sa_ref[...] = logits(0, 0)

    def tile_body(qt, carry):
        m_ref[...] = jnp.full_like(m_ref, NEG)
        acc_ref[...] = jnp.zeros_like(acc_ref)

        def pair_body(i, c):
            j = 2 * i
            sb_ref[...] = logits(qt, j + 1)
            softmax_pv(sa_ref[...], j, 0, tq, False)
            sa_ref[...] = logits(qt, j + 2)
            softmax_pv(sb_ref[...], j + 1, 0, tq, False)
            return c

        lax.fori_loop(0, qt, pair_body, 0)

        d0 = 2 * qt
        sb_ref[0:tk, :] = logits(qt, d0 + 1, tk, tk)
        softmax_pv(sa_ref[...], d0, 0, tq, True)
        sa_ref[...] = logits(jnp.minimum(qt + 1, n_tiles - 1), 0)
        softmax_pv(sb_ref[0:tk, :], d0 + 1, tk, tk, True)
        o0 = pl.multiple_of(qt * tq, tq)
        o_ref[pl.ds(o0, tq), :] = (acc_ref[:, 0:A_DV] / acc_ref[:, A_DV:2 * A_DV]).astype(BF16)
        return carry

    lax.fori_loop(0, n_tiles, tile_body, 0)


def _attn(qn, qpe, kn, kpe, v):
    b, seq, _ = qn.shape
    tq, tk = ATTN_TQ, ATTN_TK
    return pl.pallas_call(
        functools.partial(_attn_kernel, tq=tq, tk=tk, seq=seq),
        grid=(b, A_HEADS),
        in_specs=[
            pl.BlockSpec((None, seq, A_NOPE), lambda bi, h: (bi, 0, h)),
            pl.BlockSpec((None, seq, 128), lambda bi, h: (bi, 0, h // 2)),
            pl.BlockSpec((None, seq, A_NOPE), lambda bi, h: (bi, 0, h)),
            pl.BlockSpec((None, seq, 128), lambda bi, h: (bi, 0, 0)),
            pl.BlockSpec((None, seq, A_DV), lambda bi, h: (bi, 0, h)),
        ],
        out_specs=pl.BlockSpec((None, seq, A_DV), lambda bi, h: (bi, 0, h)),
        out_shape=jax.ShapeDtypeStruct((b, seq, A_HEADS * A_DV), BF16),
        scratch_shapes=[
            pltpu.VMEM((seq, 2 * A_NOPE), BF16),
            pltpu.VMEM((seq, 2 * A_DV), BF16),
            pltpu.VMEM((seq, 2 * A_NOPE), BF16),
            pltpu.VMEM((tq, 128), F32),
            pltpu.VMEM((tq, 2 * A_DV), F32),
            pltpu.VMEM((tq, tk), F32),
            pltpu.VMEM((tq, tk), F32),
        ],
        compiler_params=pltpu.CompilerParams(
            dimension_semantics=("parallel", "arbitrary"),
            vmem_limit_bytes=VMEM_LIMIT),
        name="attn",
    )(qn, qpe, kn, kpe, v)


def _mlp_kernel(x_ref, ya_ref, yb_ref, ga_ref, gb_ref, mod_ref,
                npost_ref, npre2_ref, npost2_ref, wout_ref, w1_ref, w2_ref,
                o_ref, acc_ref):
    gate_m = mod_ref[2:3, :]
    shift_f = mod_ref[3:4, :]
    scale_f = 1.0 + mod_ref[4:5, :]
    gate_f = mod_ref[5:6, :]

    y = (ga_ref[...].astype(F32) * ya_ref[...].astype(F32)
         + gb_ref[...].astype(F32) * yb_ref[...].astype(F32))
    yo = _dot(y.astype(BF16), wout_ref[...])
    x1 = x_ref[...] + gate_m * _rms(yo, npost_ref[...])

    h2 = (_rms(x1, npre2_ref[...]) * scale_f + shift_f).astype(BF16)
    for c in range(D_FF // FF_CHUNK):
        u = jnp.maximum(_dot(h2, w1_ref[:, c * FF_CHUNK:(c + 1) * FF_CHUNK]), 0.0)
        part = _dot((u * u).astype(BF16), w2_ref[c * FF_CHUNK:(c + 1) * FF_CHUNK, :])
        if c == 0:
            acc_ref[...] = part
        else:
            acc_ref[...] += part
    o_ref[...] = x1 + gate_f * _rms(acc_ref[...], npost2_ref[...])


def _mlp(x2, ya, yb, ga, gb, mod3, npost, npre2, npost2, wout, w1, w2, seq):
    t = x2.shape[0]
    tm = MLP_TM
    tiles_per_seq = seq // tm
    row = pl.BlockSpec((tm, D_MODEL), lambda i: (i, 0))
    return pl.pallas_call(
        _mlp_kernel,
        grid=(t // tm,),
        in_specs=[
            row, row, row, row, row,
            pl.BlockSpec((None, N_MOD, D_MODEL), lambda i: (i // tiles_per_seq, 0, 0)),
            _const_spec((1, D_MODEL)), _const_spec((1, D_MODEL)), _const_spec((1, D_MODEL)),
            _const_spec(wout.shape), _const_spec(w1.shape), _const_spec(w2.shape),
        ],
        out_specs=row,
        out_shape=jax.ShapeDtypeStruct((t, D_MODEL), F32),
        scratch_shapes=[pltpu.VMEM((tm, D_MODEL), F32)],
        compiler_params=pltpu.CompilerParams(
            dimension_semantics=("arbitrary",), vmem_limit_bytes=VMEM_LIMIT),
        name="mlp",
    )(x2, ya, yb, ga, gb, mod3, npost, npre2, npost2, wout, w1, w2)


def _rot_half_cols(w):
    half = w.shape[-1] // 2
    return jnp.concatenate([-w[..., half:], w[..., :half]], axis=-1)


def _prep_in_weights(w_in):
    o = 0
    parts = []
    for n in (MA_QK, MA_QK, MA_V, MA_V, M_HEADS, M_HEADS, A_QRANK, A_KVRANK, A_ROPE,
              D_MODEL, D_MODEL):
        parts.append(w_in[:, o:o + n])
        o += n
    w_q, w_k, w_v, w_o, w_i, w_f, w_cq, w_ckv, w_kpe, w_ga, w_gb = parts
    kpe_rot = _rot_half_cols(w_kpe)
    pad = jnp.zeros((w_in.shape[0], 128 - 2 * M_HEADS), w_in.dtype)
    w_s = jnp.concatenate([w_cq, w_ckv, w_kpe, w_kpe, kpe_rot, kpe_rot, w_i, w_f, pad], axis=1)
    w_qk = jnp.concatenate([w_q, w_k], axis=1)
    return tuple(t.astype(BF16) for t in (w_qk, w_v, w_o, w_ga, w_gb, w_s))


def _prep_mla_weights(w_uq, w_ukv):
    r = w_uq.reshape(A_QRANK, A_HEADS, A_NOPE + A_ROPE)
    nope = r[:, :, :A_NOPE].reshape(A_QRANK, A_HEADS * A_NOPE)
    pe = r[:, :, A_NOPE:]
    pe_rot = _rot_half_cols(pe).reshape(A_QRANK, A_HEADS * A_ROPE)
    pe = pe.reshape(A_QRANK, A_HEADS * A_ROPE)
    wuq = jnp.concatenate([nope, pe, pe_rot], axis=1).astype(BF16)
    r = w_ukv.reshape(A_KVRANK, A_HEADS, A_NOPE + A_DV)
    wukv = jnp.concatenate([r[:, :, :A_NOPE].reshape(A_KVRANK, -1),
                            r[:, :, A_NOPE:].reshape(A_KVRANK, -1)], axis=1).astype(BF16)
    return wuq, wukv


def _rope_tables(seq):
    half = A_ROPE // 2
    inv_freq = ROPE_THETA ** (-jnp.arange(half, dtype=F32) / half)
    ang = jnp.arange(seq, dtype=jnp.int32).astype(F32)[:, None] * inv_freq[None, :]
    reps = 128 // half
    return jnp.tile(jnp.cos(ang), (1, reps)), jnp.tile(jnp.sin(ang), (1, reps))


def kernel(x, c, w_ada, b_ada, norm_pre_mix, norm_post_mix, norm_pre_mlp, norm_post_mlp,
           w_in, mlstm_conv_w, mlstm_conv_b, mlstm_gate_b, mlstm_head_norm,
           mla_q_norm, mla_kv_norm, w_uq, w_ukv, w_out, w_ff1, w_ff2):
    bsz, seq, d = x.shape
    depth = w_ada.shape[0]
    cos, sin = _rope_tables(seq)
    c8 = jnp.pad(c, ((0, 8 - bsz), (0, 0)))
    qks = jnp.concatenate([jnp.ones((1, MA_QK), F32),
                           jnp.full((1, MA_QK), M_DQK ** -0.5, F32)], axis=1)
    row = lambda a: a.reshape(1, -1)
    x2 = x.reshape(bsz * seq, d)
    for l in range(depth):
        mod = _ada(c8, w_ada[l], row(b_ada[l]))[:bsz]
        mod3 = mod.reshape(bsz, N_MOD, d)
        wts = _prep_in_weights(w_in[l])
        wuq, wukv = _prep_mla_weights(w_uq[l], w_ukv[l])
        gbias = jnp.pad(row(mlstm_gate_b[l]), ((0, 0), (0, 128 - 2 * M_HEADS)))
        (qk, v_a, og, ga, gb, gates, qnope, qpe, knope, v_b, kpe) = _proj(
            x2, mod3, row(norm_pre_mix[l]), wts, mlstm_conv_w[l], row(mlstm_conv_b[l]),
            qks, gbias, row(mla_q_norm[l]), row(mla_kv_norm[l]), wuq, wukv, cos, sin, seq)
        b3 = lambda a: a.reshape(bsz, seq, a.shape[-1])
        gc = b3(gates)
        gr = jnp.transpose(gc[:, :, :8], (0, 2, 1))
        y_a = _mlstm(b3(qk), b3(v_a), b3(og), gc, gr, row(mlstm_head_norm[l]))
        y_b = _attn(b3(qnope), b3(qpe), b3(knope), b3(kpe), b3(v_b))
        x2 = _mlp(x2, y_a.reshape(bsz * seq, d), y_b.reshape(bsz * seq, d), ga, gb, mod3,
                  row(norm_post_mix[l]), row(norm_pre_mlp[l]), row(norm_post_mlp[l]),
                  w_out[l].astype(BF16), w_ff1[l].astype(BF16), w_ff2[l].astype(BF16), seq)
    return x2.reshape(bsz, seq, d)
```

```python
import functools

import jax
import jax.numpy as jnp
from jax import lax
from jax.experimental import pallas as pl
from jax.experimental.pallas import tpu as pltpu

F32 = jnp.float32
BF16 = jnp.bfloat16

D_MODEL = 1024
M_HEADS = 4
M_DQK = 128
M_DV = 256
CONV_W = 4
A_HEADS = 8
A_NOPE = 128
A_ROPE = 64
A_DV = 128
A_QRANK = 384
A_KVRANK = 256
ROPE_THETA = 10000.0
D_FF = 4096
EPS = 1e-6
N_MOD = 6
MA_QK = M_HEADS * M_DQK
MA_V = M_HEADS * M_DV

PROJ_TM = 512
HALO = 16
MLSTM_L = 256
MLSTM_G = 4
ATTN_TQ = 1024
ATTN_TK = 512
MLP_TM = 512
FF_CHUNK = 1024
ADA_TN = 1536
REGROUP_TK = 256
VMEM_LIMIT = 56 * 1024 * 1024

NEG = -1e30
LOG2E = 1.4426950408889634

_NT = (((1,), (1,)), ((), ()))
_TN = (((0,), (0,)), ((), ()))


def _dot(a, b):
    return jnp.dot(a, b, preferred_element_type=F32)


def _sigmoid(x):
    return 1.0 / (1.0 + jnp.exp(-x))


def _rms(x, w):
    return x * lax.rsqrt(jnp.mean(x * x, axis=-1, keepdims=True) + EPS) * w


def _const_spec(shape):
    nd = len(shape)
    return pl.BlockSpec(shape, lambda *_: (0,) * nd, pipeline_mode=pl.Buffered(1))


def _ada_kernel(c_ref, w_ref, b_ref, o_ref):
    c = c_ref[...]
    a = (c * _sigmoid(c)).astype(BF16)
    o_ref[...] = _dot(a, w_ref[...].astype(BF16)) + b_ref[...]


def _ada(c8, w_ada, b_ada):
    n = w_ada.shape[1]
    return pl.pallas_call(
        _ada_kernel,
        grid=(n // ADA_TN,),
        in_specs=[
            pl.BlockSpec((8, D_MODEL), lambda j: (0, 0)),
            pl.BlockSpec((D_MODEL, ADA_TN), lambda j: (0, j)),
            pl.BlockSpec((1, ADA_TN), lambda j: (0, j)),
        ],
        out_specs=pl.BlockSpec((8, ADA_TN), lambda j: (0, j)),
        out_shape=jax.ShapeDtypeStruct((8, n), F32),
        compiler_params=pltpu.CompilerParams(
            dimension_semantics=("arbitrary",), vmem_limit_bytes=VMEM_LIMIT),
        name="ada",
    )(c8, w_ada, b_ada)


def _proj_kernel(x_ref, xh_ref, mod_ref, npre_ref,
                 win_ref,
                 convw_ref, convb_ref, qks_ref, gbias_ref, qn_ref, kvn_ref,
                 wuq_ref, wukv_ref, cos_ref, sin_ref,
                 qk_out, v_out, og_out, ga_out, gb_out, gates_out,
                 qnope_out, qpe_out, knope_out, vb_out, kpe_out,
                 hext_ref, z_ref, *, tm, tiles_per_seq, q_scale):
    i = pl.program_id(0)
    first = (i % tiles_per_seq) == 0
    wqk_ref, wv_ref, wo_ref, wga_ref, wgb_ref, ws_ref = (
        win_ref.at[n * D_MODEL:(n + 1) * D_MODEL, :] for n in range(6))

    def proj(a, wt_ref):
        return lax.dot_general(a, wt_ref[...], _NT, preferred_element_type=F32)

    shift = mod_ref[0:1, :]
    scale1 = 1.0 + mod_ref[1:2, :]
    w = npre_ref[...]

    def prenorm(xv):
        return _rms(xv, w) * scale1 + shift

    hext_ref[HALO:, :] = prenorm(x_ref[...]).astype(BF16)
    hh = prenorm(xh_ref[...])
    hext_ref[0:HALO, :] = jnp.where(first, 0.0, hh).astype(BF16)

    z_ref[...] = proj(hext_ref[...], wqk_ref)
    acc = convb_ref[...]
    for j in range(CONV_W):
        off = HALO - (CONV_W - 1) + j
        acc = acc + convw_ref[j:j + 1, :] * z_ref[off:off + tm, :]
    qk_out[...] = (acc * _sigmoid(acc) * qks_ref[...]).astype(BF16)

    h = hext_ref[HALO:, :]
    v_out[...] = proj(h, wv_ref).astype(BF16)
    og_out[...] = _sigmoid(proj(h, wo_ref)).astype(BF16)
    ga_out[...] = _sigmoid(proj(h, wga_ref)).astype(BF16)
    gb_out[...] = _sigmoid(proj(h, wgb_ref)).astype(BF16)

    s = proj(h, ws_ref)
    c_q = s[:, 0:A_QRANK]
    c_kv = s[:, A_QRANK:A_QRANK + A_KVRANK]
    kp = s[:, 640:768]
    kpr = s[:, 768:896]
    g = s[:, 896:1024] + gbias_ref[...]

    lane = lax.broadcasted_iota(jnp.int32, g.shape, 1)
    logsig = jnp.minimum(g, 0.0) - jnp.log1p(jnp.exp(-jnp.abs(g)))
    gates_out[...] = jnp.where(lane < M_HEADS, g, logsig)

    cos = cos_ref[...]
    sin = sin_ref[...]
    kpe_out[...] = (kp * cos + kpr * sin).astype(BF16)

    q = _dot(_rms(c_q, qn_ref[...]).astype(BF16), wuq_ref[...])
    qnope_out[...] = (q[:, 0:1024] * q_scale).astype(BF16)
    cos4 = jnp.concatenate([cos] * 4, axis=1)
    sin4 = jnp.concatenate([sin] * 4, axis=1)
    qpe = q[:, 1024:1536] * cos4 + q[:, 1536:2048] * sin4
    qpe_out[...] = (qpe * q_scale).astype(BF16)

    kv = _dot(_rms(c_kv, kvn_ref[...]).astype(BF16), wukv_ref[...])
    knope_out[...] = kv[:, 0:1024].astype(BF16)
    vb_out[...] = kv[:, 1024:2048].astype(BF16)


def _proj(x2, mod3, npre, win, convw, convb, qks, gbias, qn, kvn, wuq, wukv,
          cos, sin, seq):
    t = x2.shape[0]
    tm = PROJ_TM
    tiles_per_seq = seq // tm
    hb = tm // HALO
    row = lambda i: (i, 0)
    out_w = [1024, 1024, 1024, 1024, 1024, 128, 1024, 512, 1024, 1024, 128]
    out_dt = [BF16] * 5 + [F32] + [BF16] * 5
    kern = functools.partial(
        _proj_kernel, tm=tm, tiles_per_seq=tiles_per_seq,
        q_scale=float((A_NOPE + A_ROPE) ** -0.5 * LOG2E))
    return pl.pallas_call(
        kern,
        grid=(t // tm,),
        in_specs=[
            pl.BlockSpec((tm, D_MODEL), row),
            pl.BlockSpec((HALO, D_MODEL), lambda i: (jnp.maximum(i * hb - 1, 0), 0)),
            pl.BlockSpec((None, N_MOD, D_MODEL), lambda i: (i // tiles_per_seq, 0, 0)),
            _const_spec((1, D_MODEL)),
            _const_spec(win.shape),
            _const_spec(convw.shape), _const_spec(convb.shape), _const_spec(qks.shape),
            _const_spec(gbias.shape), _const_spec(qn.shape), _const_spec(kvn.shape),
            _const_spec(wuq.shape), _const_spec(wukv.shape),
            pl.BlockSpec((tm, 128), lambda i: (i % tiles_per_seq, 0)),
            pl.BlockSpec((tm, 128), lambda i: (i % tiles_per_seq, 0)),
        ],
        out_specs=[pl.BlockSpec((tm, n), row) for n in out_w],
        out_shape=[jax.ShapeDtypeStruct((t, n), d) for n, d in zip(out_w, out_dt)],
        scratch_shapes=[
            pltpu.VMEM((tm + HALO, D_MODEL), BF16),
            pltpu.VMEM((tm + HALO, 2 * MA_QK), F32),
        ],
        compiler_params=pltpu.CompilerParams(
            dimension_semantics=("arbitrary",), vmem_limit_bytes=VMEM_LIMIT),
        name="proj",
    )(x2, x2, mod3, npre, win, convw, convb, qks, gbias,
      qn, kvn, wuq, wukv, cos, sin)


def _split3(x):
    hi = x.astype(BF16)
    r = x - hi.astype(F32)
    mid = r.astype(BF16)
    lo = (r - mid.astype(F32)).astype(BF16)
    return hi, mid, lo


def _mlstm_kernel(qk_ref, v_ref, og_ref, gc_ref, gr_ref, hnw_ref, y_ref,
                  ct_ref, m_ref, *, L):
    @pl.when(pl.program_id(1) == 0)
    def _():
        ct_ref[...] = jnp.zeros_like(ct_ref)
        m_ref[...] = jnp.zeros_like(m_ref)

    n_seq = qk_ref.shape[0]
    nlt = L // 128
    row = lax.broadcasted_iota(jnp.int32, (L, L), 0)
    col = lax.broadcasted_iota(jnp.int32, (L, L), 1)
    causal = col <= row
    tri = causal.astype(BF16)
    ones = jnp.ones((L, 128), BF16)
    chains = [(g, h) for g in range(n_seq) for h in range(M_HEADS)]

    def q_of(g, h):
        return qk_ref[g, :, h * M_DQK:(h + 1) * M_DQK]

    def k_of(g, h):
        return qk_ref[g, :, MA_QK + h * M_DQK:MA_QK + (h + 1) * M_DQK]

    def v_aug_of(g, h):
        return jnp.concatenate([v_ref[g, :, h * M_DV:(h + 1) * M_DV], ones], axis=1)

    gates = []
    for g in range(n_seq):
        gc = gc_ref[g] * LOG2E
        gr = gr_ref[g] * LOG2E
        bcol_all = sum(_dot(tri, p) for p in _split3(gc))
        brow_all = sum(lax.dot_general(p, tri, _NT, preferred_element_type=F32)
                       for p in _split3(gr))
        gates.append((gc, gr, bcol_all, brow_all))

    def qk_product(c):
        return lax.dot_general(q_of(*c), k_of(*c), _NT, preferred_element_type=F32)

    def gate_and_mix(c, qk):
        g, h = c
        gc, gr, bcol_all, brow_all = gates[g]
        c_row = gr[h:h + 1, :] - brow_all[M_HEADS + h:M_HEADS + h + 1, :]
        i_rep = jnp.broadcast_to(gc[:, h:h + 1], (L, 128))
        b_rep = jnp.broadcast_to(bcol_all[:, M_HEADS + h:M_HEADS + h + 1], (L, 128))
        m_prev = m_ref[g, h:h + 1, :]
        d_tiles = [jnp.where(causal[:, t * 128:(t + 1) * 128],
                             b_rep + c_row[:, t * 128:(t + 1) * 128], NEG) for t in range(nlt)]
        a = b_rep + m_prev
        m_intra = jnp.max(functools.reduce(jnp.maximum, d_tiles), axis=1, keepdims=True)
        m_out = jnp.maximum(a, m_intra)
        s_tiles = [(qk[:, t * 128:(t + 1) * 128]
                    * jnp.exp2(d_tiles[t] - m_out)).astype(BF16) for t in range(nlt)]
        w_inter = jnp.exp2(a - m_out).astype(BF16)
        ct = ct_ref[g, h]
        lhs = jnp.concatenate(s_tiles + [w_inter * q_of(g, h)], axis=1)
        rhs = jnp.concatenate([v_aug_of(g, h), ct.astype(BF16)], axis=0)
        return _dot(lhs, rhs), (m_out, b_rep, i_rep, m_prev)

    def finish(c, nd, kept):
        g, h = c
        m_out, b_rep, i_rep, m_prev = kept
        num = nd[:, 0:M_DV]
        den = nd[:, M_DV:M_DV + 128]
        inv = 1.0 / jnp.maximum(jnp.abs(den), jnp.exp2(-m_out))
        ms = jnp.mean(num * num, axis=1, keepdims=True)
        f = inv * lax.rsqrt(inv * inv * ms + EPS)
        hn = num * jnp.concatenate([f, f], axis=1) * hnw_ref[:, h * M_DV:(h + 1) * M_DV]
        y_ref[g, :, h * M_DV:(h + 1) * M_DV] = (og_ref[g, :, h * M_DV:(h + 1) * M_DV]
                                                * hn.astype(BF16))

        b_last = b_rep[L - 1:L, :]
        g_prev = b_last + m_prev
        gl = b_last - b_rep + i_rep
        m_new = jnp.maximum(g_prev, jnp.max(gl, axis=0, keepdims=True))
        wk = jnp.exp2(gl - m_new).astype(BF16)
        decay = jnp.exp2(g_prev - m_new)
        wv = jnp.concatenate([wk] * 3, axis=1) * v_aug_of(g, h)
        ct_ref[g, h] = (jnp.concatenate([decay] * 3, axis=1) * ct_ref[g, h]
                        + lax.dot_general(k_of(g, h), wv, _TN, preferred_element_type=F32))
        m_ref[g, h:h + 1, :] = m_new

    qk = [qk_product(c) for c in chains]
    mixed = [gate_and_mix(c, s) for c, s in zip(chains, qk)]
    for c, (nd, kept) in zip(chains, mixed):
        finish(c, nd, kept)


def _mlstm(qk, v, og, gc, gr, hnw):
    b, seq, _ = qk.shape
    L = MLSTM_L
    G = MLSTM_G
    blk = lambda w: pl.BlockSpec((G, L, w), lambda bi, ci: (bi, ci, 0))
    return pl.pallas_call(
        functools.partial(_mlstm_kernel, L=L),
        grid=(b // G, seq // L),
        in_specs=[
            blk(2 * MA_QK), blk(MA_V), blk(MA_V), blk(128),
            pl.BlockSpec((G, 8, L), lambda bi, ci: (bi, 0, ci)),
            _const_spec(hnw.shape),
        ],
        out_specs=blk(MA_V),
        out_shape=jax.ShapeDtypeStruct((b, seq, MA_V), BF16),
        scratch_shapes=[
            pltpu.VMEM((G, M_HEADS, M_DQK, M_DV + 128), F32),
            pltpu.VMEM((G, 8, 128), F32),
        ],
        compiler_params=pltpu.CompilerParams(
            dimension_semantics=("parallel", "arbitrary"), vmem_limit_bytes=VMEM_LIMIT),
        name="mlstm",
    )(qk, v, og, gc, gr, hnw)


def _attn_kernel(qn_ref, qpe_ref, kn_ref, kpe_ref, v_ref, o_ref,
                 kcat_ref, vaug_ref, qcat_ref, m_ref, acc_ref, sa_ref, sb_ref, *, tq, tk, seq):
    assert tq == 2 * tk
    h = pl.program_id(1)
    n_tiles = seq // tq
    nt = tk // 128

    kcat_ref[:, 0:A_NOPE] = kn_ref[...]
    kcat_ref[:, A_NOPE:2 * A_NOPE] = kpe_ref[...]
    vaug_ref[:, 0:A_DV] = v_ref[...]
    vaug_ref[:, A_DV:2 * A_DV] = jnp.ones((seq, A_DV), BF16)
    lane = lax.broadcasted_iota(jnp.int32, (seq, 128), 1)
    own = ((lane // A_ROPE) == (h % 2)).astype(F32)
    qcat_ref[:, 0:A_NOPE] = qn_ref[...]
    qcat_ref[:, A_NOPE:2 * A_NOPE] = (qpe_ref[...].astype(F32) * own).astype(BF16)

    def logits(qt, kb, r0=0, nrows=tq):
        q0 = pl.multiple_of(qt * tq + r0, tk)
        k0 = pl.multiple_of(kb * tk, tk)
        return lax.dot_general(qcat_ref[pl.ds(q0, nrows), :], kcat_ref[pl.ds(k0, tk), :],
                               _NT, preferred_element_type=F32)

    def softmax_pv(s, kb, r0, nrows, masked):
        k0 = pl.multiple_of(kb * tk, tk)
        if masked:
            r = lax.broadcasted_iota(jnp.int32, (nrows, tk), 0)
            c = lax.broadcasted_iota(jnp.int32, (nrows, tk), 1)
            s = jnp.where(c <= r, s, NEG)
        tiles = [s[:, t * 128:(t + 1) * 128] for t in range(nt)]
        mx = functools.reduce(jnp.maximum, tiles)
        m_prev = m_ref[r0:r0 + nrows, :]
        m_new = jnp.maximum(m_prev, jnp.max(mx, axis=1, keepdims=True))
        alpha = jnp.exp2(m_prev - m_new)
        p = jnp.concatenate([jnp.exp2(t - m_new).astype(BF16) for t in tiles], axis=1)
        pv = _dot(p, vaug_ref[pl.ds(k0, tk), :])
        acc_ref[r0:r0 + nrows, :] = (jnp.concatenate([alpha, alpha], axis=1)
                                     * acc_ref[r0:r0 + nrows, :] + pv)
        m_ref[r0:r0 + nrows, :] = m_new

    sa_ref[...] = logits(0, 0)

    def tile_body(qt, carry):
        m_ref[...] = jnp.full_like(m_ref, NEG)
        acc_ref[...] = jnp.zeros_like(acc_ref)

        def pair_body(i, c):
            j = 2 * i
            sb_ref[...] = logits(qt, j + 1)
            softmax_pv(sa_ref[...], j, 0, tq, False)
            sa_ref[...] = logits(qt, j + 2)
            softmax_pv(sb_ref[...], j + 1, 0, tq, False)
            return c

        lax.fori_loop(0, qt, pair_body, 0)

        d0 = 2 * qt
        sb_ref[0:tk, :] = logits(qt, d0 + 1, tk, tk)
        softmax_pv(sa_ref[...], d0, 0, tq, True)
        sa_ref[...] = logits(jnp.minimum(qt + 1, n_tiles - 1), 0)
        softmax_pv(sb_ref[0:tk, :], d0 + 1, tk, tk, True)
        o0 = pl.multiple_of(qt * tq, tq)
        o_ref[pl.ds(o0, tq), :] = (acc_ref[:, 0:A_DV] / acc_ref[:, A_DV:2 * A_DV]).astype(BF16)
        return carry

    lax.fori_loop(0, n_tiles, tile_body, 0)


def _attn(qn, qpe, kn, kpe, v):
    b, seq, _ = qn.shape
    tq, tk = ATTN_TQ, ATTN_TK
    return pl.pallas_call(
        functools.partial(_attn_kernel, tq=tq, tk=tk, seq=seq),
        grid=(b, A_HEADS),
        in_specs=[
            pl.BlockSpec((None, seq, A_NOPE), lambda bi, h: (bi, 0, h)),
            pl.BlockSpec((None, seq, 128), lambda bi, h: (bi, 0, h // 2)),
            pl.BlockSpec((None, seq, A_NOPE), lambda bi, h: (bi, 0, h)),
            pl.BlockSpec((None, seq, 128), lambda bi, h: (bi, 0, 0)),
            pl.BlockSpec((None, seq, A_DV), lambda bi, h: (bi, 0, h)),
        ],
        out_specs=pl.BlockSpec((None, seq, A_DV), lambda bi, h: (bi, 0, h)),
        out_shape=jax.ShapeDtypeStruct((b, seq, A_HEADS * A_DV), BF16),
        scratch_shapes=[
            pltpu.VMEM((seq, 2 * A_NOPE), BF16),
            pltpu.VMEM((seq, 2 * A_DV), BF16),
            pltpu.VMEM((seq, 2 * A_NOPE), BF16),
            pltpu.VMEM((tq, 128), F32),
            pltpu.VMEM((tq, 2 * A_DV), F32),
            pltpu.VMEM((tq, tk), F32),
            pltpu.VMEM((tq, tk), F32),
        ],
        compiler_params=pltpu.CompilerParams(
            dimension_semantics=("parallel", "arbitrary"),
            vmem_limit_bytes=VMEM_LIMIT),
        name="attn",
    )(qn, qpe, kn, kpe, v)


def _mlp_kernel(x_ref, ya_ref, yb_ref, ga_ref, gb_ref, mod_ref,
                npost_ref, npre2_ref, npost2_ref, wout_ref, w1_ref, w2_ref,
                o_ref, acc_ref):
    gate_m = mod_ref[2:3, :]
    shift_f = mod_ref[3:4, :]
    scale_f = 1.0 + mod_ref[4:5, :]
    gate_f = mod_ref[5:6, :]

    y = (ga_ref[...].astype(F32) * ya_ref[...].astype(F32)
         + gb_ref[...].astype(F32) * yb_ref[...].astype(F32))
    yo = _dot(y.astype(BF16), wout_ref[...])
    x1 = x_ref[...] + gate_m * _rms(yo, npost_ref[...])

    h2 = (_rms(x1, npre2_ref[...]) * scale_f + shift_f).astype(BF16)
    for c in range(D_FF // FF_CHUNK):
        u = jnp.maximum(_dot(h2, w1_ref[:, c * FF_CHUNK:(c + 1) * FF_CHUNK]), 0.0)
        part = _dot((u * u).astype(BF16), w2_ref[c * FF_CHUNK:(c + 1) * FF_CHUNK, :])
        if c == 0:
            acc_ref[...] = part
        else:
            acc_ref[...] += part
    o_ref[...] = x1 + gate_f * _rms(acc_ref[...], npost2_ref[...])


def _mlp(x2, ya, yb, ga, gb, mod3, npost, npre2, npost2, wout, w1, w2, seq):
    t = x2.shape[0]
    tm = MLP_TM
    tiles_per_seq = seq // tm
    row = pl.BlockSpec((tm, D_MODEL), lambda i: (i, 0))
    return pl.pallas_call(
        _mlp_kernel,
        grid=(t // tm,),
        in_specs=[
            row, row, row, row, row,
            pl.BlockSpec((None, N_MOD, D_MODEL), lambda i: (i // tiles_per_seq, 0, 0)),
            _const_spec((1, D_MODEL)), _const_spec((1, D_MODEL)), _const_spec((1, D_MODEL)),
            _const_spec(wout.shape), _const_spec(w1.shape), _const_spec(w2.shape),
        ],
        out_specs=row,
        out_shape=jax.ShapeDtypeStruct((t, D_MODEL), F32),
        scratch_shapes=[pltpu.VMEM((tm, D_MODEL), F32)],
        compiler_params=pltpu.CompilerParams(
            dimension_semantics=("arbitrary",), vmem_limit_bytes=VMEM_LIMIT),
        name="mlp",
    )(x2, ya, yb, ga, gb, mod3, npost, npre2, npost2, wout, w1, w2)


def _rot_half_cols(w):
    half = w.shape[-1] // 2
    return jnp.concatenate([-w[..., half:], w[..., :half]], axis=-1)


def _regroup_kernel(w_ref, o_ref):
    sizes = (MA_QK, MA_QK, MA_V, MA_V, M_HEADS, M_HEADS, A_QRANK, A_KVRANK, A_ROPE,
             D_MODEL, D_MODEL)
    offs = [sum(sizes[:n]) for n in range(len(sizes))]
    o_q, _, _, _, o_i, _, o_cq, _, o_kpe, o_ga, _ = offs
    half = A_ROPE // 2
    cursor = [0]

    def put(src, n, negate=False):
        blk = w_ref[src:src + n, :]
        o_ref[cursor[0]:cursor[0] + n, :] = (-blk if negate else blk).astype(BF16)
        cursor[0] += n

    put(o_q, 2 * MA_QK + 2 * MA_V)
    put(o_ga, 2 * D_MODEL)
    put(o_cq, A_QRANK + A_KVRANK)
    put(o_kpe, A_ROPE)
    put(o_kpe, A_ROPE)
    for _ in range(2):
        put(o_kpe + half, half, negate=True)
        put(o_kpe, half)
    gate_rows = jnp.concatenate(
        [w_ref[o_i:o_i + 2 * M_HEADS, :],
         jnp.zeros((128 - 2 * M_HEADS, w_ref.shape[1]), F32)], axis=0)
    o_ref[cursor[0]:cursor[0] + 128, :] = gate_rows.astype(BF16)


def _prep_in_weights(w_in_t):
    n_in, k = w_in_t.shape
    n_out = 6 * D_MODEL
    return pl.pallas_call(
        _regroup_kernel,
        grid=(k // REGROUP_TK,),
        in_specs=[pl.BlockSpec((n_in, REGROUP_TK), lambda i: (0, i))],
        out_specs=pl.BlockSpec((n_out, REGROUP_TK), lambda i: (0, i)),
        out_shape=jax.ShapeDtypeStruct((n_out, k), BF16),
        compiler_params=pltpu.CompilerParams(
            dimension_semantics=("arbitrary",), vmem_limit_bytes=VMEM_LIMIT),
        name="regroup",
    )(w_in_t)


def _prep_mla_weights(w_uq, w_ukv):
    r = w_uq.reshape(A_QRANK, A_HEADS, A_NOPE + A_ROPE)
    nope = r[:, :, :A_NOPE].reshape(A_QRANK, A_HEADS * A_NOPE)
    pe = r[:, :, A_NOPE:]
    pe_rot = _rot_half_cols(pe).reshape(A_QRANK, A_HEADS * A_ROPE)
    pe = pe.reshape(A_QRANK, A_HEADS * A_ROPE)
    wuq = jnp.concatenate([nope, pe, pe_rot], axis=1).astype(BF16)
    r = w_ukv.reshape(A_KVRANK, A_HEADS, A_NOPE + A_DV)
    wukv = jnp.concatenate([r[:, :, :A_NOPE].reshape(A_KVRANK, -1),
                            r[:, :, A_NOPE:].reshape(A_KVRANK, -1)], axis=1).astype(BF16)
    return wuq, wukv


def _rope_tables(seq):
    half = A_ROPE // 2
    inv_freq = ROPE_THETA ** (-jnp.arange(half, dtype=F32) / half)
    ang = jnp.arange(seq, dtype=jnp.int32).astype(F32)[:, None] * inv_freq[None, :]
    reps = 128 // half
    return jnp.tile(jnp.cos(ang), (1, reps)), jnp.tile(jnp.sin(ang), (1, reps))


def kernel(x, c, w_ada, b_ada, norm_pre_mix, norm_post_mix, norm_pre_mlp, norm_post_mlp,
           w_in, mlstm_conv_w, mlstm_conv_b, mlstm_gate_b, mlstm_head_norm,
           mla_q_norm, mla_kv_norm, w_uq, w_ukv, w_out, w_ff1, w_ff2):
    bsz, seq, d = x.shape
    depth = w_ada.shape[0]
    cos, sin = _rope_tables(seq)
    c8 = jnp.pad(c, ((0, 8 - bsz), (0, 0)))
    qks = jnp.concatenate([jnp.ones((1, MA_QK), F32),
                           jnp.full((1, MA_QK), M_DQK ** -0.5, F32)], axis=1)
    row = lambda a: a.reshape(1, -1)
    x2 = x.reshape(bsz * seq, d)
    for l in range(depth):
        mod = _ada(c8, w_ada[l], row(b_ada[l]))[:bsz]
        mod3 = mod.reshape(bsz, N_MOD, d)
        win = _prep_in_weights(jnp.transpose(w_in[l]))
        wuq, wukv = _prep_mla_weights(w_uq[l], w_ukv[l])
        gbias = jnp.pad(row(mlstm_gate_b[l]), ((0, 0), (0, 128 - 2 * M_HEADS)))
        (qk, v_a, og, ga, gb, gates, qnope, qpe, knope, v_b, kpe) = _proj(
            x2, mod3, row(norm_pre_mix[l]), win, mlstm_conv_w[l], row(mlstm_conv_b[l]),
            qks, gbias, row(mla_q_norm[l]), row(mla_kv_norm[l]), wuq, wukv, cos, sin, seq)
        b3 = lambda a: a.reshape(bsz, seq, a.shape[-1])
        gc = b3(gates)
        gr = jnp.transpose(gc[:, :, :8], (0, 2, 1))
        y_a = _mlstm(b3(qk), b3(v_a), b3(og), gc, gr, row(mlstm_head_norm[l]))
        y_b = _attn(b3(qnope), b3(qpe), b3(knope), b3(kpe), b3(v_b))
        x2 = _mlp(x2, y_a.reshape(bsz * seq, d), y_b.reshape(bsz * seq, d), ga, gb, mod3,
                  row(norm_post_mix[l]), row(norm_pre_mlp[l]), row(norm_post_mlp[l]),
                  w_out[l].astype(BF16), w_ff1[l].astype(BF16), w_ff2[l].astype(BF16), seq)
    return x2.reshape(bsz, seq, d)
```

```python
import functools

import jax
import jax.numpy as jnp
import numpy as np
from jax import lax
from jax.experimental import pallas as pl
from jax.experimental.pallas import tpu as pltpu

F32 = jnp.float32
BF16 = jnp.bfloat16

D_MODEL = 1024
M_HEADS = 4
M_DQK = 128
M_DV = 256
CONV_W = 4
A_HEADS = 8
A_NOPE = 128
A_ROPE = 64
A_DV = 128
A_QRANK = 384
A_KVRANK = 256
ROPE_THETA = 10000.0
D_FF = 4096
EPS = 1e-6
N_MOD = 6
MA_QK = M_HEADS * M_DQK
MA_V = M_HEADS * M_DV

PROJ_TM = 512
HALO = 16
MLSTM_L = 256
MLSTM_G = 4
ATTN_TQ = 1024
ATTN_TK = 512
MLP_TM = 512
FF_CHUNK = 1024
ADA_TN = 1536
REGROUP_TK = 256
VMEM_LIMIT = 56 * 1024 * 1024

NEG = -1e30
LOG2E = 1.4426950408889634

_NT = (((1,), (1,)), ((), ()))
_TN = (((0,), (0,)), ((), ()))


def _dot(a, b):
    return jnp.dot(a, b, preferred_element_type=F32)


def _sigmoid(x):
    return 1.0 / (1.0 + jnp.exp(-x))


def _rms(x, w):
    return x * lax.rsqrt(jnp.mean(x * x, axis=-1, keepdims=True) + EPS) * w


def _const_spec(shape):
    nd = len(shape)
    return pl.BlockSpec(shape, lambda *_: (0,) * nd, pipeline_mode=pl.Buffered(1))


def _ada_kernel(c_ref, w_ref, b_ref, o_ref):
    c = c_ref[...]
    a = (c * _sigmoid(c)).astype(BF16)
    o_ref[...] = _dot(a, w_ref[...].astype(BF16)) + b_ref[...]


def _ada(c8, w_ada, b_ada):
    n = w_ada.shape[1]
    return pl.pallas_call(
        _ada_kernel,
        grid=(n // ADA_TN,),
        in_specs=[
            pl.BlockSpec((8, D_MODEL), lambda j: (0, 0)),
            pl.BlockSpec((D_MODEL, ADA_TN), lambda j: (0, j)),
            pl.BlockSpec((1, ADA_TN), lambda j: (0, j)),
        ],
        out_specs=pl.BlockSpec((8, ADA_TN), lambda j: (0, j)),
        out_shape=jax.ShapeDtypeStruct((8, n), F32),
        compiler_params=pltpu.CompilerParams(
            dimension_semantics=("arbitrary",), vmem_limit_bytes=VMEM_LIMIT),
        name="ada",
    )(c8, w_ada, b_ada)


def _proj_kernel(x_ref, xh_ref, mod_ref, npre_ref,
                 win_ref,
                 convw_ref, convb_ref, qks_ref, gbias_ref, qn_ref, kvn_ref,
                 wuq_ref, wukv_ref, cos_ref, sin_ref,
                 qk_out, v_out, og_out, ga_out, gb_out, gates_out,
                 qnope_out, qpe_out, knope_out, vb_out, kpe_out,
                 hext_ref, z_ref, *, tm, tiles_per_seq, q_scale):
    i = pl.program_id(0)
    first = (i % tiles_per_seq) == 0
    wqk_ref, wv_ref, wo_ref, wga_ref, wgb_ref, ws_ref = (
        win_ref.at[n * D_MODEL:(n + 1) * D_MODEL, :] for n in range(6))

    def proj(a, wt_ref):
        return lax.dot_general(a, wt_ref[...], _NT, preferred_element_type=F32)

    shift = mod_ref[0:1, :]
    scale1 = 1.0 + mod_ref[1:2, :]
    w = npre_ref[...]

    def prenorm(xv):
        return _rms(xv, w) * scale1 + shift

    hext_ref[HALO:, :] = prenorm(x_ref[...]).astype(BF16)
    hh = prenorm(xh_ref[...])
    hext_ref[0:HALO, :] = jnp.where(first, 0.0, hh).astype(BF16)

    z_ref[...] = proj(hext_ref[...], wqk_ref)
    acc = convb_ref[...]
    for j in range(CONV_W):
        off = HALO - (CONV_W - 1) + j
        acc = acc + convw_ref[j:j + 1, :] * z_ref[off:off + tm, :]
    qk_out[...] = (acc * _sigmoid(acc) * qks_ref[...]).astype(BF16)

    h = hext_ref[HALO:, :]
    v_out[...] = proj(h, wv_ref).astype(BF16)
    og_out[...] = _sigmoid(proj(h, wo_ref)).astype(BF16)
    ga_out[...] = _sigmoid(proj(h, wga_ref)).astype(BF16)
    gb_out[...] = _sigmoid(proj(h, wgb_ref)).astype(BF16)

    s = proj(h, ws_ref)
    c_q = s[:, 0:A_QRANK]
    c_kv = s[:, A_QRANK:A_QRANK + A_KVRANK]
    kp = s[:, 640:768]
    kpr = s[:, 768:896]
    g = s[:, 896:1024] + gbias_ref[...]

    lane = lax.broadcasted_iota(jnp.int32, g.shape, 1)
    logsig = jnp.minimum(g, 0.0) - jnp.log1p(jnp.exp(-jnp.abs(g)))
    gates_out[...] = jnp.where(lane < M_HEADS, g, logsig)

    cos = cos_ref[...]
    sin = sin_ref[...]
    kpe_out[...] = (kp * cos + kpr * sin).astype(BF16)

    q = _dot(_rms(c_q, qn_ref[...]).astype(BF16), wuq_ref[...])
    qnope_out[...] = (q[:, 0:1024] * q_scale).astype(BF16)
    cos4 = jnp.concatenate([cos] * 4, axis=1)
    sin4 = jnp.concatenate([sin] * 4, axis=1)
    qpe = q[:, 1024:1536] * cos4 + q[:, 1536:2048] * sin4
    qpe_out[...] = (qpe * q_scale).astype(BF16)

    kv = _dot(_rms(c_kv, kvn_ref[...]).astype(BF16), wukv_ref[...])
    knope_out[...] = kv[:, 0:1024].astype(BF16)
    vb_out[...] = kv[:, 1024:2048].astype(BF16)


def _proj(x2, mod3, npre, win, convw, convb, qks, gbias, qn, kvn, wuq, wukv,
          cos, sin, seq):
    t = x2.shape[0]
    tm = PROJ_TM
    tiles_per_seq = seq // tm
    hb = tm // HALO
    row = lambda i: (i, 0)
    out_w = [1024, 1024, 1024, 1024, 1024, 128, 1024, 512, 1024, 1024, 128]
    out_dt = [BF16] * 5 + [F32] + [BF16] * 5
    kern = functools.partial(
        _proj_kernel, tm=tm, tiles_per_seq=tiles_per_seq,
        q_scale=float((A_NOPE + A_ROPE) ** -0.5 * LOG2E))
    return pl.pallas_call(
        kern,
        grid=(t // tm,),
        in_specs=[
            pl.BlockSpec((tm, D_MODEL), row),
            pl.BlockSpec((HALO, D_MODEL), lambda i: (jnp.maximum(i * hb - 1, 0), 0)),
            pl.BlockSpec((None, N_MOD, D_MODEL), lambda i: (i // tiles_per_seq, 0, 0)),
            _const_spec((1, D_MODEL)),
            _const_spec(win.shape),
            _const_spec(convw.shape), _const_spec(convb.shape), _const_spec(qks.shape),
            _const_spec(gbias.shape), _const_spec(qn.shape), _const_spec(kvn.shape),
            _const_spec(wuq.shape), _const_spec(wukv.shape),
            pl.BlockSpec((tm, 128), lambda i: (i % tiles_per_seq, 0)),
            pl.BlockSpec((tm, 128), lambda i: (i % tiles_per_seq, 0)),
        ],
        out_specs=[pl.BlockSpec((tm, n), row) for n in out_w],
        out_shape=[jax.ShapeDtypeStruct((t, n), d) for n, d in zip(out_w, out_dt)],
        scratch_shapes=[
            pltpu.VMEM((tm + HALO, D_MODEL), BF16),
            pltpu.VMEM((tm + HALO, 2 * MA_QK), F32),
        ],
        compiler_params=pltpu.CompilerParams(
            dimension_semantics=("arbitrary",), vmem_limit_bytes=VMEM_LIMIT),
        name="proj",
    )(x2, x2, mod3, npre, win, convw, convb, qks, gbias,
      qn, kvn, wuq, wukv, cos, sin)


def _split3(x):
    hi = x.astype(BF16)
    r = x - hi.astype(F32)
    mid = r.astype(BF16)
    lo = (r - mid.astype(F32)).astype(BF16)
    return hi, mid, lo


def _mlstm_kernel(qk_ref, v_ref, og_ref, gc_ref, hnw_ref, y_ref,
                  ct_ref, m_ref, *, L):
    @pl.when(pl.program_id(1) == 0)
    def _():
        ct_ref[...] = jnp.zeros_like(ct_ref)
        m_ref[...] = jnp.zeros_like(m_ref)

    n_seq = qk_ref.shape[0]
    nlt = L // 128
    row = lax.broadcasted_iota(jnp.int32, (L, L), 0)
    col = lax.broadcasted_iota(jnp.int32, (L, L), 1)
    causal = col <= row
    tri = causal.astype(BF16)
    ones = jnp.ones((L, 128), BF16)
    chains = [(g, h) for g in range(n_seq) for h in range(M_HEADS)]

    def q_of(g, h):
        return qk_ref[g, :, h * M_DQK:(h + 1) * M_DQK]

    def k_of(g, h):
        return qk_ref[g, :, MA_QK + h * M_DQK:MA_QK + (h + 1) * M_DQK]

    def v_aug_of(g, h):
        return jnp.concatenate([v_ref[g, :, h * M_DV:(h + 1) * M_DV], ones], axis=1)

    gates = []
    for g in range(n_seq):
        gc = gc_ref[g] * LOG2E
        gr = gc.T[0:2 * M_HEADS, :]
        bcol_all = sum(_dot(tri, p) for p in _split3(gc))
        brow_all = sum(lax.dot_general(p, tri, _NT, preferred_element_type=F32)
                       for p in _split3(gr))
        gates.append((gc, gr, bcol_all, brow_all))

    def qk_product(c):
        return lax.dot_general(q_of(*c), k_of(*c), _NT, preferred_element_type=F32)

    def gate_and_mix(c, qk):
        g, h = c
        gc, gr, bcol_all, brow_all = gates[g]
        c_row = gr[h:h + 1, :] - brow_all[M_HEADS + h:M_HEADS + h + 1, :]
        i_rep = jnp.broadcast_to(gc[:, h:h + 1], (L, 128))
        b_rep = jnp.broadcast_to(bcol_all[:, M_HEADS + h:M_HEADS + h + 1], (L, 128))
        m_prev = m_ref[g, h:h + 1, :]
        d_tiles = [jnp.where(causal[:, t * 128:(t + 1) * 128],
                             b_rep + c_row[:, t * 128:(t + 1) * 128], NEG) for t in range(nlt)]
        a = b_rep + m_prev
        m_intra = jnp.max(functools.reduce(jnp.maximum, d_tiles), axis=1, keepdims=True)
        m_out = jnp.maximum(a, m_intra)
        s_tiles = [(qk[:, t * 128:(t + 1) * 128]
                    * jnp.exp2(d_tiles[t] - m_out)).astype(BF16) for t in range(nlt)]
        w_inter = jnp.exp2(a - m_out).astype(BF16)
        ct = ct_ref[g, h]
        lhs = jnp.concatenate(s_tiles + [w_inter * q_of(g, h)], axis=1)
        rhs = jnp.concatenate([v_aug_of(g, h), ct.astype(BF16)], axis=0)
        return _dot(lhs, rhs), (m_out, b_rep, i_rep, m_prev)

    def finish(c, nd, kept):
        g, h = c
        m_out, b_rep, i_rep, m_prev = kept
        num = nd[:, 0:M_DV]
        den = nd[:, M_DV:M_DV + 128]
        inv = 1.0 / jnp.maximum(jnp.abs(den), jnp.exp2(-m_out))
        ms = jnp.mean(num * num, axis=1, keepdims=True)
        f = inv * lax.rsqrt(inv * inv * ms + EPS)
        hn = num * jnp.concatenate([f, f], axis=1) * hnw_ref[:, h * M_DV:(h + 1) * M_DV]
        y_ref[g, :, h * M_DV:(h + 1) * M_DV] = (og_ref[g, :, h * M_DV:(h + 1) * M_DV]
                                                * hn.astype(BF16))

        b_last = b_rep[L - 1:L, :]
        g_prev = b_last + m_prev
        gl = b_last - b_rep + i_rep
        m_new = jnp.maximum(g_prev, jnp.max(gl, axis=0, keepdims=True))
        wk = jnp.exp2(gl - m_new).astype(BF16)
        decay = jnp.exp2(g_prev - m_new)
        wv = jnp.concatenate([wk] * 3, axis=1) * v_aug_of(g, h)
        ct_ref[g, h] = (jnp.concatenate([decay] * 3, axis=1) * ct_ref[g, h]
                        + lax.dot_general(k_of(g, h), wv, _TN, preferred_element_type=F32))
        m_ref[g, h:h + 1, :] = m_new

    qk = [qk_product(c) for c in chains]
    mixed = [gate_and_mix(c, s) for c, s in zip(chains, qk)]
    for c, (nd, kept) in zip(chains, mixed):
        finish(c, nd, kept)


def _mlstm(qk, v, og, gc, hnw):
    b, seq, _ = qk.shape
    L = MLSTM_L
    G = MLSTM_G
    blk = lambda w: pl.BlockSpec((G, L, w), lambda bi, ci: (bi, ci, 0))
    return pl.pallas_call(
        functools.partial(_mlstm_kernel, L=L),
        grid=(b // G, seq // L),
        in_specs=[
            blk(2 * MA_QK), blk(MA_V), blk(MA_V), blk(128),
            _const_spec(hnw.shape),
        ],
        out_specs=blk(MA_V),
        out_shape=jax.ShapeDtypeStruct((b, seq, MA_V), BF16),
        scratch_shapes=[
            pltpu.VMEM((G, M_HEADS, M_DQK, M_DV + 128), F32),
            pltpu.VMEM((G, 8, 128), F32),
        ],
        compiler_params=pltpu.CompilerParams(
            dimension_semantics=("parallel", "arbitrary"), vmem_limit_bytes=VMEM_LIMIT),
        name="mlstm",
    )(qk, v, og, gc, hnw)


def _attn_kernel(qn_ref, qpe_ref, kn_ref, kpe_ref, v_ref, o_ref,
                 kcat_ref, vaug_ref, qcat_ref, m_ref, acc_ref, sa_ref, sb_ref, *, tq, tk, seq):
    assert tq == 2 * tk
    h = pl.program_id(1)
    n_tiles = seq // tq
    nt = tk // 128

    kcat_ref[:, 0:A_NOPE] = kn_ref[...]
    kcat_ref[:, A_NOPE:2 * A_NOPE] = kpe_ref[...]
    vaug_ref[:, 0:A_DV] = v_ref[...]
    vaug_ref[:, A_DV:2 * A_DV] = jnp.ones((seq, A_DV), BF16)
    lane = lax.broadcasted_iota(jnp.int32, (seq, 128), 1)
    own = ((lane // A_ROPE) == (h % 2)).astype(F32)
    qcat_ref[:, 0:A_NOPE] = qn_ref[...]
    qcat_ref[:, A_NOPE:2 * A_NOPE] = (qpe_ref[...].astype(F32) * own).astype(BF16)

    def logits(qt, kb, r0=0, nrows=tq):
        q0 = pl.multiple_of(qt * tq + r0, tk)
        k0 = pl.multiple_of(kb * tk, tk)
        return lax.dot_general(qcat_ref[pl.ds(q0, nrows), :], kcat_ref[pl.ds(k0, tk), :],
                               _NT, preferred_element_type=F32)

    def softmax_pv(s, kb, r0, nrows, masked):
        k0 = pl.multiple_of(kb * tk, tk)
        if masked:
            r = lax.broadcasted_iota(jnp.int32, (nrows, tk), 0)
            c = lax.broadcasted_iota(jnp.int32, (nrows, tk), 1)
            s = jnp.where(c <= r, s, NEG)
        tiles = [s[:, t * 128:(t + 1) * 128] for t in range(nt)]
        mx = functools.reduce(jnp.maximum, tiles)
        m_prev = m_ref[r0:r0 + nrows, :]
        m_new = jnp.maximum(m_prev, jnp.max(mx, axis=1, keepdims=True))
        alpha = jnp.exp2(m_prev - m_new)
        p = jnp.concatenate([jnp.exp2(t - m_new).astype(BF16) for t in tiles], axis=1)
        pv = _dot(p, vaug_ref[pl.ds(k0, tk), :])
        acc_ref[r0:r0 + nrows, :] = (jnp.concatenate([alpha, alpha], axis=1)
                                     * acc_ref[r0:r0 + nrows, :] + pv)
        m_ref[r0:r0 + nrows, :] = m_new

    sa_ref[...] = logits(0, 0)

    def tile_body(qt, carry):
        m_ref[...] = jnp.full_like(m_ref, NEG)
        acc_ref[...] = jnp.zeros_like(acc_ref)

        def pair_body(i, c):
            j = 2 * i
            sb_ref[...] = logits(qt, j + 1)
            softmax_pv(sa_ref[...], j, 0, tq, False)
            sa_ref[...] = logits(qt, j + 2)
            softmax_pv(sb_ref[...], j + 1, 0, tq, False)
            return c

        lax.fori_loop(0, qt, pair_body, 0)

        d0 = 2 * qt
        sb_ref[0:tk, :] = logits(qt, d0 + 1, tk, tk)
        softmax_pv(sa_ref[...], d0, 0, tq, True)
        sa_ref[...] = logits(jnp.minimum(qt + 1, n_tiles - 1), 0)
        softmax_pv(sb_ref[0:tk, :], d0 + 1, tk, tk, True)
        o0 = pl.multiple_of(qt * tq, tq)
        o_ref[pl.ds(o0, tq), :] = (acc_ref[:, 0:A_DV] / acc_ref[:, A_DV:2 * A_DV]).astype(BF16)
        return carry

    lax.fori_loop(0, n_tiles, tile_body, 0)


def _attn(qn, qpe, kn, kpe, v):
    b, seq, _ = qn.shape
    tq, tk = ATTN_TQ, ATTN_TK
    return pl.pallas_call(
        functools.partial(_attn_kernel, tq=tq, tk=tk, seq=seq),
        grid=(b, A_HEADS),
        in_specs=[
            pl.BlockSpec((None, seq, A_NOPE), lambda bi, h: (bi, 0, h)),
            pl.BlockSpec((None, seq, 128), lambda bi, h: (bi, 0, h // 2)),
            pl.BlockSpec((None, seq, A_NOPE), lambda bi, h: (bi, 0, h)),
            pl.BlockSpec((None, seq, 128), lambda bi, h: (bi, 0, 0)),
            pl.BlockSpec((None, seq, A_DV), lambda bi, h: (bi, 0, h)),
        ],
        out_specs=pl.BlockSpec((None, seq, A_DV), lambda bi, h: (bi, 0, h)),
        out_shape=jax.ShapeDtypeStruct((b, seq, A_HEADS * A_DV), BF16),
        scratch_shapes=[
            pltpu.VMEM((seq, 2 * A_NOPE), BF16),
            pltpu.VMEM((seq, 2 * A_DV), BF16),
            pltpu.VMEM((seq, 2 * A_NOPE), BF16),
            pltpu.VMEM((tq, 128), F32),
            pltpu.VMEM((tq, 2 * A_DV), F32),
            pltpu.VMEM((tq, tk), F32),
            pltpu.VMEM((tq, tk), F32),
        ],
        compiler_params=pltpu.CompilerParams(
            dimension_semantics=("parallel", "arbitrary"),
            vmem_limit_bytes=VMEM_LIMIT),
        name="attn",
    )(qn, qpe, kn, kpe, v)


def _mlp_kernel(x_ref, ya_ref, yb_ref, ga_ref, gb_ref, mod_ref,
                npost_ref, npre2_ref, npost2_ref, wout_ref, w1_ref, w2_ref,
                o_ref, acc_ref):
    gate_m = mod_ref[2:3, :]
    shift_f = mod_ref[3:4, :]
    scale_f = 1.0 + mod_ref[4:5, :]
    gate_f = mod_ref[5:6, :]

    y = (ga_ref[...].astype(F32) * ya_ref[...].astype(F32)
         + gb_ref[...].astype(F32) * yb_ref[...].astype(F32))
    yo = _dot(y.astype(BF16), wout_ref[...])
    x1 = x_ref[...] + gate_m * _rms(yo, npost_ref[...])

    h2 = (_rms(x1, npre2_ref[...]) * scale_f + shift_f).astype(BF16)
    for c in range(D_FF // FF_CHUNK):
        u = jnp.maximum(_dot(h2, w1_ref[:, c * FF_CHUNK:(c + 1) * FF_CHUNK]), 0.0)
        part = _dot((u * u).astype(BF16), w2_ref[c * FF_CHUNK:(c + 1) * FF_CHUNK, :])
        if c == 0:
            acc_ref[...] = part
        else:
            acc_ref[...] += part
    o_ref[...] = x1 + gate_f * _rms(acc_ref[...], npost2_ref[...])


def _mlp(x2, ya, yb, ga, gb, mod3, npost, npre2, npost2, wout, w1, w2, seq):
    t = x2.shape[0]
    tm = MLP_TM
    tiles_per_seq = seq // tm
    row = pl.BlockSpec((tm, D_MODEL), lambda i: (i, 0))
    return pl.pallas_call(
        _mlp_kernel,
        grid=(t // tm,),
        in_specs=[
            row, row, row, row, row,
            pl.BlockSpec((None, N_MOD, D_MODEL), lambda i: (i // tiles_per_seq, 0, 0)),
            _const_spec((1, D_MODEL)), _const_spec((1, D_MODEL)), _const_spec((1, D_MODEL)),
            _const_spec(wout.shape), _const_spec(w1.shape), _const_spec(w2.shape),
        ],
        out_specs=row,
        out_shape=jax.ShapeDtypeStruct((t, D_MODEL), F32),
        scratch_shapes=[pltpu.VMEM((tm, D_MODEL), F32)],
        compiler_params=pltpu.CompilerParams(
            dimension_semantics=("arbitrary",), vmem_limit_bytes=VMEM_LIMIT),
        name="mlp",
    )(x2, ya, yb, ga, gb, mod3, npost, npre2, npost2, wout, w1, w2)


def _rot_half_cols(w):
    half = w.shape[-1] // 2
    return jnp.concatenate([-w[..., half:], w[..., :half]], axis=-1)


def _regroup_kernel(w_ref, o_ref):
    sizes = (MA_QK, MA_QK, MA_V, MA_V, M_HEADS, M_HEADS, A_QRANK, A_KVRANK, A_ROPE,
             D_MODEL, D_MODEL)
    offs = [sum(sizes[:n]) for n in range(len(sizes))]
    o_q, _, _, _, o_i, _, o_cq, _, o_kpe, o_ga, _ = offs
    half = A_ROPE // 2
    cursor = [0]

    def put(src, n, negate=False):
        blk = w_ref[src:src + n, :]
        o_ref[cursor[0]:cursor[0] + n, :] = (-blk if negate else blk).astype(BF16)
        cursor[0] += n

    put(o_q, 2 * MA_QK + 2 * MA_V)
    put(o_ga, 2 * D_MODEL)
    put(o_cq, A_QRANK + A_KVRANK)
    put(o_kpe, A_ROPE)
    put(o_kpe, A_ROPE)
    for _ in range(2):
        put(o_kpe + half, half, negate=True)
        put(o_kpe, half)
    gate_rows = jnp.concatenate(
        [w_ref[o_i:o_i + 2 * M_HEADS, :],
         jnp.zeros((128 - 2 * M_HEADS, w_ref.shape[1]), F32)], axis=0)
    o_ref[cursor[0]:cursor[0] + 128, :] = gate_rows.astype(BF16)


def _prep_in_weights(w_in_t):
    n_in, k = w_in_t.shape
    n_out = 6 * D_MODEL
    return pl.pallas_call(
        _regroup_kernel,
        grid=(k // REGROUP_TK,),
        in_specs=[pl.BlockSpec((n_in, REGROUP_TK), lambda i: (0, i))],
        out_specs=pl.BlockSpec((n_out, REGROUP_TK), lambda i: (0, i)),
        out_shape=jax.ShapeDtypeStruct((n_out, k), BF16),
        compiler_params=pltpu.CompilerParams(
            dimension_semantics=("arbitrary",), vmem_limit_bytes=VMEM_LIMIT),
        name="regroup",
    )(w_in_t)


def _prep_mla_weights(w_uq, w_ukv):
    r = w_uq.reshape(A_QRANK, A_HEADS, A_NOPE + A_ROPE)
    nope = r[:, :, :A_NOPE].reshape(A_QRANK, A_HEADS * A_NOPE)
    pe = r[:, :, A_NOPE:]
    pe_rot = _rot_half_cols(pe).reshape(A_QRANK, A_HEADS * A_ROPE)
    pe = pe.reshape(A_QRANK, A_HEADS * A_ROPE)
    wuq = jnp.concatenate([nope, pe, pe_rot], axis=1).astype(BF16)
    r = w_ukv.reshape(A_KVRANK, A_HEADS, A_NOPE + A_DV)
    wukv = jnp.concatenate([r[:, :, :A_NOPE].reshape(A_KVRANK, -1),
                            r[:, :, A_NOPE:].reshape(A_KVRANK, -1)], axis=1).astype(BF16)
    return wuq, wukv


def _rope_tables(seq):
    half = A_ROPE // 2
    inv_freq = ROPE_THETA ** (-np.arange(half, dtype=np.float64) / half)
    ang = np.arange(seq, dtype=np.float64)[:, None] * inv_freq[None, :]
    reps = 128 // half
    return (jnp.asarray(np.tile(np.cos(ang), (1, reps)), F32),
            jnp.asarray(np.tile(np.sin(ang), (1, reps)), F32))


def kernel(x, c, w_ada, b_ada, norm_pre_mix, norm_post_mix, norm_pre_mlp, norm_post_mlp,
           w_in, mlstm_conv_w, mlstm_conv_b, mlstm_gate_b, mlstm_head_norm,
           mla_q_norm, mla_kv_norm, w_uq, w_ukv, w_out, w_ff1, w_ff2):
    bsz, seq, d = x.shape
    depth = w_ada.shape[0]
    cos, sin = _rope_tables(seq)
    c8 = jnp.pad(c, ((0, 8 - bsz), (0, 0)))
    qks = jnp.concatenate([jnp.ones((1, MA_QK), F32),
                           jnp.full((1, MA_QK), M_DQK ** -0.5, F32)], axis=1)
    row = lambda a: a.reshape(1, -1)
    x2 = x.reshape(bsz * seq, d)
    for l in range(depth):
        mod = _ada(c8, w_ada[l], row(b_ada[l]))[:bsz]
        mod3 = mod.reshape(bsz, N_MOD, d)
        win = _prep_in_weights(jnp.transpose(w_in[l]))
        wuq, wukv = _prep_mla_weights(w_uq[l], w_ukv[l])
        gbias = jnp.pad(row(mlstm_gate_b[l]), ((0, 0), (0, 128 - 2 * M_HEADS)))
        (qk, v_a, og, ga, gb, gates, qnope, qpe, knope, v_b, kpe) = _proj(
            x2, mod3, row(norm_pre_mix[l]), win, mlstm_conv_w[l], row(mlstm_conv_b[l]),
            qks, gbias, row(mla_q_norm[l]), row(mla_kv_norm[l]), wuq, wukv, cos, sin, seq)
        b3 = lambda a: a.reshape(bsz, seq, a.shape[-1])
        y_a = _mlstm(b3(qk), b3(v_a), b3(og), b3(gates), row(mlstm_head_norm[l]))
        y_b = _attn(b3(qnope), b3(qpe), b3(knope), b3(kpe), b3(v_b))
        x2 = _mlp(x2, y_a.reshape(bsz * seq, d), y_b.reshape(bsz * seq, d), ga, gb, mod3,
                  row(norm_post_mix[l]), row(norm_pre_mlp[l]), row(norm_post_mlp[l]),
                  w_out[l].astype(BF16), w_ff1[l].astype(BF16), w_ff2[l].astype(BF16), seq)
    return x2.reshape(bsz, seq, d)
```

```python
import functools

import jax
import jax.numpy as jnp
import numpy as np
from jax import lax
from jax.experimental import pallas as pl
from jax.experimental.pallas import tpu as pltpu

F32 = jnp.float32
BF16 = jnp.bfloat16

D_MODEL = 1024
M_HEADS = 4
M_DQK = 128
M_DV = 256
CONV_W = 4
A_HEADS = 8
A_NOPE = 128
A_ROPE = 64
A_DV = 128
A_QRANK = 384
A_KVRANK = 256
ROPE_THETA = 10000.0
D_FF = 4096
EPS = 1e-6
N_MOD = 6
MA_QK = M_HEADS * M_DQK
MA_V = M_HEADS * M_DV

PROJ_TM = 512
HALO = 16
MLSTM_L = 256
MLSTM_G = 4
ATTN_TQ = 1024
ATTN_TK = 512
MLP_TM = 512
FF_CHUNK = 1024
ADA_TN = 1536
REGROUP_TK = 256
VMEM_LIMIT = 56 * 1024 * 1024

NEG = -1e30
LOG2E = 1.4426950408889634

_NT = (((1,), (1,)), ((), ()))
_TN = (((0,), (0,)), ((), ()))


def _dot(a, b):
    return jnp.dot(a, b, preferred_element_type=F32)


def _sigmoid(x):
    return 1.0 / (1.0 + jnp.exp(-x))


def _rms(x, w):
    return x * lax.rsqrt(jnp.mean(x * x, axis=-1, keepdims=True) + EPS) * w


def _const_spec(shape):
    nd = len(shape)
    return pl.BlockSpec(shape, lambda *_: (0,) * nd, pipeline_mode=pl.Buffered(1))


def _ada_kernel(c_ref, w_ref, b_ref, o_ref):
    c = c_ref[...]
    a = (c * _sigmoid(c)).astype(BF16)
    o_ref[...] = _dot(a, w_ref[...].astype(BF16)) + b_ref[...]


def _ada(c8, w_ada, b_ada):
    n = w_ada.shape[1]
    return pl.pallas_call(
        _ada_kernel,
        grid=(n // ADA_TN,),
        in_specs=[
            pl.BlockSpec((8, D_MODEL), lambda j: (0, 0)),
            pl.BlockSpec((D_MODEL, ADA_TN), lambda j: (0, j)),
            pl.BlockSpec((1, ADA_TN), lambda j: (0, j)),
        ],
        out_specs=pl.BlockSpec((8, ADA_TN), lambda j: (0, j)),
        out_shape=jax.ShapeDtypeStruct((8, n), F32),
        compiler_params=pltpu.CompilerParams(
            dimension_semantics=("arbitrary",), vmem_limit_bytes=VMEM_LIMIT),
        name="ada",
    )(c8, w_ada, b_ada)


def _proj_kernel(x_ref, xh_ref, mod_ref, npre_ref,
                 win_ref,
                 convw_ref, convb_ref, qks_ref, gbias_ref, qn_ref, kvn_ref,
                 wuq_ref, wukv_ref, cos_ref, sin_ref,
                 qk_out, v_out, og_out, ga_out, gb_out, gates_out,
                 qnope_out, qpe_out, knope_out, vb_out, kpe_out,
                 hext_ref, z_ref, *, tm, tiles_per_seq, q_scale):
    i = pl.program_id(0)
    first = (i % tiles_per_seq) == 0
    wqk_ref, wv_ref, wo_ref, wga_ref, wgb_ref, ws_ref = (
        win_ref.at[n * D_MODEL:(n + 1) * D_MODEL, :] for n in range(6))

    def proj(a, wt_ref):
        return lax.dot_general(a, wt_ref[...], _NT, preferred_element_type=F32)

    shift = mod_ref[0:1, :]
    scale1 = 1.0 + mod_ref[1:2, :]
    w = npre_ref[...]

    def prenorm(xv):
        return _rms(xv, w) * scale1 + shift

    hext_ref[HALO:, :] = prenorm(x_ref[...]).astype(BF16)
    hh = prenorm(xh_ref[...])
    hext_ref[0:HALO, :] = jnp.where(first, 0.0, hh).astype(BF16)

    z_ref[...] = proj(hext_ref[...], wqk_ref)
    acc = convb_ref[...]
    for j in range(CONV_W):
        off = HALO - (CONV_W - 1) + j
        acc = acc + convw_ref[j:j + 1, :] * z_ref[off:off + tm, :]
    qk_out[...] = (acc * _sigmoid(acc) * qks_ref[...]).astype(BF16)

    h = hext_ref[HALO:, :]
    v_out[...] = proj(h, wv_ref).astype(BF16)
    og_out[...] = _sigmoid(proj(h, wo_ref)).astype(BF16)
    ga_out[...] = _sigmoid(proj(h, wga_ref)).astype(BF16)
    gb_out[...] = _sigmoid(proj(h, wgb_ref)).astype(BF16)

    s = proj(h, ws_ref)
    c_q = s[:, 0:A_QRANK]
    c_kv = s[:, A_QRANK:A_QRANK + A_KVRANK]
    kp = s[:, 640:768]
    kpr = s[:, 768:896]
    g = s[:, 896:1024] + gbias_ref[...]

    lane = lax.broadcasted_iota(jnp.int32, g.shape, 1)
    logsig = jnp.minimum(g, 0.0) - jnp.log1p(jnp.exp(-jnp.abs(g)))
    gates_out[...] = jnp.where(lane < M_HEADS, g, logsig)

    cos = cos_ref[...]
    sin = sin_ref[...]
    kpe_out[...] = (kp * cos + kpr * sin).astype(BF16)

    q = _dot(_rms(c_q, qn_ref[...]).astype(BF16), wuq_ref[...])
    qnope_out[...] = (q[:, 0:1024] * q_scale).astype(BF16)
    cos4 = jnp.concatenate([cos] * 4, axis=1)
    sin4 = jnp.concatenate([sin] * 4, axis=1)
    qpe = q[:, 1024:1536] * cos4 + q[:, 1536:2048] * sin4
    qpe_out[...] = (qpe * q_scale).astype(BF16)

    kv = _dot(_rms(c_kv, kvn_ref[...]).astype(BF16), wukv_ref[...])
    knope_out[...] = kv[:, 0:1024].astype(BF16)
    vb_out[...] = kv[:, 1024:2048].astype(BF16)


def _proj(x2, mod3, npre, win, convw, convb, qks, gbias, qn, kvn, wuq, wukv,
          cos, sin, seq):
    t = x2.shape[0]
    tm = PROJ_TM
    tiles_per_seq = seq // tm
    hb = tm // HALO
    row = lambda i: (i, 0)
    out_w = [1024, 1024, 1024, 1024, 1024, 128, 1024, 512, 1024, 1024, 128]
    out_dt = [BF16] * 5 + [F32] + [BF16] * 5
    kern = functools.partial(
        _proj_kernel, tm=tm, tiles_per_seq=tiles_per_seq,
        q_scale=float((A_NOPE + A_ROPE) ** -0.5 * LOG2E))
    return pl.pallas_call(
        kern,
        grid=(t // tm,),
        in_specs=[
            pl.BlockSpec((tm, D_MODEL), row),
            pl.BlockSpec((HALO, D_MODEL), lambda i: (jnp.maximum(i * hb - 1, 0), 0)),
            pl.BlockSpec((None, N_MOD, D_MODEL), lambda i: (i // tiles_per_seq, 0, 0)),
            _const_spec((1, D_MODEL)),
            _const_spec(win.shape),
            _const_spec(convw.shape), _const_spec(convb.shape), _const_spec(qks.shape),
            _const_spec(gbias.shape), _const_spec(qn.shape), _const_spec(kvn.shape),
            _const_spec(wuq.shape), _const_spec(wukv.shape),
            pl.BlockSpec((tm, 128), lambda i: (i % tiles_per_seq, 0)),
            pl.BlockSpec((tm, 128), lambda i: (i % tiles_per_seq, 0)),
        ],
        out_specs=[pl.BlockSpec((tm, n), row) for n in out_w],
        out_shape=[jax.ShapeDtypeStruct((t, n), d) for n, d in zip(out_w, out_dt)],
        scratch_shapes=[
            pltpu.VMEM((tm + HALO, D_MODEL), BF16),
            pltpu.VMEM((tm + HALO, 2 * MA_QK), F32),
        ],
        compiler_params=pltpu.CompilerParams(
            dimension_semantics=("arbitrary",), vmem_limit_bytes=VMEM_LIMIT),
        name="proj",
    )(x2, x2, mod3, npre, win, convw, convb, qks, gbias,
      qn, kvn, wuq, wukv, cos, sin)


def _split3(x):
    hi = x.astype(BF16)
    r = x - hi.astype(F32)
    mid = r.astype(BF16)
    lo = (r - mid.astype(F32)).astype(BF16)
    return hi, mid, lo


def _mlstm_kernel(qk_ref, v_ref, og_ref, gc_ref, hnw_ref, y_ref,
                  ct_ref, m_ref, *, L):
    @pl.when(pl.program_id(1) == 0)
    def _():
        ct_ref[...] = jnp.zeros_like(ct_ref)
        m_ref[...] = jnp.zeros_like(m_ref)

    n_seq = qk_ref.shape[0]
    nlt = L // 128
    row = lax.broadcasted_iota(jnp.int32, (L, L), 0)
    col = lax.broadcasted_iota(jnp.int32, (L, L), 1)
    causal = col <= row
    tri = causal.astype(BF16)
    ones = jnp.ones((L, 128), BF16)
    chains = [(g, h) for g in range(n_seq) for h in range(M_HEADS)]

    def q_of(g, h):
        return qk_ref[g, :, h * M_DQK:(h + 1) * M_DQK]

    def k_of(g, h):
        return qk_ref[g, :, MA_QK + h * M_DQK:MA_QK + (h + 1) * M_DQK]

    def v_aug_of(g, h):
        return jnp.concatenate([v_ref[g, :, h * M_DV:(h + 1) * M_DV], ones], axis=1)

    gates = []
    for g in range(n_seq):
        gc = gc_ref[g] * LOG2E
        gr = gc.T[0:2 * M_HEADS, :]
        bcol_all = sum(_dot(tri, p) for p in _split3(gc))
        brow_all = sum(lax.dot_general(p, tri, _NT, preferred_element_type=F32)
                       for p in _split3(gr))
        gates.append((gc, gr, bcol_all, brow_all))

    def qk_product(c):
        return lax.dot_general(q_of(*c), k_of(*c), _NT, preferred_element_type=F32)

    def gate_and_mix(c, qk):
        g, h = c
        gc, gr, bcol_all, brow_all = gates[g]
        c_row = gr[h:h + 1, :] - brow_all[M_HEADS + h:M_HEADS + h + 1, :]
        i_rep = jnp.broadcast_to(gc[:, h:h + 1], (L, 128))
        b_rep = jnp.broadcast_to(bcol_all[:, M_HEADS + h:M_HEADS + h + 1], (L, 128))
        m_prev = m_ref[g, h:h + 1, :]
        d_tiles = [jnp.where(causal[:, t * 128:(t + 1) * 128],
                             b_rep + c_row[:, t * 128:(t + 1) * 128], NEG) for t in range(nlt)]
        a = b_rep + m_prev
        m_intra = jnp.max(functools.reduce(jnp.maximum, d_tiles), axis=1, keepdims=True)
        m_out = jnp.maximum(a, m_intra)
        s_tiles = [(qk[:, t * 128:(t + 1) * 128]
                    * jnp.exp2(d_tiles[t] - m_out)).astype(BF16) for t in range(nlt)]
        w_inter = jnp.exp2(a - m_out).astype(BF16)
        ct = ct_ref[g, h]
        lhs = jnp.concatenate(s_tiles + [w_inter * q_of(g, h)], axis=1)
        rhs = jnp.concatenate([v_aug_of(g, h), ct.astype(BF16)], axis=0)
        return _dot(lhs, rhs), (m_out, b_rep, i_rep, m_prev)

    def finish(c, nd, kept):
        g, h = c
        m_out, b_rep, i_rep, m_prev = kept
        num = nd[:, 0:M_DV]
        den = nd[:, M_DV:M_DV + 128]
        inv = 1.0 / jnp.maximum(jnp.abs(den), jnp.exp2(-m_out))
        ms = jnp.mean(num * num, axis=1, keepdims=True)
        f = inv * lax.rsqrt(inv * inv * ms + EPS)
        hn = num * jnp.concatenate([f, f], axis=1) * hnw_ref[:, h * M_DV:(h + 1) * M_DV]
        y_ref[g, :, h * M_DV:(h + 1) * M_DV] = (og_ref[g, :, h * M_DV:(h + 1) * M_DV]
                                                * hn.astype(BF16))

        b_last = b_rep[L - 1:L, :]
        g_prev = b_last + m_prev
        gl = b_last - b_rep + i_rep
        m_new = jnp.maximum(g_prev, jnp.max(gl, axis=0, keepdims=True))
        wk = jnp.exp2(gl - m_new).astype(BF16)
        decay = jnp.exp2(g_prev - m_new)
        wv = jnp.concatenate([wk] * 3, axis=1) * v_aug_of(g, h)
        ct_ref[g, h] = (jnp.concatenate([decay] * 3, axis=1) * ct_ref[g, h]
                        + lax.dot_general(k_of(g, h), wv, _TN, preferred_element_type=F32))
        m_ref[g, h:h + 1, :] = m_new

    qk = [qk_product(c) for c in chains]
    mixed = [gate_and_mix(c, s) for c, s in zip(chains, qk)]
    for c, (nd, kept) in zip(chains, mixed):
        finish(c, nd, kept)


def _mlstm(qk, v, og, gc, hnw):
    b, seq, _ = qk.shape
    L = MLSTM_L
    G = MLSTM_G
    blk = lambda w: pl.BlockSpec((G, L, w), lambda bi, ci: (bi, ci, 0))
    return pl.pallas_call(
        functools.partial(_mlstm_kernel, L=L),
        grid=(b // G, seq // L),
        in_specs=[
            blk(2 * MA_QK), blk(MA_V), blk(MA_V), blk(128),
            _const_spec(hnw.shape),
        ],
        out_specs=blk(MA_V),
        out_shape=jax.ShapeDtypeStruct((b, seq, MA_V), BF16),
        scratch_shapes=[
            pltpu.VMEM((G, M_HEADS, M_DQK, M_DV + 128), F32),
            pltpu.VMEM((G, 8, 128), F32),
        ],
        compiler_params=pltpu.CompilerParams(
            dimension_semantics=("parallel", "arbitrary"), vmem_limit_bytes=VMEM_LIMIT),
        name="mlstm",
    )(qk, v, og, gc, hnw)


def _attn_kernel(qn_ref, qpe_ref, kn_ref, kpe_ref, v_ref, o_ref,
                 kcat_ref, vaug_ref, qcat_ref, m_ref, acc_ref, sa_ref, sb_ref, *, tq, tk, seq):
    assert tq == 2 * tk
    h = pl.program_id(1)
    n_tiles = seq // tq
    nt = tk // 128

    @pl.when(h == 0)
    def _():
        kcat_ref[:, A_NOPE:2 * A_NOPE] = kpe_ref[...]
        vaug_ref[:, A_DV:2 * A_DV] = jnp.ones((seq, A_DV), BF16)

    kcat_ref[:, 0:A_NOPE] = kn_ref[...]
    vaug_ref[:, 0:A_DV] = v_ref[...]
    lane = lax.broadcasted_iota(jnp.int32, (seq, 128), 1)
    own = ((lane // A_ROPE) == (h % 2)).astype(BF16)
    qcat_ref[:, 0:A_NOPE] = qn_ref[...]
    qcat_ref[:, A_NOPE:2 * A_NOPE] = qpe_ref[...] * own

    def logits(qt, kb, r0=0, nrows=tq):
        q0 = pl.multiple_of(qt * tq + r0, tk)
        k0 = pl.multiple_of(kb * tk, tk)
        return lax.dot_general(qcat_ref[pl.ds(q0, nrows), :], kcat_ref[pl.ds(k0, tk), :],
                               _NT, preferred_element_type=F32)

    def softmax_pv(s, kb, r0, nrows, masked):
        k0 = pl.multiple_of(kb * tk, tk)
        if masked:
            r = lax.broadcasted_iota(jnp.int32, (nrows, tk), 0)
            c = lax.broadcasted_iota(jnp.int32, (nrows, tk), 1)
            s = jnp.where(c <= r, s, NEG)
        tiles = [s[:, t * 128:(t + 1) * 128] for t in range(nt)]
        mx = functools.reduce(jnp.maximum, tiles)
        m_prev = m_ref[r0:r0 + nrows, :]
        m_new = jnp.maximum(m_prev, jnp.max(mx, axis=1, keepdims=True))
        alpha = jnp.exp2(m_prev - m_new)
        p = jnp.concatenate([jnp.exp2(t - m_new).astype(BF16) for t in tiles], axis=1)
        pv = _dot(p, vaug_ref[pl.ds(k0, tk), :])
        acc_ref[r0:r0 + nrows, :] = (jnp.concatenate([alpha, alpha], axis=1)
                                     * acc_ref[r0:r0 + nrows, :] + pv)
        m_ref[r0:r0 + nrows, :] = m_new

    sa_ref[...] = logits(0, 0)

    def tile_body(qt, carry):
        m_ref[...] = jnp.full_like(m_ref, NEG)
        acc_ref[...] = jnp.zeros_like(acc_ref)

        def pair_body(i, c):
            j = 2 * i
            sb_ref[...] = logits(qt, j + 1)
            softmax_pv(sa_ref[...], j, 0, tq, False)
            sa_ref[...] = logits(qt, j + 2)
            softmax_pv(sb_ref[...], j + 1, 0, tq, False)
            return c

        lax.fori_loop(0, qt, pair_body, 0)

        d0 = 2 * qt
        sb_ref[0:tk, :] = logits(qt, d0 + 1, tk, tk)
        softmax_pv(sa_ref[...], d0, 0, tq, True)
        sa_ref[...] = logits(jnp.minimum(qt + 1, n_tiles - 1), 0)
        softmax_pv(sb_ref[0:tk, :], d0 + 1, tk, tk, True)
        o0 = pl.multiple_of(qt * tq, tq)
        o_ref[pl.ds(o0, tq), :] = (acc_ref[:, 0:A_DV] / acc_ref[:, A_DV:2 * A_DV]).astype(BF16)
        return carry

    lax.fori_loop(0, n_tiles, tile_body, 0)


def _attn(qn, qpe, kn, kpe, v):
    b, seq, _ = qn.shape
    tq, tk = ATTN_TQ, ATTN_TK
    return pl.pallas_call(
        functools.partial(_attn_kernel, tq=tq, tk=tk, seq=seq),
        grid=(b, A_HEADS),
        in_specs=[
            pl.BlockSpec((None, seq, A_NOPE), lambda bi, h: (bi, 0, h)),
            pl.BlockSpec((None, seq, 128), lambda bi, h: (bi, 0, h // 2)),
            pl.BlockSpec((None, seq, A_NOPE), lambda bi, h: (bi, 0, h)),
            pl.BlockSpec((None, seq, 128), lambda bi, h: (bi, 0, 0)),
            pl.BlockSpec((None, seq, A_DV), lambda bi, h: (bi, 0, h)),
        ],
        out_specs=pl.BlockSpec((None, seq, A_DV), lambda bi, h: (bi, 0, h)),
        out_shape=jax.ShapeDtypeStruct((b, seq, A_HEADS * A_DV), BF16),
        scratch_shapes=[
            pltpu.VMEM((seq, 2 * A_NOPE), BF16),
            pltpu.VMEM((seq, 2 * A_DV), BF16),
            pltpu.VMEM((seq, 2 * A_NOPE), BF16),
            pltpu.VMEM((tq, 128), F32),
            pltpu.VMEM((tq, 2 * A_DV), F32),
            pltpu.VMEM((tq, tk), F32),
            pltpu.VMEM((tq, tk), F32),
        ],
        compiler_params=pltpu.CompilerParams(
            dimension_semantics=("parallel", "arbitrary"),
            vmem_limit_bytes=VMEM_LIMIT),
        name="attn",
    )(qn, qpe, kn, kpe, v)


def _mlp_kernel(x_ref, ya_ref, yb_ref, ga_ref, gb_ref, mod_ref,
                npost_ref, npre2_ref, npost2_ref, wout_ref, w1_ref, w2_ref,
                o_ref, acc_ref):
    gate_m = mod_ref[2:3, :]
    shift_f = mod_ref[3:4, :]
    scale_f = 1.0 + mod_ref[4:5, :]
    gate_f = mod_ref[5:6, :]

    y = (ga_ref[...].astype(F32) * ya_ref[...].astype(F32)
         + gb_ref[...].astype(F32) * yb_ref[...].astype(F32))
    yo = _dot(y.astype(BF16), wout_ref[...])
    x1 = x_ref[...] + gate_m * _rms(yo, npost_ref[...])

    h2 = (_rms(x1, npre2_ref[...]) * scale_f + shift_f).astype(BF16)
    for c in range(D_FF // FF_CHUNK):
        u = jnp.maximum(_dot(h2, w1_ref[:, c * FF_CHUNK:(c + 1) * FF_CHUNK]), 0.0)
        part = _dot((u * u).astype(BF16), w2_ref[c * FF_CHUNK:(c + 1) * FF_CHUNK, :])
        if c == 0:
            acc_ref[...] = part
        else:
            acc_ref[...] += part
    o_ref[...] = x1 + gate_f * _rms(acc_ref[...], npost2_ref[...])


def _mlp(x2, ya, yb, ga, gb, mod3, npost, npre2, npost2, wout, w1, w2, seq):
    t = x2.shape[0]
    tm = MLP_TM
    tiles_per_seq = seq // tm
    row = pl.BlockSpec((tm, D_MODEL), lambda i: (i, 0))
    return pl.pallas_call(
        _mlp_kernel,
        grid=(t // tm,),
        in_specs=[
            row, row, row, row, row,
            pl.BlockSpec((None, N_MOD, D_MODEL), lambda i: (i // tiles_per_seq, 0, 0)),
            _const_spec((1, D_MODEL)), _const_spec((1, D_MODEL)), _const_spec((1, D_MODEL)),
            _const_spec(wout.shape), _const_spec(w1.shape), _const_spec(w2.shape),
        ],
        out_specs=row,
        out_shape=jax.ShapeDtypeStruct((t, D_MODEL), F32),
        scratch_shapes=[pltpu.VMEM((tm, D_MODEL), F32)],
        compiler_params=pltpu.CompilerParams(
            dimension_semantics=("arbitrary",), vmem_limit_bytes=VMEM_LIMIT),
        name="mlp",
    )(x2, ya, yb, ga, gb, mod3, npost, npre2, npost2, wout, w1, w2)


def _rot_half_cols(w):
    half = w.shape[-1] // 2
    return jnp.concatenate([-w[..., half:], w[..., :half]], axis=-1)


def _regroup_kernel(w_ref, o_ref):
    sizes = (MA_QK, MA_QK, MA_V, MA_V, M_HEADS, M_HEADS, A_QRANK, A_KVRANK, A_ROPE,
             D_MODEL, D_MODEL)
    offs = [sum(sizes[:n]) for n in range(len(sizes))]
    o_q, _, _, _, o_i, _, o_cq, _, o_kpe, o_ga, _ = offs
    half = A_ROPE // 2
    cursor = [0]

    def put(src, n, negate=False):
        blk = w_ref[src:src + n, :]
        o_ref[cursor[0]:cursor[0] + n, :] = (-blk if negate else blk).astype(BF16)
        cursor[0] += n

    put(o_q, 2 * MA_QK + 2 * MA_V)
    put(o_ga, 2 * D_MODEL)
    put(o_cq, A_QRANK + A_KVRANK)
    put(o_kpe, A_ROPE)
    put(o_kpe, A_ROPE)
    for _ in range(2):
        put(o_kpe + half, half, negate=True)
        put(o_kpe, half)
    gate_rows = jnp.concatenate(
        [w_ref[o_i:o_i + 2 * M_HEADS, :],
         jnp.zeros((128 - 2 * M_HEADS, w_ref.shape[1]), F32)], axis=0)
    o_ref[cursor[0]:cursor[0] + 128, :] = gate_rows.astype(BF16)


def _prep_in_weights(w_in_t):
    n_in, k = w_in_t.shape
    n_out = 6 * D_MODEL
    return pl.pallas_call(
        _regroup_kernel,
        grid=(k // REGROUP_TK,),
        in_specs=[pl.BlockSpec((n_in, REGROUP_TK), lambda i: (0, i))],
        out_specs=pl.BlockSpec((n_out, REGROUP_TK), lambda i: (0, i)),
        out_shape=jax.ShapeDtypeStruct((n_out, k), BF16),
        compiler_params=pltpu.CompilerParams(
            dimension_semantics=("arbitrary",), vmem_limit_bytes=VMEM_LIMIT),
        name="regroup",
    )(w_in_t)


def _prep_mla_weights(w_uq, w_ukv):
    r = w_uq.reshape(A_QRANK, A_HEADS, A_NOPE + A_ROPE)
    nope = r[:, :, :A_NOPE].reshape(A_QRANK, A_HEADS * A_NOPE)
    pe = r[:, :, A_NOPE:]
    pe_rot = _rot_half_cols(pe).reshape(A_QRANK, A_HEADS * A_ROPE)
    pe = pe.reshape(A_QRANK, A_HEADS * A_ROPE)
    wuq = jnp.concatenate([nope, pe, pe_rot], axis=1).astype(BF16)
    r = w_ukv.reshape(A_KVRANK, A_HEADS, A_NOPE + A_DV)
    wukv = jnp.concatenate([r[:, :, :A_NOPE].reshape(A_KVRANK, -1),
                            r[:, :, A_NOPE:].reshape(A_KVRANK, -1)], axis=1).astype(BF16)
    return wuq, wukv


def _rope_tables(seq):
    half = A_ROPE // 2
    inv_freq = ROPE_THETA ** (-np.arange(half, dtype=np.float64) / half)
    ang = np.arange(seq, dtype=np.float64)[:, None] * inv_freq[None, :]
    reps = 128 // half
    return (jnp.asarray(np.tile(np.cos(ang), (1, reps)), F32),
            jnp.asarray(np.tile(np.sin(ang), (1, reps)), F32))


def kernel(x, c, w_ada, b_ada, norm_pre_mix, norm_post_mix, norm_pre_mlp, norm_post_mlp,
           w_in, mlstm_conv_w, mlstm_conv_b, mlstm_gate_b, mlstm_head_norm,
           mla_q_norm, mla_kv_norm, w_uq, w_ukv, w_out, w_ff1, w_ff2):
    bsz, seq, d = x.shape
    depth = w_ada.shape[0]
    cos, sin = _rope_tables(seq)
    c8 = jnp.pad(c, ((0, 8 - bsz), (0, 0)))
    qks = jnp.concatenate([jnp.ones((1, MA_QK), F32),
                           jnp.full((1, MA_QK), M_DQK ** -0.5, F32)], axis=1)
    row = lambda a: a.reshape(1, -1)
    x2 = x.reshape(bsz * seq, d)
    for l in range(depth):
        mod = _ada(c8, w_ada[l], row(b_ada[l]))[:bsz]
        mod3 = mod.reshape(bsz, N_MOD, d)
        win = _prep_in_weights(jnp.transpose(w_in[l]))
        wuq, wukv = _prep_mla_weights(w_uq[l], w_ukv[l])
        gbias = jnp.pad(row(mlstm_gate_b[l]), ((0, 0), (0, 128 - 2 * M_HEADS)))
        (qk, v_a, og, ga, gb, gates, qnope, qpe, knope, v_b, kpe) = _proj(
            x2, mod3, row(norm_pre_mix[l]), win, mlstm_conv_w[l], row(mlstm_conv_b[l]),
            qks, gbias, row(mla_q_norm[l]), row(mla_kv_norm[l]), wuq, wukv, cos, sin, seq)
        b3 = lambda a: a.reshape(bsz, seq, a.shape[-1])
        y_a = _mlstm(b3(qk), b3(v_a), b3(og), b3(gates), row(mlstm_head_norm[l]))
        y_b = _attn(b3(qnope), b3(qpe), b3(knope), b3(kpe), b3(v_b))
        x2 = _mlp(x2, y_a.reshape(bsz * seq, d), y_b.reshape(bsz * seq, d), ga, gb, mod3,
                  row(norm_post_mix[l]), row(norm_pre_mlp[l]), row(norm_post_mlp[l]),
                  w_out[l].astype(BF16), w_ff1[l].astype(BF16), w_ff2[l].astype(BF16), seq)
    return x2.reshape(bsz, seq, d)
```

```python
import functools

import jax
import jax.numpy as jnp
import numpy as np
from jax import lax
from jax.experimental import pallas as pl
from jax.experimental.pallas import tpu as pltpu

F32 = jnp.float32
BF16 = jnp.bfloat16

D_MODEL = 1024
M_HEADS = 4
M_DQK = 128
M_DV = 256
CONV_W = 4
A_HEADS = 8
A_NOPE = 128
A_ROPE = 64
A_DV = 128
A_QRANK = 384
A_KVRANK = 256
ROPE_THETA = 10000.0
D_FF = 4096
EPS = 1e-6
N_MOD = 6
MA_QK = M_HEADS * M_DQK
MA_V = M_HEADS * M_DV

PROJ_TM = 512
HALO = 16
MLSTM_L = 256
MLSTM_G = 4
ATTN_TQ = 1024
ATTN_TK = 512
MLP_TM = 512
FF_CHUNK = 1024
ADA_TN = 1536
REGROUP_TK = 256
VMEM_LIMIT = 56 * 1024 * 1024

NEG = -1e30
LOG2E = 1.4426950408889634

_NT = (((1,), (1,)), ((), ()))
_TN = (((0,), (0,)), ((), ()))


def _dot(a, b):
    return jnp.dot(a, b, preferred_element_type=F32)


def _sigmoid(x):
    return 1.0 / (1.0 + jnp.exp(-x))


def _rms(x, w):
    return x * lax.rsqrt(jnp.mean(x * x, axis=-1, keepdims=True) + EPS) * w


def _const_spec(shape):
    nd = len(shape)
    return pl.BlockSpec(shape, lambda *_: (0,) * nd, pipeline_mode=pl.Buffered(1))


def _ada_kernel(c_ref, w_ref, b_ref, o_ref):
    c = c_ref[...]
    a = (c * _sigmoid(c)).astype(BF16)
    o_ref[...] = _dot(a, w_ref[...].astype(BF16)) + b_ref[...]


def _ada(c8, w_ada, b_ada):
    n = w_ada.shape[1]
    return pl.pallas_call(
        _ada_kernel,
        grid=(n // ADA_TN,),
        in_specs=[
            pl.BlockSpec((8, D_MODEL), lambda j: (0, 0)),
            pl.BlockSpec((D_MODEL, ADA_TN), lambda j: (0, j)),
            pl.BlockSpec((1, ADA_TN), lambda j: (0, j)),
        ],
        out_specs=pl.BlockSpec((8, ADA_TN), lambda j: (0, j)),
        out_shape=jax.ShapeDtypeStruct((8, n), F32),
        compiler_params=pltpu.CompilerParams(
            dimension_semantics=("arbitrary",), vmem_limit_bytes=VMEM_LIMIT),
        name="ada",
    )(c8, w_ada, b_ada)


def _proj_kernel(x_ref, xh_ref, mod_ref, npre_ref,
                 win_ref,
                 convw_ref, convb_ref, qks_ref, gbias_ref, qn_ref, kvn_ref,
                 wuq_ref, wukv_ref, cos_ref, sin_ref,
                 qk_out, v_out, og_out, ga_out, gb_out, gates_out,
                 qnope_out, qpe_out, knope_out, vb_out, kpe_out,
                 hext_ref, z_ref, *, tm, tiles_per_seq, q_scale):
    i = pl.program_id(0)
    first = (i % tiles_per_seq) == 0
    wqk_ref, wv_ref, wo_ref, wga_ref, wgb_ref, ws_ref = (
        win_ref.at[n * D_MODEL:(n + 1) * D_MODEL, :] for n in range(6))

    def proj(a, wt_ref):
        return lax.dot_general(a, wt_ref[...], _NT, preferred_element_type=F32)

    shift = mod_ref[0:1, :]
    w_scaled = npre_ref[...] * (1.0 + mod_ref[1:2, :])

    def prenorm(xv):
        return _rms(xv, w_scaled) + shift

    hext_ref[HALO:, :] = prenorm(x_ref[...]).astype(BF16)
    hh = prenorm(xh_ref[...])
    hext_ref[0:HALO, :] = jnp.where(first, 0.0, hh).astype(BF16)

    z_ref[...] = proj(hext_ref[...], wqk_ref)
    acc = convb_ref[...]
    for j in range(CONV_W):
        off = HALO - (CONV_W - 1) + j
        acc = acc + convw_ref[j:j + 1, :] * z_ref[off:off + tm, :]
    qk_out[...] = (acc * _sigmoid(acc) * qks_ref[...]).astype(BF16)

    h = hext_ref[HALO:, :]
    v_out[...] = proj(h, wv_ref).astype(BF16)
    og_out[...] = _sigmoid(proj(h, wo_ref)).astype(BF16)
    ga_out[...] = _sigmoid(proj(h, wga_ref)).astype(BF16)
    gb_out[...] = _sigmoid(proj(h, wgb_ref)).astype(BF16)

    s = proj(h, ws_ref)
    c_q = s[:, 0:A_QRANK]
    c_kv = s[:, A_QRANK:A_QRANK + A_KVRANK]
    kp = s[:, 640:768]
    kpr = s[:, 768:896]
    g = s[:, 896:1024] + gbias_ref[...]

    lane = lax.broadcasted_iota(jnp.int32, g.shape, 1)
    logsig = jnp.minimum(g, 0.0) - jnp.log1p(jnp.exp(-jnp.abs(g)))
    gates_out[...] = jnp.where(lane < M_HEADS, g, logsig)

    cos = cos_ref[...]
    sin = sin_ref[...]
    kpe_out[...] = (kp * cos + kpr * sin).astype(BF16)

    q = _dot(_rms(c_q, qn_ref[...]).astype(BF16), wuq_ref[...])
    qnope_out[...] = (q[:, 0:1024] * q_scale).astype(BF16)
    cos4 = jnp.concatenate([cos] * 4, axis=1)
    sin4 = jnp.concatenate([sin] * 4, axis=1)
    qpe = q[:, 1024:1536] * cos4 + q[:, 1536:2048] * sin4
    qpe_out[...] = (qpe * q_scale).astype(BF16)

    kv = _dot(_rms(c_kv, kvn_ref[...]).astype(BF16), wukv_ref[...])
    knope_out[...] = kv[:, 0:1024].astype(BF16)
    vb_out[...] = kv[:, 1024:2048].astype(BF16)


def _proj(x2, mod3, npre, win, convw, convb, qks, gbias, qn, kvn, wuq, wukv,
          cos, sin, seq):
    t = x2.shape[0]
    tm = PROJ_TM
    tiles_per_seq = seq // tm
    hb = tm // HALO
    row = lambda i: (i, 0)
    out_w = [1024, 1024, 1024, 1024, 1024, 128, 1024, 512, 1024, 1024, 128]
    out_dt = [BF16] * 5 + [F32] + [BF16] * 5
    kern = functools.partial(
        _proj_kernel, tm=tm, tiles_per_seq=tiles_per_seq,
        q_scale=float((A_NOPE + A_ROPE) ** -0.5 * LOG2E))
    return pl.pallas_call(
        kern,
        grid=(t // tm,),
        in_specs=[
            pl.BlockSpec((tm, D_MODEL), row),
            pl.BlockSpec((HALO, D_MODEL), lambda i: (jnp.maximum(i * hb - 1, 0), 0)),
            pl.BlockSpec((None, N_MOD, D_MODEL), lambda i: (i // tiles_per_seq, 0, 0)),
            _const_spec((1, D_MODEL)),
            _const_spec(win.shape),
            _const_spec(convw.shape), _const_spec(convb.shape), _const_spec(qks.shape),
            _const_spec(gbias.shape), _const_spec(qn.shape), _const_spec(kvn.shape),
            _const_spec(wuq.shape), _const_spec(wukv.shape),
            pl.BlockSpec((tm, 128), lambda i: (i % tiles_per_seq, 0)),
            pl.BlockSpec((tm, 128), lambda i: (i % tiles_per_seq, 0)),
        ],
        out_specs=[pl.BlockSpec((tm, n), row) for n in out_w],
        out_shape=[jax.ShapeDtypeStruct((t, n), d) for n, d in zip(out_w, out_dt)],
        scratch_shapes=[
            pltpu.VMEM((tm + HALO, D_MODEL), BF16),
            pltpu.VMEM((tm + HALO, 2 * MA_QK), F32),
        ],
        compiler_params=pltpu.CompilerParams(
            dimension_semantics=("arbitrary",), vmem_limit_bytes=VMEM_LIMIT),
        name="proj",
    )(x2, x2, mod3, npre, win, convw, convb, qks, gbias,
      qn, kvn, wuq, wukv, cos, sin)


def _split3(x):
    hi = x.astype(BF16)
    r = x - hi.astype(F32)
    mid = r.astype(BF16)
    lo = (r - mid.astype(F32)).astype(BF16)
    return hi, mid, lo


def _mlstm_kernel(qk_ref, v_ref, og_ref, gc_ref, hnw_ref, y_ref,
                  ct_ref, m_ref, *, L):
    @pl.when(pl.program_id(1) == 0)
    def _():
        ct_ref[...] = jnp.zeros_like(ct_ref)
        m_ref[...] = jnp.zeros_like(m_ref)

    n_seq = qk_ref.shape[0]
    nlt = L // 128
    row = lax.broadcasted_iota(jnp.int32, (L, L), 0)
    col = lax.broadcasted_iota(jnp.int32, (L, L), 1)
    causal = col <= row
    tri = causal.astype(BF16)
    ones = jnp.ones((L, 128), BF16)
    chains = [(g, h) for g in range(n_seq) for h in range(M_HEADS)]

    def q_of(g, h):
        return qk_ref[g, :, h * M_DQK:(h + 1) * M_DQK]

    def k_of(g, h):
        return qk_ref[g, :, MA_QK + h * M_DQK:MA_QK + (h + 1) * M_DQK]

    def v_aug_of(g, h):
        return jnp.concatenate([v_ref[g, :, h * M_DV:(h + 1) * M_DV], ones], axis=1)

    gates = []
    for g in range(n_seq):
        gc = gc_ref[g] * LOG2E
        gr = gc.T[0:2 * M_HEADS, :]
        bcol_all = sum(_dot(tri, p) for p in _split3(gc))
        brow_all = sum(lax.dot_general(p, tri, _NT, preferred_element_type=F32)
                       for p in _split3(gr))
        gates.append((gc, gr, bcol_all, brow_all))

    def qk_product(c):
        return lax.dot_general(q_of(*c), k_of(*c), _NT, preferred_element_type=F32)

    def gate_and_mix(c, qk):
        g, h = c
        gc, gr, bcol_all, brow_all = gates[g]
        c_row = gr[h:h + 1, :] - brow_all[M_HEADS + h:M_HEADS + h + 1, :]
        i_rep = jnp.broadcast_to(gc[:, h:h + 1], (L, 128))
        b_rep = jnp.broadcast_to(bcol_all[:, M_HEADS + h:M_HEADS + h + 1], (L, 128))
        m_prev = m_ref[g, h:h + 1, :]
        d_tiles = [jnp.where(causal[:, t * 128:(t + 1) * 128],
                             b_rep + c_row[:, t * 128:(t + 1) * 128], NEG) for t in range(nlt)]
        a = b_rep + m_prev
        m_intra = jnp.max(functools.reduce(jnp.maximum, d_tiles), axis=1, keepdims=True)
        m_out = jnp.maximum(a, m_intra)
        s_tiles = [(qk[:, t * 128:(t + 1) * 128]
                    * jnp.exp2(d_tiles[t] - m_out)).astype(BF16) for t in range(nlt)]
        w_inter = jnp.exp2(a - m_out).astype(BF16)
        ct = ct_ref[g, h]
        lhs = jnp.concatenate(s_tiles + [w_inter * q_of(g, h)], axis=1)
        rhs = jnp.concatenate([v_aug_of(g, h), ct.astype(BF16)], axis=0)
        return _dot(lhs, rhs), (m_out, b_rep, i_rep, m_prev)

    def finish(c, nd, kept):
        g, h = c
        m_out, b_rep, i_rep, m_prev = kept
        num = nd[:, 0:M_DV]
        den = nd[:, M_DV:M_DV + 128]
        inv = 1.0 / jnp.maximum(jnp.abs(den), jnp.exp2(-m_out))
        ms = jnp.mean(num * num, axis=1, keepdims=True)
        f = inv * lax.rsqrt(inv * inv * ms + EPS)
        hn = num * jnp.concatenate([f, f], axis=1) * hnw_ref[:, h * M_DV:(h + 1) * M_DV]
        y_ref[g, :, h * M_DV:(h + 1) * M_DV] = (og_ref[g, :, h * M_DV:(h + 1) * M_DV]
                                                * hn.astype(BF16))

        b_last = b_rep[L - 1:L, :]
        g_prev = b_last + m_prev
        gl = b_last - b_rep + i_rep
        m_new = jnp.maximum(g_prev, jnp.max(gl, axis=0, keepdims=True))
        wk = jnp.exp2(gl - m_new).astype(BF16)
        decay = jnp.exp2(g_prev - m_new)
        wv = jnp.concatenate([wk] * 3, axis=1) * v_aug_of(g, h)
        ct_ref[g, h] = (jnp.concatenate([decay] * 3, axis=1) * ct_ref[g, h]
                        + lax.dot_general(k_of(g, h), wv, _TN, preferred_element_type=F32))
        m_ref[g, h:h + 1, :] = m_new

    pending = []
    for g in range(n_seq):
        mine = [c for c in chains if c[0] == g]
        qk = [qk_product(c) for c in mine]
        mixed = [gate_and_mix(c, s) for c, s in zip(mine, qk)]
        for c, nd, kept in pending:
            finish(c, nd, kept)
        pending = [(c, nd, kept) for c, (nd, kept) in zip(mine, mixed)]
    for c, nd, kept in pending:
        finish(c, nd, kept)


def _mlstm(qk, v, og, gc, hnw):
    b, seq, _ = qk.shape
    L = MLSTM_L
    G = MLSTM_G
    blk = lambda w: pl.BlockSpec((G, L, w), lambda bi, ci: (bi, ci, 0))
    return pl.pallas_call(
        functools.partial(_mlstm_kernel, L=L),
        grid=(b // G, seq // L),
        in_specs=[
            blk(2 * MA_QK), blk(MA_V), blk(MA_V), blk(128),
            _const_spec(hnw.shape),
        ],
        out_specs=blk(MA_V),
        out_shape=jax.ShapeDtypeStruct((b, seq, MA_V), BF16),
        scratch_shapes=[
            pltpu.VMEM((G, M_HEADS, M_DQK, M_DV + 128), F32),
            pltpu.VMEM((G, 8, 128), F32),
        ],
        compiler_params=pltpu.CompilerParams(
            dimension_semantics=("parallel", "arbitrary"), vmem_limit_bytes=VMEM_LIMIT),
        name="mlstm",
    )(qk, v, og, gc, hnw)


def _attn_kernel(qn_ref, qpe_ref, kn_ref, kpe_ref, v_ref, o_ref,
                 kcat_ref, vaug_ref, qcat_ref, m_ref, acc_ref, sa_ref, sb_ref, *, tq, tk, seq):
    assert tq == 2 * tk
    h = pl.program_id(1)
    n_tiles = seq // tq
    nt = tk // 128

    @pl.when(h == 0)
    def _():
        kcat_ref[:, A_NOPE:2 * A_NOPE] = kpe_ref[...]
        vaug_ref[:, A_DV:2 * A_DV] = jnp.ones((seq, A_DV), BF16)

    kcat_ref[:, 0:A_NOPE] = kn_ref[...]
    vaug_ref[:, 0:A_DV] = v_ref[...]
    lane = lax.broadcasted_iota(jnp.int32, (seq, 128), 1)
    own = ((lane // A_ROPE) == (h % 2)).astype(BF16)
    qcat_ref[:, 0:A_NOPE] = qn_ref[...]
    qcat_ref[:, A_NOPE:2 * A_NOPE] = qpe_ref[...] * own

    def logits(qt, kb, r0=0, nrows=tq):
        q0 = pl.multiple_of(qt * tq + r0, tk)
        k0 = pl.multiple_of(kb * tk, tk)
        return lax.dot_general(qcat_ref[pl.ds(q0, nrows), :], kcat_ref[pl.ds(k0, tk), :],
                               _NT, preferred_element_type=F32)

    def softmax_pv(s, kb, r0, nrows, masked):
        k0 = pl.multiple_of(kb * tk, tk)
        if masked:
            r = lax.broadcasted_iota(jnp.int32, (nrows, tk), 0)
            c = lax.broadcasted_iota(jnp.int32, (nrows, tk), 1)
            s = jnp.where(c <= r, s, NEG)
        tiles = [s[:, t * 128:(t + 1) * 128] for t in range(nt)]
        mx = functools.reduce(jnp.maximum, tiles)
        m_prev = m_ref[r0:r0 + nrows, :]
        m_new = jnp.maximum(m_prev, jnp.max(mx, axis=1, keepdims=True))
        alpha = jnp.exp2(m_prev - m_new)
        p = jnp.concatenate([jnp.exp2(t - m_new).astype(BF16) for t in tiles], axis=1)
        pv = _dot(p, vaug_ref[pl.ds(k0, tk), :])
        acc_ref[r0:r0 + nrows, :] = (jnp.concatenate([alpha, alpha], axis=1)
                                     * acc_ref[r0:r0 + nrows, :] + pv)
        m_ref[r0:r0 + nrows, :] = m_new

    sa_ref[...] = logits(0, 0)

    def tile_body(qt, carry):
        m_ref[...] = jnp.full_like(m_ref, NEG)
        acc_ref[...] = jnp.zeros_like(acc_ref)

        def pair_body(i, c):
            j = 2 * i
            sb_ref[...] = logits(qt, j + 1)
            softmax_pv(sa_ref[...], j, 0, tq, False)
            sa_ref[...] = logits(qt, j + 2)
            softmax_pv(sb_ref[...], j + 1, 0, tq, False)
            return c

        lax.fori_loop(0, qt, pair_body, 0)

        d0 = 2 * qt
        sb_ref[0:tk, :] = logits(qt, d0 + 1, tk, tk)
        softmax_pv(sa_ref[...], d0, 0, tq, True)
        sa_ref[...] = logits(jnp.minimum(qt + 1, n_tiles - 1), 0)
        softmax_pv(sb_ref[0:tk, :], d0 + 1, tk, tk, True)
        o0 = pl.multiple_of(qt * tq, tq)
        o_ref[pl.ds(o0, tq), :] = (acc_ref[:, 0:A_DV] / acc_ref[:, A_DV:2 * A_DV]).astype(BF16)
        return carry

    lax.fori_loop(0, n_tiles, tile_body, 0)


def _attn(qn, qpe, kn, kpe, v):
    b, seq, _ = qn.shape
    tq, tk = ATTN_TQ, ATTN_TK
    return pl.pallas_call(
        functools.partial(_attn_kernel, tq=tq, tk=tk, seq=seq),
        grid=(b, A_HEADS),
        in_specs=[
            pl.BlockSpec((None, seq, A_NOPE), lambda bi, h: (bi, 0, h)),
            pl.BlockSpec((None, seq, 128), lambda bi, h: (bi, 0, h // 2)),
            pl.BlockSpec((None, seq, A_NOPE), lambda bi, h: (bi, 0, h)),
            pl.BlockSpec((None, seq, 128), lambda bi, h: (bi, 0, 0)),
            pl.BlockSpec((None, seq, A_DV), lambda bi, h: (bi, 0, h)),
        ],
        out_specs=pl.BlockSpec((None, seq, A_DV), lambda bi, h: (bi, 0, h)),
        out_shape=jax.ShapeDtypeStruct((b, seq, A_HEADS * A_DV), BF16),
        scratch_shapes=[
            pltpu.VMEM((seq, 2 * A_NOPE), BF16),
            pltpu.VMEM((seq, 2 * A_DV), BF16),
            pltpu.VMEM((seq, 2 * A_NOPE), BF16),
            pltpu.VMEM((tq, 128), F32),
            pltpu.VMEM((tq, 2 * A_DV), F32),
            pltpu.VMEM((tq, tk), F32),
            pltpu.VMEM((tq, tk), F32),
        ],
        compiler_params=pltpu.CompilerParams(
            dimension_semantics=("parallel", "arbitrary"),
            vmem_limit_bytes=VMEM_LIMIT),
        name="attn",
    )(qn, qpe, kn, kpe, v)


def _mlp_kernel(x_ref, ya_ref, yb_ref, ga_ref, gb_ref, mod_ref,
                npost_ref, npre2_ref, npost2_ref, wout_ref, w1_ref, w2_ref,
                o_ref, acc_ref):
    post_mix_gain = mod_ref[2:3, :] * npost_ref[...]
    shift_f = mod_ref[3:4, :]
    pre_mlp_gain = (1.0 + mod_ref[4:5, :]) * npre2_ref[...]
    post_mlp_gain = mod_ref[5:6, :] * npost2_ref[...]

    y = (ga_ref[...].astype(F32) * ya_ref[...].astype(F32)
         + gb_ref[...].astype(F32) * yb_ref[...].astype(F32))
    yo = _dot(y.astype(BF16), wout_ref[...])
    x1 = x_ref[...] + _rms(yo, post_mix_gain)

    h2 = (_rms(x1, pre_mlp_gain) + shift_f).astype(BF16)
    for c in range(D_FF // FF_CHUNK):
        u = jnp.maximum(_dot(h2, w1_ref[:, c * FF_CHUNK:(c + 1) * FF_CHUNK]), 0.0)
        part = _dot((u * u).astype(BF16), w2_ref[c * FF_CHUNK:(c + 1) * FF_CHUNK, :])
        if c == 0:
            acc_ref[...] = part
        else:
            acc_ref[...] += part
    o_ref[...] = x1 + _rms(acc_ref[...], post_mlp_gain)


def _mlp(x2, ya, yb, ga, gb, mod3, npost, npre2, npost2, wout, w1, w2, seq):
    t = x2.shape[0]
    tm = MLP_TM
    tiles_per_seq = seq // tm
    row = pl.BlockSpec((tm, D_MODEL), lambda i: (i, 0))
    return pl.pallas_call(
        _mlp_kernel,
        grid=(t // tm,),
        in_specs=[
            row, row, row, row, row,
            pl.BlockSpec((None, N_MOD, D_MODEL), lambda i: (i // tiles_per_seq, 0, 0)),
            _const_spec((1, D_MODEL)), _const_spec((1, D_MODEL)), _const_spec((1, D_MODEL)),
            _const_spec(wout.shape), _const_spec(w1.shape), _const_spec(w2.shape),
        ],
        out_specs=row,
        out_shape=jax.ShapeDtypeStruct((t, D_MODEL), F32),
        scratch_shapes=[pltpu.VMEM((tm, D_MODEL), F32)],
        compiler_params=pltpu.CompilerParams(
            dimension_semantics=("arbitrary",), vmem_limit_bytes=VMEM_LIMIT),
        name="mlp",
    )(x2, ya, yb, ga, gb, mod3, npost, npre2, npost2, wout, w1, w2)


def _rot_half_cols(w):
    half = w.shape[-1] // 2
    return jnp.concatenate([-w[..., half:], w[..., :half]], axis=-1)


def _regroup_kernel(w_ref, o_ref):
    sizes = (MA_QK, MA_QK, MA_V, MA_V, M_HEADS, M_HEADS, A_QRANK, A_KVRANK, A_ROPE,
             D_MODEL, D_MODEL)
    offs = [sum(sizes[:n]) for n in range(len(sizes))]
    o_q, _, _, _, o_i, _, o_cq, _, o_kpe, o_ga, _ = offs
    half = A_ROPE // 2
    cursor = [0]

    def put(src, n, negate=False):
        blk = w_ref[src:src + n, :]
        o_ref[cursor[0]:cursor[0] + n, :] = (-blk if negate else blk).astype(BF16)
        cursor[0] += n

    put(o_q, 2 * MA_QK + 2 * MA_V)
    put(o_ga, 2 * D_MODEL)
    put(o_cq, A_QRANK + A_KVRANK)
    put(o_kpe, A_ROPE)
    put(o_kpe, A_ROPE)
    for _ in range(2):
        put(o_kpe + half, half, negate=True)
        put(o_kpe, half)
    gate_rows = jnp.concatenate(
        [w_ref[o_i:o_i + 2 * M_HEADS, :],
         jnp.zeros((128 - 2 * M_HEADS, w_ref.shape[1]), F32)], axis=0)
    o_ref[cursor[0]:cursor[0] + 128, :] = gate_rows.astype(BF16)


def _prep_in_weights(w_in_t):
    n_in, k = w_in_t.shape
    n_out = 6 * D_MODEL
    return pl.pallas_call(
        _regroup_kernel,
        grid=(k // REGROUP_TK,),
        in_specs=[pl.BlockSpec((n_in, REGROUP_TK), lambda i: (0, i))],
        out_specs=pl.BlockSpec((n_out, REGROUP_TK), lambda i: (0, i)),
        out_shape=jax.ShapeDtypeStruct((n_out, k), BF16),
        compiler_params=pltpu.CompilerParams(
            dimension_semantics=("arbitrary",), vmem_limit_bytes=VMEM_LIMIT),
        name="regroup",
    )(w_in_t)


def _prep_mla_weights(w_uq, w_ukv):
    r = w_uq.reshape(A_QRANK, A_HEADS, A_NOPE + A_ROPE)
    nope = r[:, :, :A_NOPE].reshape(A_QRANK, A_HEADS * A_NOPE)
    pe = r[:, :, A_NOPE:]
    pe_rot = _rot_half_cols(pe).reshape(A_QRANK, A_HEADS * A_ROPE)
    pe = pe.reshape(A_QRANK, A_HEADS * A_ROPE)
    wuq = jnp.concatenate([nope, pe, pe_rot], axis=1).astype(BF16)
    r = w_ukv.reshape(A_KVRANK, A_HEADS, A_NOPE + A_DV)
    wukv = jnp.concatenate([r[:, :, :A_NOPE].reshape(A_KVRANK, -1),
                            r[:, :, A_NOPE:].reshape(A_KVRANK, -1)], axis=1).astype(BF16)
    return wuq, wukv


def _rope_tables(seq):
    half = A_ROPE // 2
    inv_freq = ROPE_THETA ** (-np.arange(half, dtype=np.float64) / half)
    ang = np.arange(seq, dtype=np.float64)[:, None] * inv_freq[None, :]
    reps = 128 // half
    return (jnp.asarray(np.tile(np.cos(ang), (1, reps)), F32),
            jnp.asarray(np.tile(np.sin(ang), (1, reps)), F32))


def kernel(x, c, w_ada, b_ada, norm_pre_mix, norm_post_mix, norm_pre_mlp, norm_post_mlp,
           w_in, mlstm_conv_w, mlstm_conv_b, mlstm_gate_b, mlstm_head_norm,
           mla_q_norm, mla_kv_norm, w_uq, w_ukv, w_out, w_ff1, w_ff2):
    bsz, seq, d = x.shape
    depth = w_ada.shape[0]
    cos, sin = _rope_tables(seq)
    c8 = jnp.pad(c, ((0, 8 - bsz), (0, 0)))
    qks = jnp.concatenate([jnp.ones((1, MA_QK), F32),
                           jnp.full((1, MA_QK), M_DQK ** -0.5, F32)], axis=1)
    row = lambda a: a.reshape(1, -1)
    x2 = x.reshape(bsz * seq, d)
    for l in range(depth):
        mod = _ada(c8, w_ada[l], row(b_ada[l]))[:bsz]
        mod3 = mod.reshape(bsz, N_MOD, d)
        win = _prep_in_weights(jnp.transpose(w_in[l]))
        wuq, wukv = _prep_mla_weights(w_uq[l], w_ukv[l])
        gbias = jnp.pad(row(mlstm_gate_b[l]), ((0, 0), (0, 128 - 2 * M_HEADS)))
        (qk, v_a, og, ga, gb, gates, qnope, qpe, knope, v_b, kpe) = _proj(
            x2, mod3, row(norm_pre_mix[l]), win, mlstm_conv_w[l], row(mlstm_conv_b[l]),
            qks, gbias, row(mla_q_norm[l]), row(mla_kv_norm[l]), wuq, wukv, cos, sin, seq)
        b3 = lambda a: a.reshape(bsz, seq, a.shape[-1])
        y_a = _mlstm(b3(qk), b3(v_a), b3(og), b3(gates), row(mlstm_head_norm[l]))
        y_b = _attn(b3(qnope), b3(qpe), b3(knope), b3(kpe), b3(v_b))
        x2 = _mlp(x2, y_a.reshape(bsz * seq, d), y_b.reshape(bsz * seq, d), ga, gb, mod3,
                  row(norm_post_mix[l]), row(norm_pre_mlp[l]), row(norm_post_mlp[l]),
                  w_out[l].astype(BF16), w_ff1[l].astype(BF16), w_ff2[l].astype(BF16), seq)
    return x2.reshape(bsz, seq, d)
```

```python
import functools

import jax
import jax.numpy as jnp
import numpy as np
from jax import lax
from jax.experimental import pallas as pl
from jax.experimental.pallas import tpu as pltpu

F32 = jnp.float32
BF16 = jnp.bfloat16

D_MODEL = 1024
M_HEADS = 4
M_DQK = 128
M_DV = 256
CONV_W = 4
A_HEADS = 8
A_NOPE = 128
A_ROPE = 64
A_DV = 128
A_QRANK = 384
A_KVRANK = 256
ROPE_THETA = 10000.0
D_FF = 4096
EPS = 1e-6
N_MOD = 6
MA_QK = M_HEADS * M_DQK
MA_V = M_HEADS * M_DV

PROJ_TM = 512
HALO = 16
MLSTM_L = 256
MLSTM_G = 4
ATTN_TQ = 1024
ATTN_TK = 512
MLP_TM = 512
FF_CHUNK = 1024
ADA_TN = 1536
REGROUP_TK = 256
VMEM_LIMIT = 56 * 1024 * 1024

NEG = -1e30
LOG2E = 1.4426950408889634

_NT = (((1,), (1,)), ((), ()))
_TN = (((0,), (0,)), ((), ()))


def _dot(a, b):
    return jnp.dot(a, b, preferred_element_type=F32)


def _sigmoid(x):
    return 1.0 / (1.0 + jnp.exp(-x))


def _rms(x, w):
    return x * lax.rsqrt(jnp.mean(x * x, axis=-1, keepdims=True) + EPS) * w


def _const_spec(shape):
    nd = len(shape)
    return pl.BlockSpec(shape, lambda *_: (0,) * nd, pipeline_mode=pl.Buffered(1))


def _ada_kernel(c_ref, w_ref, b_ref, o_ref):
    c = c_ref[...]
    a = (c * _sigmoid(c)).astype(BF16)
    o_ref[...] = _dot(a, w_ref[...].astype(BF16)) + b_ref[...]


def _ada(c8, w_ada, b_ada):
    n = w_ada.shape[1]
    return pl.pallas_call(
        _ada_kernel,
        grid=(n // ADA_TN,),
        in_specs=[
            pl.BlockSpec((8, D_MODEL), lambda j: (0, 0)),
            pl.BlockSpec((D_MODEL, ADA_TN), lambda j: (0, j)),
            pl.BlockSpec((1, ADA_TN), lambda j: (0, j)),
        ],
        out_specs=pl.BlockSpec((8, ADA_TN), lambda j: (0, j)),
        out_shape=jax.ShapeDtypeStruct((8, n), F32),
        compiler_params=pltpu.CompilerParams(
            dimension_semantics=("arbitrary",), vmem_limit_bytes=VMEM_LIMIT),
        name="ada",
    )(c8, w_ada, b_ada)


def _proj_kernel(x_ref, xh_ref, mod_ref, npre_ref,
                 win_ref,
                 convw_ref, convb_ref, qks_ref, gbias_ref, qn_ref, kvn_ref,
                 wuq_ref, wukv_ref, cos_ref, sin_ref,
                 qk_out, v_out, og_out, ga_out, gb_out, gates_out,
                 qnope_out, qpe_out, knope_out, vb_out, kpe_out,
                 hext_ref, z_ref, *, tm, tiles_per_seq, q_scale):
    i = pl.program_id(0)
    first = (i % tiles_per_seq) == 0
    wqk_ref, wv_ref, wo_ref, wga_ref, wgb_ref, ws_ref = (
        win_ref.at[n * D_MODEL:(n + 1) * D_MODEL, :] for n in range(6))

    def proj(a, wt_ref):
        return lax.dot_general(a, wt_ref[...], _NT, preferred_element_type=F32)

    shift = mod_ref[0:1, :]
    w_scaled = npre_ref[...] * (1.0 + mod_ref[1:2, :])

    def prenorm(xv):
        return _rms(xv, w_scaled) + shift

    hext_ref[HALO:, :] = prenorm(x_ref[...]).astype(BF16)
    hh = prenorm(xh_ref[...])
    hext_ref[0:HALO, :] = jnp.where(first, 0.0, hh).astype(BF16)

    z_ref[...] = proj(hext_ref[...], wqk_ref)
    acc = convb_ref[...]
    for j in range(CONV_W):
        off = HALO - (CONV_W - 1) + j
        acc = acc + convw_ref[j:j + 1, :] * z_ref[off:off + tm, :]
    qk_out[...] = (acc * _sigmoid(acc) * qks_ref[...]).astype(BF16)

    h = hext_ref[HALO:, :]
    v_out[...] = proj(h, wv_ref).astype(BF16)
    og_out[...] = _sigmoid(proj(h, wo_ref)).astype(BF16)
    ga_out[...] = _sigmoid(proj(h, wga_ref)).astype(BF16)
    gb_out[...] = _sigmoid(proj(h, wgb_ref)).astype(BF16)

    s = proj(h, ws_ref)
    c_q = s[:, 0:A_QRANK]
    c_kv = s[:, A_QRANK:A_QRANK + A_KVRANK]
    kp = s[:, 640:768]
    kpr = s[:, 768:896]
    g = s[:, 896:1024] + gbias_ref[...]

    lane = lax.broadcasted_iota(jnp.int32, g.shape, 1)
    logsig = jnp.minimum(g, 0.0) - jnp.log1p(jnp.exp(-jnp.abs(g)))
    gates_out[...] = jnp.where(lane < M_HEADS, g, logsig)

    cos = cos_ref[...]
    sin = sin_ref[...]
    kpe_out[...] = (kp * cos + kpr * sin).astype(BF16)

    q = _dot(_rms(c_q, qn_ref[...]).astype(BF16), wuq_ref[...])
    qnope_out[...] = (q[:, 0:1024] * q_scale).astype(BF16)
    cos4 = jnp.concatenate([cos] * 4, axis=1)
    sin4 = jnp.concatenate([sin] * 4, axis=1)
    qpe = q[:, 1024:1536] * cos4 + q[:, 1536:2048] * sin4
    qpe_out[...] = (qpe * q_scale).astype(BF16)

    kv = _dot(_rms(c_kv, kvn_ref[...]).astype(BF16), wukv_ref[...])
    knope_out[...] = kv[:, 0:1024].astype(BF16)
    vb_out[...] = kv[:, 1024:2048].astype(BF16)


def _proj(x2, mod3, npre, win, convw, convb, qks, gbias, qn, kvn, wuq, wukv,
          cos, sin, seq):
    t = x2.shape[0]
    tm = PROJ_TM
    tiles_per_seq = seq // tm
    hb = tm // HALO
    row = lambda i: (i, 0)
    out_w = [1024, 1024, 1024, 1024, 1024, 128, 1024, 512, 1024, 1024, 128]
    out_dt = [BF16] * 5 + [F32] + [BF16] * 5
    kern = functools.partial(
        _proj_kernel, tm=tm, tiles_per_seq=tiles_per_seq,
        q_scale=float((A_NOPE + A_ROPE) ** -0.5 * LOG2E))
    return pl.pallas_call(
        kern,
        grid=(t // tm,),
        in_specs=[
            pl.BlockSpec((tm, D_MODEL), row),
            pl.BlockSpec((HALO, D_MODEL), lambda i: (jnp.maximum(i * hb - 1, 0), 0)),
            pl.BlockSpec((None, N_MOD, D_MODEL), lambda i: (i // tiles_per_seq, 0, 0)),
            _const_spec((1, D_MODEL)),
            _const_spec(win.shape),
            _const_spec(convw.shape), _const_spec(convb.shape), _const_spec(qks.shape),
            _const_spec(gbias.shape), _const_spec(qn.shape), _const_spec(kvn.shape),
            _const_spec(wuq.shape), _const_spec(wukv.shape),
            pl.BlockSpec((tm, 128), lambda i: (i % tiles_per_seq, 0)),
            pl.BlockSpec((tm, 128), lambda i: (i % tiles_per_seq, 0)),
        ],
        out_specs=[pl.BlockSpec((tm, n), row) for n in out_w],
        out_shape=[jax.ShapeDtypeStruct((t, n), d) for n, d in zip(out_w, out_dt)],
        scratch_shapes=[
            pltpu.VMEM((tm + HALO, D_MODEL), BF16),
            pltpu.VMEM((tm + HALO, 2 * MA_QK), F32),
        ],
        compiler_params=pltpu.CompilerParams(
            dimension_semantics=("arbitrary",), vmem_limit_bytes=VMEM_LIMIT),
        name="proj",
    )(x2, x2, mod3, npre, win, convw, convb, qks, gbias,
      qn, kvn, wuq, wukv, cos, sin)


def _split3(x):
    hi = x.astype(BF16)
    r = x - hi.astype(F32)
    mid = r.astype(BF16)
    lo = (r - mid.astype(F32)).astype(BF16)
    return hi, mid, lo


def _mlstm_kernel(qk_ref, v_ref, og_ref, gc_ref, hnw_ref, y_ref,
                  ct_ref, m_ref, *, L):
    @pl.when(pl.program_id(1) == 0)
    def _():
        ct_ref[...] = jnp.zeros_like(ct_ref)
        m_ref[...] = jnp.zeros_like(m_ref)

    n_seq = qk_ref.shape[0]
    nlt = L // 128
    row = lax.broadcasted_iota(jnp.int32, (L, L), 0)
    col = lax.broadcasted_iota(jnp.int32, (L, L), 1)
    causal = col <= row
    tri = causal.astype(BF16)
    ones = jnp.ones((L, 128), BF16)
    chains = [(g, h) for g in range(n_seq) for h in range(M_HEADS)]

    def q_of(g, h):
        return qk_ref[g, :, h * M_DQK:(h + 1) * M_DQK]

    def k_of(g, h):
        return qk_ref[g, :, MA_QK + h * M_DQK:MA_QK + (h + 1) * M_DQK]

    def v_aug_of(g, h):
        return jnp.concatenate([v_ref[g, :, h * M_DV:(h + 1) * M_DV], ones], axis=1)

    gates = []
    for g in range(n_seq):
        gc = gc_ref[g] * LOG2E
        gr = gc.T[0:2 * M_HEADS, :]
        bcol_all = sum(_dot(tri, p) for p in _split3(gc))
        brow_all = sum(lax.dot_general(p, tri, _NT, preferred_element_type=F32)
                       for p in _split3(gr))
        gates.append((gc, gr, bcol_all, brow_all))

    def qk_product(c):
        return lax.dot_general(q_of(*c), k_of(*c), _NT, preferred_element_type=F32)

    def gate_and_mix(c, qk):
        g, h = c
        gc, gr, bcol_all, brow_all = gates[g]
        c_row = gr[h:h + 1, :] - brow_all[M_HEADS + h:M_HEADS + h + 1, :]
        i_rep = jnp.broadcast_to(gc[:, h:h + 1], (L, 128))
        b_rep = jnp.broadcast_to(bcol_all[:, M_HEADS + h:M_HEADS + h + 1], (L, 128))
        m_prev = m_ref[g, h:h + 1, :]
        a = b_rep + m_prev
        m_rows, s_rows = [], []
        for rb in range(nlt):
            rows = slice(rb * 128, (rb + 1) * 128)
            d_row = [b_rep[rows] + c_row[:, t * 128:(t + 1) * 128] for t in range(rb + 1)]
            d_row[rb] = jnp.where(causal[0:128, 0:128], d_row[rb], NEG)
            m_intra = jnp.max(functools.reduce(jnp.maximum, d_row), axis=1, keepdims=True)
            m_blk = jnp.maximum(a[rows], m_intra)
            tiles = [(qk[rows, t * 128:(t + 1) * 128]
                      * jnp.exp2(d_row[t] - m_blk)).astype(BF16) for t in range(rb + 1)]
            tiles += [jnp.zeros((128, 128), BF16)] * (nlt - 1 - rb)
            m_rows.append(m_blk)
            s_rows.append(jnp.concatenate(tiles, axis=1))
        m_out = jnp.concatenate(m_rows, axis=0)
        s_mat = jnp.concatenate(s_rows, axis=0)
        w_inter = jnp.exp2(a - m_out).astype(BF16)
        ct = ct_ref[g, h]
        lhs = jnp.concatenate([s_mat, w_inter * q_of(g, h)], axis=1)
        rhs = jnp.concatenate([v_aug_of(g, h), ct.astype(BF16)], axis=0)
        return _dot(lhs, rhs), (m_out, b_rep, i_rep, m_prev)

    def finish(c, nd, kept):
        g, h = c
        m_out, b_rep, i_rep, m_prev = kept
        num = nd[:, 0:M_DV]
        den = nd[:, M_DV:M_DV + 128]
        inv = 1.0 / jnp.maximum(jnp.abs(den), jnp.exp2(-m_out))
        ms = jnp.mean(num * num, axis=1, keepdims=True)
        f = inv * lax.rsqrt(inv * inv * ms + EPS)
        hn = num * jnp.concatenate([f, f], axis=1) * hnw_ref[:, h * M_DV:(h + 1) * M_DV]
        y_ref[g, :, h * M_DV:(h + 1) * M_DV] = (og_ref[g, :, h * M_DV:(h + 1) * M_DV]
                                                * hn.astype(BF16))

        b_last = b_rep[L - 1:L, :]
        g_prev = b_last + m_prev
        gl = b_last - b_rep + i_rep
        m_new = jnp.maximum(g_prev, jnp.max(gl, axis=0, keepdims=True))
        wk = jnp.exp2(gl - m_new).astype(BF16)
        decay = jnp.exp2(g_prev - m_new)
        wv = jnp.concatenate([wk] * 3, axis=1) * v_aug_of(g, h)
        ct_ref[g, h] = (jnp.concatenate([decay] * 3, axis=1) * ct_ref[g, h]
                        + lax.dot_general(k_of(g, h), wv, _TN, preferred_element_type=F32))
        m_ref[g, h:h + 1, :] = m_new

    pending = []
    for g in range(n_seq):
        mine = [c for c in chains if c[0] == g]
        qk = [qk_product(c) for c in mine]
        mixed = [gate_and_mix(c, s) for c, s in zip(mine, qk)]
        for c, nd, kept in pending:
            finish(c, nd, kept)
        pending = [(c, nd, kept) for c, (nd, kept) in zip(mine, mixed)]
    for c, nd, kept in pending:
        finish(c, nd, kept)


def _mlstm(qk, v, og, gc, hnw):
    b, seq, _ = qk.shape
    L = MLSTM_L
    G = MLSTM_G
    blk = lambda w: pl.BlockSpec((G, L, w), lambda bi, ci: (bi, ci, 0))
    return pl.pallas_call(
        functools.partial(_mlstm_kernel, L=L),
        grid=(b // G, seq // L),
        in_specs=[
            blk(2 * MA_QK), blk(MA_V), blk(MA_V), blk(128),
            _const_spec(hnw.shape),
        ],
        out_specs=blk(MA_V),
        out_shape=jax.ShapeDtypeStruct((b, seq, MA_V), BF16),
        scratch_shapes=[
            pltpu.VMEM((G, M_HEADS, M_DQK, M_DV + 128), F32),
            pltpu.VMEM((G, 8, 128), F32),
        ],
        compiler_params=pltpu.CompilerParams(
            dimension_semantics=("parallel", "arbitrary"), vmem_limit_bytes=VMEM_LIMIT),
        name="mlstm",
    )(qk, v, og, gc, hnw)


def _attn_kernel(qn_ref, qpe_ref, kn_ref, kpe_ref, v_ref, o_ref,
                 kcat_ref, vaug_ref, qcat_ref, m_ref, acc_ref, sa_ref, sb_ref, *, tq, tk, seq):
    assert tq == 2 * tk
    h = pl.program_id(1)
    n_tiles = seq // tq
    nt = tk // 128

    @pl.when(h == 0)
    def _():
        kcat_ref[:, A_NOPE:2 * A_NOPE] = kpe_ref[...]
        vaug_ref[:, A_DV:2 * A_DV] = jnp.ones((seq, A_DV), BF16)

    kcat_ref[:, 0:A_NOPE] = kn_ref[...]
    vaug_ref[:, 0:A_DV] = v_ref[...]
    lane = lax.broadcasted_iota(jnp.int32, (seq, 128), 1)
    own = ((lane // A_ROPE) == (h % 2)).astype(BF16)
    qcat_ref[:, 0:A_NOPE] = qn_ref[...]
    qcat_ref[:, A_NOPE:2 * A_NOPE] = qpe_ref[...] * own

    def logits(qt, kb, r0=0, nrows=tq):
        q0 = pl.multiple_of(qt * tq + r0, tk)
        k0 = pl.multiple_of(kb * tk, tk)
        return lax.dot_general(qcat_ref[pl.ds(q0, nrows), :], kcat_ref[pl.ds(k0, tk), :],
                               _NT, preferred_element_type=F32)

    def softmax_pv(s, kb, r0, nrows, masked):
        k0 = pl.multiple_of(kb * tk, tk)
        if masked:
            r = lax.broadcasted_iota(jnp.int32, (nrows, tk), 0)
            c = lax.broadcasted_iota(jnp.int32, (nrows, tk), 1)
            s = jnp.where(c <= r, s, NEG)
        tiles = [s[:, t * 128:(t + 1) * 128] for t in range(nt)]
        mx = functools.reduce(jnp.maximum, tiles)
        m_prev = m_ref[r0:r0 + nrows, :]
        m_new = jnp.maximum(m_prev, jnp.max(mx, axis=1, keepdims=True))
        alpha = jnp.exp2(m_prev - m_new)
        p = jnp.concatenate([jnp.exp2(t - m_new).astype(BF16) for t in tiles], axis=1)
        pv = _dot(p, vaug_ref[pl.ds(k0, tk), :])
        acc_ref[r0:r0 + nrows, :] = (jnp.concatenate([alpha, alpha], axis=1)
                                     * acc_ref[r0:r0 + nrows, :] + pv)
        m_ref[r0:r0 + nrows, :] = m_new

    sa_ref[...] = logits(0, 0)

    def tile_body(qt, carry):
        m_ref[...] = jnp.full_like(m_ref, NEG)
        acc_ref[...] = jnp.zeros_like(acc_ref)

        def pair_body(i, c):
            j = 2 * i
            sb_ref[...] = logits(qt, j + 1)
            softmax_pv(sa_ref[...], j, 0, tq, False)
            sa_ref[...] = logits(qt, j + 2)
            softmax_pv(sb_ref[...], j + 1, 0, tq, False)
            return c

        lax.fori_loop(0, qt, pair_body, 0)

        d0 = 2 * qt
        sb_ref[0:tk, :] = logits(qt, d0 + 1, tk, tk)
        softmax_pv(sa_ref[...], d0, 0, tq, True)
        sa_ref[...] = logits(jnp.minimum(qt + 1, n_tiles - 1), 0)
        softmax_pv(sb_ref[0:tk, :], d0 + 1, tk, tk, True)
        o0 = pl.multiple_of(qt * tq, tq)
        o_ref[pl.ds(o0, tq), :] = (acc_ref[:, 0:A_DV] / acc_ref[:, A_DV:2 * A_DV]).astype(BF16)
        return carry

    lax.fori_loop(0, n_tiles, tile_body, 0)


def _attn(qn, qpe, kn, kpe, v):
    b, seq, _ = qn.shape
    tq, tk = ATTN_TQ, ATTN_TK
    return pl.pallas_call(
        functools.partial(_attn_kernel, tq=tq, tk=tk, seq=seq),
        grid=(b, A_HEADS),
        in_specs=[
            pl.BlockSpec((None, seq, A_NOPE), lambda bi, h: (bi, 0, h)),
            pl.BlockSpec((None, seq, 128), lambda bi, h: (bi, 0, h // 2)),
            pl.BlockSpec((None, seq, A_NOPE), lambda bi, h: (bi, 0, h)),
            pl.BlockSpec((None, seq, 128), lambda bi, h: (bi, 0, 0)),
            pl.BlockSpec((None, seq, A_DV), lambda bi, h: (bi, 0, h)),
        ],
        out_specs=pl.BlockSpec((None, seq, A_DV), lambda bi, h: (bi, 0, h)),
        out_shape=jax.ShapeDtypeStruct((b, seq, A_HEADS * A_DV), BF16),
        scratch_shapes=[
            pltpu.VMEM((seq, 2 * A_NOPE), BF16),
            pltpu.VMEM((seq, 2 * A_DV), BF16),
            pltpu.VMEM((seq, 2 * A_NOPE), BF16),
            pltpu.VMEM((tq, 128), F32),
            pltpu.VMEM((tq, 2 * A_DV), F32),
            pltpu.VMEM((tq, tk), F32),
            pltpu.VMEM((tq, tk), F32),
        ],
        compiler_params=pltpu.CompilerParams(
            dimension_semantics=("parallel", "arbitrary"),
            vmem_limit_bytes=VMEM_LIMIT),
        name="attn",
    )(qn, qpe, kn, kpe, v)


def _mlp_kernel(x_ref, ya_ref, yb_ref, ga_ref, gb_ref, mod_ref,
                npost_ref, npre2_ref, npost2_ref, wout_ref, w1_ref, w2_ref,
                o_ref, acc_ref):
    post_mix_gain = mod_ref[2:3, :] * npost_ref[...]
    shift_f = mod_ref[3:4, :]
    pre_mlp_gain = (1.0 + mod_ref[4:5, :]) * npre2_ref[...]
    post_mlp_gain = mod_ref[5:6, :] * npost2_ref[...]

    y = (ga_ref[...].astype(F32) * ya_ref[...].astype(F32)
         + gb_ref[...].astype(F32) * yb_ref[...].astype(F32))
    yo = _dot(y.astype(BF16), wout_ref[...])
    x1 = x_ref[...] + _rms(yo, post_mix_gain)

    h2 = (_rms(x1, pre_mlp_gain) + shift_f).astype(BF16)
    for c in range(D_FF // FF_CHUNK):
        u = jnp.maximum(_dot(h2, w1_ref[:, c * FF_CHUNK:(c + 1) * FF_CHUNK]), 0.0)
        part = _dot((u * u).astype(BF16), w2_ref[c * FF_CHUNK:(c + 1) * FF_CHUNK, :])
        if c == 0:
            acc_ref[...] = part
        else:
            acc_ref[...] += part
    o_ref[...] = x1 + _rms(acc_ref[...], post_mlp_gain)


def _mlp(x2, ya, yb, ga, gb, mod3, npost, npre2, npost2, wout, w1, w2, seq):
    t = x2.shape[0]
    tm = MLP_TM
    tiles_per_seq = seq // tm
    row = pl.BlockSpec((tm, D_MODEL), lambda i: (i, 0))
    return pl.pallas_call(
        _mlp_kernel,
        grid=(t // tm,),
        in_specs=[
            row, row, row, row, row,
            pl.BlockSpec((None, N_MOD, D_MODEL), lambda i: (i // tiles_per_seq, 0, 0)),
            _const_spec((1, D_MODEL)), _const_spec((1, D_MODEL)), _const_spec((1, D_MODEL)),
            _const_spec(wout.shape), _const_spec(w1.shape), _const_spec(w2.shape),
        ],
        out_specs=row,
        out_shape=jax.ShapeDtypeStruct((t, D_MODEL), F32),
        scratch_shapes=[pltpu.VMEM((tm, D_MODEL), F32)],
        compiler_params=pltpu.CompilerParams(
            dimension_semantics=("arbitrary",), vmem_limit_bytes=VMEM_LIMIT),
        name="mlp",
    )(x2, ya, yb, ga, gb, mod3, npost, npre2, npost2, wout, w1, w2)


def _rot_half_cols(w):
    half = w.shape[-1] // 2
    return jnp.concatenate([-w[..., half:], w[..., :half]], axis=-1)


def _regroup_kernel(w_ref, o_ref):
    sizes = (MA_QK, MA_QK, MA_V, MA_V, M_HEADS, M_HEADS, A_QRANK, A_KVRANK, A_ROPE,
             D_MODEL, D_MODEL)
    offs = [sum(sizes[:n]) for n in range(len(sizes))]
    o_q, _, _, _, o_i, _, o_cq, _, o_kpe, o_ga, _ = offs
    half = A_ROPE // 2
    cursor = [0]

    def put(src, n, negate=False):
        blk = w_ref[src:src + n, :]
        o_ref[cursor[0]:cursor[0] + n, :] = (-blk if negate else blk).astype(BF16)
        cursor[0] += n

    put(o_q, 2 * MA_QK + 2 * MA_V)
    put(o_ga, 2 * D_MODEL)
    put(o_cq, A_QRANK + A_KVRANK)
    put(o_kpe, A_ROPE)
    put(o_kpe, A_ROPE)
    for _ in range(2):
        put(o_kpe + half, half, negate=True)
        put(o_kpe, half)
    gate_rows = jnp.concatenate(
        [w_ref[o_i:o_i + 2 * M_HEADS, :],
         jnp.zeros((128 - 2 * M_HEADS, w_ref.shape[1]), F32)], axis=0)
    o_ref[cursor[0]:cursor[0] + 128, :] = gate_rows.astype(BF16)


def _prep_in_weights(w_in_t):
    n_in, k = w_in_t.shape
    n_out = 6 * D_MODEL
    return pl.pallas_call(
        _regroup_kernel,
        grid=(k // REGROUP_TK,),
        in_specs=[pl.BlockSpec((n_in, REGROUP_TK), lambda i: (0, i))],
        out_specs=pl.BlockSpec((n_out, REGROUP_TK), lambda i: (0, i)),
        out_shape=jax.ShapeDtypeStruct((n_out, k), BF16),
        compiler_params=pltpu.CompilerParams(
            dimension_semantics=("arbitrary",), vmem_limit_bytes=VMEM_LIMIT),
        name="regroup",
    )(w_in_t)


def _prep_mla_weights(w_uq, w_ukv):
    r = w_uq.reshape(A_QRANK, A_HEADS, A_NOPE + A_ROPE)
    nope = r[:, :, :A_NOPE].reshape(A_QRANK, A_HEADS * A_NOPE)
    pe = r[:, :, A_NOPE:]
    pe_rot = _rot_half_cols(pe).reshape(A_QRANK, A_HEADS * A_ROPE)
    pe = pe.reshape(A_QRANK, A_HEADS * A_ROPE)
    wuq = jnp.concatenate([nope, pe, pe_rot], axis=1).astype(BF16)
    r = w_ukv.reshape(A_KVRANK, A_HEADS, A_NOPE + A_DV)
    wukv = jnp.concatenate([r[:, :, :A_NOPE].reshape(A_KVRANK, -1),
                            r[:, :, A_NOPE:].reshape(A_KVRANK, -1)], axis=1).astype(BF16)
    return wuq, wukv


def _rope_tables(seq):
    half = A_ROPE // 2
    inv_freq = ROPE_THETA ** (-np.arange(half, dtype=np.float64) / half)
    ang = np.arange(seq, dtype=np.float64)[:, None] * inv_freq[None, :]
    reps = 128 // half
    return (jnp.asarray(np.tile(np.cos(ang), (1, reps)), F32),
            jnp.asarray(np.tile(np.sin(ang), (1, reps)), F32))


def kernel(x, c, w_ada, b_ada, norm_pre_mix, norm_post_mix, norm_pre_mlp, norm_post_mlp,
           w_in, mlstm_conv_w, mlstm_conv_b, mlstm_gate_b, mlstm_head_norm,
           mla_q_norm, mla_kv_norm, w_uq, w_ukv, w_out, w_ff1, w_ff2):
    bsz, seq, d = x.shape
    depth = w_ada.shape[0]
    cos, sin = _rope_tables(seq)
    c8 = jnp.pad(c, ((0, 8 - bsz), (0, 0)))
    qks = jnp.concatenate([jnp.ones((1, MA_QK), F32),
                           jnp.full((1, MA_QK), M_DQK ** -0.5, F32)], axis=1)
    row = lambda a: a.reshape(1, -1)
    x2 = x.reshape(bsz * seq, d)
    for l in range(depth):
        mod = _ada(c8, w_ada[l], row(b_ada[l]))[:bsz]
        mod3 = mod.reshape(bsz, N_MOD, d)
        win = _prep_in_weights(jnp.transpose(w_in[l]))
        wuq, wukv = _prep_mla_weights(w_uq[l], w_ukv[l])
        gbias = jnp.pad(row(mlstm_gate_b[l]), ((0, 0), (0, 128 - 2 * M_HEADS)))
        (qk, v_a, og, ga, gb, gates, qnope, qpe, knope, v_b, kpe) = _proj(
            x2, mod3, row(norm_pre_mix[l]), win, mlstm_conv_w[l], row(mlstm_conv_b[l]),
            qks, gbias, row(mla_q_norm[l]), row(mla_kv_norm[l]), wuq, wukv, cos, sin, seq)
        b3 = lambda a: a.reshape(bsz, seq, a.shape[-1])
        y_a = _mlstm(b3(qk), b3(v_a), b3(og), b3(gates), row(mlstm_head_norm[l]))
        y_b = _attn(b3(qnope), b3(qpe), b3(knope), b3(kpe), b3(v_b))
        x2 = _mlp(x2, y_a.reshape(bsz * seq, d), y_b.reshape(bsz * seq, d), ga, gb, mod3,
                  row(norm_post_mix[l]), row(norm_pre_mlp[l]), row(norm_post_mlp[l]),
                  w_out[l].astype(BF16), w_ff1[l].astype(BF16), w_ff2[l].astype(BF16), seq)
    return x2.reshape(bsz, seq, d)
```

```python
import functools

import jax
import jax.numpy as jnp
import numpy as np
from jax import lax
from jax.experimental import pallas as pl
from jax.experimental.pallas import tpu as pltpu

F32 = jnp.float32
BF16 = jnp.bfloat16

D_MODEL = 1024
M_HEADS = 4
M_DQK = 128
M_DV = 256
CONV_W = 4
A_HEADS = 8
A_NOPE = 128
A_ROPE = 64
A_DV = 128
A_QRANK = 384
A_KVRANK = 256
ROPE_THETA = 10000.0
D_FF = 4096
EPS = 1e-6
N_MOD = 6
MA_QK = M_HEADS * M_DQK
MA_V = M_HEADS * M_DV

PROJ_TM = 512
HALO = 16
MLSTM_L = 256
MLSTM_G = 4
ATTN_TQ = 1024
ATTN_TK = 512
MLP_TM = 512
FF_CHUNK = 1024
ADA_TN = 1536
REGROUP_TK = 256
VMEM_LIMIT = 56 * 1024 * 1024

NEG = -1e30
LOG2E = 1.4426950408889634

_NT = (((1,), (1,)), ((), ()))
_TN = (((0,), (0,)), ((), ()))


def _dot(a, b):
    return jnp.dot(a, b, preferred_element_type=F32)


def _sigmoid(x):
    return 1.0 / (1.0 + jnp.exp(-x))


def _rms(x, w):
    return x * lax.rsqrt(jnp.mean(x * x, axis=-1, keepdims=True) + EPS) * w


def _const_spec(shape):
    nd = len(shape)
    return pl.BlockSpec(shape, lambda *_: (0,) * nd, pipeline_mode=pl.Buffered(1))


def _ada_kernel(c_ref, w_ref, b_ref, o_ref):
    c = c_ref[...]
    a = (c * _sigmoid(c)).astype(BF16)
    o_ref[...] = _dot(a, w_ref[...].astype(BF16)) + b_ref[...]


def _ada(c8, w_ada, b_ada):
    n = w_ada.shape[1]
    return pl.pallas_call(
        _ada_kernel,
        grid=(n // ADA_TN,),
        in_specs=[
            pl.BlockSpec((8, D_MODEL), lambda j: (0, 0)),
            pl.BlockSpec((D_MODEL, ADA_TN), lambda j: (0, j)),
            pl.BlockSpec((1, ADA_TN), lambda j: (0, j)),
        ],
        out_specs=pl.BlockSpec((8, ADA_TN), lambda j: (0, j)),
        out_shape=jax.ShapeDtypeStruct((8, n), F32),
        compiler_params=pltpu.CompilerParams(
            dimension_semantics=("arbitrary",), vmem_limit_bytes=VMEM_LIMIT),
        name="ada",
    )(c8, w_ada, b_ada)


def _proj_kernel(x_ref, xh_ref, mod_ref, npre_ref,
                 win_ref,
                 convw_ref, convb_ref, qks_ref, gbias_ref, qn_ref, kvn_ref,
                 wuq_ref, wukv_ref, cos_ref, sin_ref,
                 qk_out, v_out, og_out, ga_out, gb_out, gates_out,
                 qnope_out, qpe_out, knope_out, vb_out, kpe_out,
                 hext_ref, z_ref, *, tm, tiles_per_seq, q_scale):
    i = pl.program_id(0)
    first = (i % tiles_per_seq) == 0
    wqk_ref, wv_ref, wo_ref, wga_ref, wgb_ref, ws_ref = (
        win_ref.at[n * D_MODEL:(n + 1) * D_MODEL, :] for n in range(6))

    def proj(a, wt_ref):
        return lax.dot_general(a, wt_ref[...], _NT, preferred_element_type=F32)

    shift = mod_ref[0:1, :]
    w_scaled = npre_ref[...] * (1.0 + mod_ref[1:2, :])

    def prenorm(xv):
        return _rms(xv, w_scaled) + shift

    hext_ref[HALO:, :] = prenorm(x_ref[...]).astype(BF16)
    hh = prenorm(xh_ref[...])
    hext_ref[0:HALO, :] = jnp.where(first, 0.0, hh).astype(BF16)

    z_ref[...] = proj(hext_ref[...], wqk_ref)
    zwin = z_ref[HALO - 8:, :]
    n_win = tm + 8
    acc = convb_ref[...]
    for j in range(CONV_W):
        lead = 8 - (CONV_W - 1) + j
        tap = zwin if lead == 0 else pltpu.roll(zwin, n_win - lead, 0)
        acc = acc + convw_ref[j:j + 1, :] * tap[0:tm, :]
    qk_out[...] = (acc * _sigmoid(acc) * qks_ref[...]).astype(BF16)

    h = hext_ref[HALO:, :]
    v_out[...] = proj(h, wv_ref).astype(BF16)
    og_out[...] = _sigmoid(proj(h, wo_ref)).astype(BF16)
    ga_out[...] = _sigmoid(proj(h, wga_ref)).astype(BF16)
    gb_out[...] = _sigmoid(proj(h, wgb_ref)).astype(BF16)

    s = proj(h, ws_ref)
    c_q = s[:, 0:A_QRANK]
    c_kv = s[:, A_QRANK:A_QRANK + A_KVRANK]
    kp = s[:, 640:768]
    kpr = s[:, 768:896]
    g = s[:, 896:1024] + gbias_ref[...]

    lane = lax.broadcasted_iota(jnp.int32, g.shape, 1)
    logsig = jnp.minimum(g, 0.0) - jnp.log1p(jnp.exp(-jnp.abs(g)))
    gates_out[...] = jnp.where(lane < M_HEADS, g, logsig)

    cos = cos_ref[...]
    sin = sin_ref[...]
    kpe_out[...] = (kp * cos + kpr * sin).astype(BF16)

    q = _dot(_rms(c_q, qn_ref[...]).astype(BF16), wuq_ref[...])
    qnope_out[...] = (q[:, 0:1024] * q_scale).astype(BF16)
    cos4 = jnp.concatenate([cos] * 4, axis=1)
    sin4 = jnp.concatenate([sin] * 4, axis=1)
    qpe = q[:, 1024:1536] * cos4 + q[:, 1536:2048] * sin4
    qpe_out[...] = (qpe * q_scale).astype(BF16)

    kv = _dot(_rms(c_kv, kvn_ref[...]).astype(BF16), wukv_ref[...])
    knope_out[...] = kv[:, 0:1024].astype(BF16)
    vb_out[...] = kv[:, 1024:2048].astype(BF16)


def _proj(x2, mod3, npre, win, convw, convb, qks, gbias, qn, kvn, wuq, wukv,
          cos, sin, seq):
    t = x2.shape[0]
    tm = PROJ_TM
    tiles_per_seq = seq // tm
    hb = tm // HALO
    row = lambda i: (i, 0)
    out_w = [1024, 1024, 1024, 1024, 1024, 128, 1024, 512, 1024, 1024, 128]
    out_dt = [BF16] * 5 + [F32] + [BF16] * 5
    kern = functools.partial(
        _proj_kernel, tm=tm, tiles_per_seq=tiles_per_seq,
        q_scale=float((A_NOPE + A_ROPE) ** -0.5 * LOG2E))
    return pl.pallas_call(
        kern,
        grid=(t // tm,),
        in_specs=[
            pl.BlockSpec((tm, D_MODEL), row),
            pl.BlockSpec((HALO, D_MODEL), lambda i: (jnp.maximum(i * hb - 1, 0), 0)),
            pl.BlockSpec((None, N_MOD, D_MODEL), lambda i: (i // tiles_per_seq, 0, 0)),
            _const_spec((1, D_MODEL)),
            _const_spec(win.shape),
            _const_spec(convw.shape), _const_spec(convb.shape), _const_spec(qks.shape),
            _const_spec(gbias.shape), _const_spec(qn.shape), _const_spec(kvn.shape),
            _const_spec(wuq.shape), _const_spec(wukv.shape),
            pl.BlockSpec((tm, 128), lambda i: (i % tiles_per_seq, 0)),
            pl.BlockSpec((tm, 128), lambda i: (i % tiles_per_seq, 0)),
        ],
        out_specs=[pl.BlockSpec((tm, n), row) for n in out_w],
        out_shape=[jax.ShapeDtypeStruct((t, n), d) for n, d in zip(out_w, out_dt)],
        scratch_shapes=[
            pltpu.VMEM((tm + HALO, D_MODEL), BF16),
            pltpu.VMEM((tm + HALO, 2 * MA_QK), F32),
        ],
        compiler_params=pltpu.CompilerParams(
            dimension_semantics=("arbitrary",), vmem_limit_bytes=VMEM_LIMIT),
        name="proj",
    )(x2, x2, mod3, npre, win, convw, convb, qks, gbias,
      qn, kvn, wuq, wukv, cos, sin)


def _split3(x):
    hi = x.astype(BF16)
    r = x - hi.astype(F32)
    mid = r.astype(BF16)
    lo = (r - mid.astype(F32)).astype(BF16)
    return hi, mid, lo


def _mlstm_kernel(qk_ref, v_ref, og_ref, gc_ref, hnw_ref, y_ref,
                  ct_ref, m_ref, *, L):
    @pl.when(pl.program_id(1) == 0)
    def _():
        ct_ref[...] = jnp.zeros_like(ct_ref)
        m_ref[...] = jnp.zeros_like(m_ref)

    n_seq = qk_ref.shape[0]
    nlt = L // 128
    row = lax.broadcasted_iota(jnp.int32, (L, L), 0)
    col = lax.broadcasted_iota(jnp.int32, (L, L), 1)
    causal = col <= row
    tri = causal.astype(BF16)
    ones = jnp.ones((L, 128), BF16)
    chains = [(g, h) for g in range(n_seq) for h in range(M_HEADS)]

    def q_of(g, h):
        return qk_ref[g, :, h * M_DQK:(h + 1) * M_DQK]

    def k_of(g, h):
        return qk_ref[g, :, MA_QK + h * M_DQK:MA_QK + (h + 1) * M_DQK]

    def v_aug_of(g, h):
        return jnp.concatenate([v_ref[g, :, h * M_DV:(h + 1) * M_DV], ones], axis=1)

    gates = []
    for g in range(n_seq):
        gc = gc_ref[g] * LOG2E
        gr = gc.T[0:2 * M_HEADS, :]
        bcol_all = sum(_dot(tri, p) for p in _split3(gc))
        brow_all = sum(lax.dot_general(p, tri, _NT, preferred_element_type=F32)
                       for p in _split3(gr))
        gates.append((gc, gr, bcol_all, brow_all))

    def qk_product(c):
        return lax.dot_general(q_of(*c), k_of(*c), _NT, preferred_element_type=F32)

    def gate_and_mix(c, qk):
        g, h = c
        gc, gr, bcol_all, brow_all = gates[g]
        c_row = gr[h:h + 1, :] - brow_all[M_HEADS + h:M_HEADS + h + 1, :]
        i_rep = jnp.broadcast_to(gc[:, h:h + 1], (L, 128))
        b_rep = jnp.broadcast_to(bcol_all[:, M_HEADS + h:M_HEADS + h + 1], (L, 128))
        m_prev = m_ref[g, h:h + 1, :]
        a = b_rep + m_prev
        m_rows, s_rows = [], []
        for rb in range(nlt):
            rows = slice(rb * 128, (rb + 1) * 128)
            d_row = [b_rep[rows] + c_row[:, t * 128:(t + 1) * 128] for t in range(rb + 1)]
            d_row[rb] = jnp.where(causal[0:128, 0:128], d_row[rb], NEG)
            m_intra = jnp.max(functools.reduce(jnp.maximum, d_row), axis=1, keepdims=True)
            m_blk = jnp.maximum(a[rows], m_intra)
            tiles = [(qk[rows, t * 128:(t + 1) * 128]
                      * jnp.exp2(d_row[t] - m_blk)).astype(BF16) for t in range(rb + 1)]
            tiles += [jnp.zeros((128, 128), BF16)] * (nlt - 1 - rb)
            m_rows.append(m_blk)
            s_rows.append(jnp.concatenate(tiles, axis=1))
        m_out = jnp.concatenate(m_rows, axis=0)
        s_mat = jnp.concatenate(s_rows, axis=0)
        w_inter = jnp.exp2(a - m_out).astype(BF16)
        ct = ct_ref[g, h]
        lhs = jnp.concatenate([s_mat, w_inter * q_of(g, h)], axis=1)
        rhs = jnp.concatenate([v_aug_of(g, h), ct.astype(BF16)], axis=0)
        return _dot(lhs, rhs), (m_out, b_rep, i_rep, m_prev)

    def finish(c, nd, kept):
        g, h = c
        m_out, b_rep, i_rep, m_prev = kept
        num = nd[:, 0:M_DV]
        den = nd[:, M_DV:M_DV + 128]
        inv = 1.0 / jnp.maximum(jnp.abs(den), jnp.exp2(-m_out))
        ms = jnp.mean(num * num, axis=1, keepdims=True)
        f = inv * lax.rsqrt(inv * inv * ms + EPS)
        hn = num * jnp.concatenate([f, f], axis=1) * hnw_ref[:, h * M_DV:(h + 1) * M_DV]
        y_ref[g, :, h * M_DV:(h + 1) * M_DV] = (og_ref[g, :, h * M_DV:(h + 1) * M_DV]
                                                * hn.astype(BF16))

        b_last = b_rep[L - 1:L, :]
        g_prev = b_last + m_prev
        gl = b_last - b_rep + i_rep
        m_new = jnp.maximum(g_prev, jnp.max(gl, axis=0, keepdims=True))
        wk = jnp.exp2(gl - m_new).astype(BF16)
        decay = jnp.exp2(g_prev - m_new)
        wv = jnp.concatenate([wk] * 3, axis=1) * v_aug_of(g, h)
        ct_ref[g, h] = (jnp.concatenate([decay] * 3, axis=1) * ct_ref[g, h]
                        + lax.dot_general(k_of(g, h), wv, _TN, preferred_element_type=F32))
        m_ref[g, h:h + 1, :] = m_new

    pending = []
    for g in range(n_seq):
        mine = [c for c in chains if c[0] == g]
        qk = [qk_product(c) for c in mine]
        mixed = [gate_and_mix(c, s) for c, s in zip(mine, qk)]
        for c, nd, kept in pending:
            finish(c, nd, kept)
        pending = [(c, nd, kept) for c, (nd, kept) in zip(mine, mixed)]
    for c, nd, kept in pending:
        finish(c, nd, kept)


def _mlstm(qk, v, og, gc, hnw):
    b, seq, _ = qk.shape
    L = MLSTM_L
    G = MLSTM_G
    blk = lambda w: pl.BlockSpec((G, L, w), lambda bi, ci: (bi, ci, 0))
    return pl.pallas_call(
        functools.partial(_mlstm_kernel, L=L),
        grid=(b // G, seq // L),
        in_specs=[
            blk(2 * MA_QK), blk(MA_V), blk(MA_V), blk(128),
            _const_spec(hnw.shape),
        ],
        out_specs=blk(MA_V),
        out_shape=jax.ShapeDtypeStruct((b, seq, MA_V), BF16),
        scratch_shapes=[
            pltpu.VMEM((G, M_HEADS, M_DQK, M_DV + 128), F32),
            pltpu.VMEM((G, 8, 128), F32),
        ],
        compiler_params=pltpu.CompilerParams(
            dimension_semantics=("parallel", "arbitrary"), vmem_limit_bytes=VMEM_LIMIT),
        name="mlstm",
    )(qk, v, og, gc, hnw)


def _attn_kernel(qn_ref, qpe_ref, kn_ref, kpe_ref, v_ref, o_ref,
                 kcat_ref, vaug_ref, qcat_ref, m_ref, acc_ref, sa_ref, sb_ref, *, tq, tk, seq):
    assert tq == 2 * tk
    h = pl.program_id(1)
    n_tiles = seq // tq
    nt = tk // 128

    @pl.when(h == 0)
    def _():
        kcat_ref[:, A_NOPE:2 * A_NOPE] = kpe_ref[...]
        vaug_ref[:, A_DV:2 * A_DV] = jnp.ones((seq, A_DV), BF16)

    kcat_ref[:, 0:A_NOPE] = kn_ref[...]
    vaug_ref[:, 0:A_DV] = v_ref[...]
    lane = lax.broadcasted_iota(jnp.int32, (seq, 128), 1)
    own = ((lane // A_ROPE) == (h % 2)).astype(BF16)
    qcat_ref[:, 0:A_NOPE] = qn_ref[...]
    qcat_ref[:, A_NOPE:2 * A_NOPE] = qpe_ref[...] * own

    def logits(qt, kb, r0=0, nrows=tq):
        q0 = pl.multiple_of(qt * tq + r0, tk)
        k0 = pl.multiple_of(kb * tk, tk)
        return lax.dot_general(qcat_ref[pl.ds(q0, nrows), :], kcat_ref[pl.ds(k0, tk), :],
                               _NT, preferred_element_type=F32)

    def softmax_pv(s, kb, r0, nrows, masked):
        k0 = pl.multiple_of(kb * tk, tk)
        if masked:
            r = lax.broadcasted_iota(jnp.int32, (nrows, tk), 0)
            c = lax.broadcasted_iota(jnp.int32, (nrows, tk), 1)
            s = jnp.where(c <= r, s, NEG)
        tiles = [s[:, t * 128:(t + 1) * 128] for t in range(nt)]
        mx = functools.reduce(jnp.maximum, tiles)
        m_prev = m_ref[r0:r0 + nrows, :]
        m_new = jnp.maximum(m_prev, jnp.max(mx, axis=1, keepdims=True))
        alpha = jnp.exp2(m_prev - m_new)
        p = jnp.concatenate([jnp.exp2(t - m_new).astype(BF16) for t in tiles], axis=1)
        pv = _dot(p, vaug_ref[pl.ds(k0, tk), :])
        acc_ref[r0:r0 + nrows, :] = (jnp.concatenate([alpha, alpha], axis=1)
                                     * acc_ref[r0:r0 + nrows, :] + pv)
        m_ref[r0:r0 + nrows, :] = m_new

    sa_ref[...] = logits(0, 0)

    def tile_body(qt, carry):
        m_ref[...] = jnp.full_like(m_ref, NEG)
        acc_ref[...] = jnp.zeros_like(acc_ref)

        def pair_body(i, c):
            j = 2 * i
            sb_ref[...] = logits(qt, j + 1)
            softmax_pv(sa_ref[...], j, 0, tq, False)
            sa_ref[...] = logits(qt, j + 2)
            softmax_pv(sb_ref[...], j + 1, 0, tq, False)
            return c

        lax.fori_loop(0, qt, pair_body, 0)

        d0 = 2 * qt
        sb_ref[0:tk, :] = logits(qt, d0 + 1, tk, tk)
        softmax_pv(sa_ref[...], d0, 0, tq, True)
        sa_ref[...] = logits(jnp.minimum(qt + 1, n_tiles - 1), 0)
        softmax_pv(sb_ref[0:tk, :], d0 + 1, tk, tk, True)
        o0 = pl.multiple_of(qt * tq, tq)
        o_ref[pl.ds(o0, tq), :] = (acc_ref[:, 0:A_DV] / acc_ref[:, A_DV:2 * A_DV]).astype(BF16)
        return carry

    lax.fori_loop(0, n_tiles, tile_body, 0)


def _attn(qn, qpe, kn, kpe, v):
    b, seq, _ = qn.shape
    tq, tk = ATTN_TQ, ATTN_TK
    return pl.pallas_call(
        functools.partial(_attn_kernel, tq=tq, tk=tk, seq=seq),
        grid=(b, A_HEADS),
        in_specs=[
            pl.BlockSpec((None, seq, A_NOPE), lambda bi, h: (bi, 0, h)),
            pl.BlockSpec((None, seq, 128), lambda bi, h: (bi, 0, h // 2)),
            pl.BlockSpec((None, seq, A_NOPE), lambda bi, h: (bi, 0, h)),
            pl.BlockSpec((None, seq, 128), lambda bi, h: (bi, 0, 0)),
            pl.BlockSpec((None, seq, A_DV), lambda bi, h: (bi, 0, h)),
        ],
        out_specs=pl.BlockSpec((None, seq, A_DV), lambda bi, h: (bi, 0, h)),
        out_shape=jax.ShapeDtypeStruct((b, seq, A_HEADS * A_DV), BF16),
        scratch_shapes=[
            pltpu.VMEM((seq, 2 * A_NOPE), BF16),
            pltpu.VMEM((seq, 2 * A_DV), BF16),
            pltpu.VMEM((seq, 2 * A_NOPE), BF16),
            pltpu.VMEM((tq, 128), F32),
            pltpu.VMEM((tq, 2 * A_DV), F32),
            pltpu.VMEM((tq, tk), F32),
            pltpu.VMEM((tq, tk), F32),
        ],
        compiler_params=pltpu.CompilerParams(
            dimension_semantics=("parallel", "arbitrary"),
            vmem_limit_bytes=VMEM_LIMIT),
        name="attn",
    )(qn, qpe, kn, kpe, v)


def _mlp_kernel(x_ref, ya_ref, yb_ref, ga_ref, gb_ref, mod_ref,
                npost_ref, npre2_ref, npost2_ref, wout_ref, w1_ref, w2_ref,
                o_ref, acc_ref):
    post_mix_gain = mod_ref[2:3, :] * npost_ref[...]
    shift_f = mod_ref[3:4, :]
    pre_mlp_gain = (1.0 + mod_ref[4:5, :]) * npre2_ref[...]
    post_mlp_gain = mod_ref[5:6, :] * npost2_ref[...]

    y = (ga_ref[...].astype(F32) * ya_ref[...].astype(F32)
         + gb_ref[...].astype(F32) * yb_ref[...].astype(F32))
    yo = _dot(y.astype(BF16), wout_ref[...])
    x1 = x_ref[...] + _rms(yo, post_mix_gain)

    h2 = (_rms(x1, pre_mlp_gain) + shift_f).astype(BF16)
    for c in range(D_FF // FF_CHUNK):
        u = jnp.maximum(_dot(h2, w1_ref[:, c * FF_CHUNK:(c + 1) * FF_CHUNK]), 0.0)
        part = _dot((u * u).astype(BF16), w2_ref[c * FF_CHUNK:(c + 1) * FF_CHUNK, :])
        if c == 0:
            acc_ref[...] = part
        else:
            acc_ref[...] += part
    o_ref[...] = x1 + _rms(acc_ref[...], post_mlp_gain)


def _mlp(x2, ya, yb, ga, gb, mod3, npost, npre2, npost2, wout, w1, w2, seq):
    t = x2.shape[0]
    tm = MLP_TM
    tiles_per_seq = seq // tm
    row = pl.BlockSpec((tm, D_MODEL), lambda i: (i, 0))
    return pl.pallas_call(
        _mlp_kernel,
        grid=(t // tm,),
        in_specs=[
            row, row, row, row, row,
            pl.BlockSpec((None, N_MOD, D_MODEL), lambda i: (i // tiles_per_seq, 0, 0)),
            _const_spec((1, D_MODEL)), _const_spec((1, D_MODEL)), _const_spec((1, D_MODEL)),
            _const_spec(wout.shape), _const_spec(w1.shape), _const_spec(w2.shape),
        ],
        out_specs=row,
        out_shape=jax.ShapeDtypeStruct((t, D_MODEL), F32),
        scratch_shapes=[pltpu.VMEM((tm, D_MODEL), F32)],
        compiler_params=pltpu.CompilerParams(
            dimension_semantics=("arbitrary",), vmem_limit_bytes=VMEM_LIMIT),
        name="mlp",
    )(x2, ya, yb, ga, gb, mod3, npost, npre2, npost2, wout, w1, w2)


def _rot_half_cols(w):
    half = w.shape[-1] // 2
    return jnp.concatenate([-w[..., half:], w[..., :half]], axis=-1)


def _regroup_kernel(w_ref, o_ref):
    sizes = (MA_QK, MA_QK, MA_V, MA_V, M_HEADS, M_HEADS, A_QRANK, A_KVRANK, A_ROPE,
             D_MODEL, D_MODEL)
    offs = [sum(sizes[:n]) for n in range(len(sizes))]
    o_q, _, _, _, o_i, _, o_cq, _, o_kpe, o_ga, _ = offs
    half = A_ROPE // 2
    cursor = [0]

    def put(src, n, negate=False):
        blk = w_ref[src:src + n, :]
        o_ref[cursor[0]:cursor[0] + n, :] = (-blk if negate else blk).astype(BF16)
        cursor[0] += n

    put(o_q, 2 * MA_QK + 2 * MA_V)
    put(o_ga, 2 * D_MODEL)
    put(o_cq, A_QRANK + A_KVRANK)
    put(o_kpe, A_ROPE)
    put(o_kpe, A_ROPE)
    for _ in range(2):
        put(o_kpe + half, half, negate=True)
        put(o_kpe, half)
    gate_rows = jnp.concatenate(
        [w_ref[o_i:o_i + 2 * M_HEADS, :],
         jnp.zeros((128 - 2 * M_HEADS, w_ref.shape[1]), F32)], axis=0)
    o_ref[cursor[0]:cursor[0] + 128, :] = gate_rows.astype(BF16)


def _prep_in_weights(w_in_t):
    n_in, k = w_in_t.shape
    n_out = 6 * D_MODEL
    return pl.pallas_call(
        _regroup_kernel,
        grid=(k // REGROUP_TK,),
        in_specs=[pl.BlockSpec((n_in, REGROUP_TK), lambda i: (0, i))],
        out_specs=pl.BlockSpec((n_out, REGROUP_TK), lambda i: (0, i)),
        out_shape=jax.ShapeDtypeStruct((n_out, k), BF16),
        compiler_params=pltpu.CompilerParams(
            dimension_semantics=("arbitrary",), vmem_limit_bytes=VMEM_LIMIT),
        name="regroup",
    )(w_in_t)


def _prep_mla_weights(w_uq, w_ukv):
    r = w_uq.reshape(A_QRANK, A_HEADS, A_NOPE + A_ROPE)
    nope = r[:, :, :A_NOPE].reshape(A_QRANK, A_HEADS * A_NOPE)
    pe = r[:, :, A_NOPE:]
    pe_rot = _rot_half_cols(pe).reshape(A_QRANK, A_HEADS * A_ROPE)
    pe = pe.reshape(A_QRANK, A_HEADS * A_ROPE)
    wuq = jnp.concatenate([nope, pe, pe_rot], axis=1).astype(BF16)
    r = w_ukv.reshape(A_KVRANK, A_HEADS, A_NOPE + A_DV)
    wukv = jnp.concatenate([r[:, :, :A_NOPE].reshape(A_KVRANK, -1),
                            r[:, :, A_NOPE:].reshape(A_KVRANK, -1)], axis=1).astype(BF16)
    return wuq, wukv


def _rope_tables(seq):
    half = A_ROPE // 2
    inv_freq = ROPE_THETA ** (-np.arange(half, dtype=np.float64) / half)
    ang = np.arange(seq, dtype=np.float64)[:, None] * inv_freq[None, :]
    reps = 128 // half
    return (jnp.asarray(np.tile(np.cos(ang), (1, reps)), F32),
            jnp.asarray(np.tile(np.sin(ang), (1, reps)), F32))


def kernel(x, c, w_ada, b_ada, norm_pre_mix, norm_post_mix, norm_pre_mlp, norm_post_mlp,
           w_in, mlstm_conv_w, mlstm_conv_b, mlstm_gate_b, mlstm_head_norm,
           mla_q_norm, mla_kv_norm, w_uq, w_ukv, w_out, w_ff1, w_ff2):
    bsz, seq, d = x.shape
    depth = w_ada.shape[0]
    cos, sin = _rope_tables(seq)
    c8 = jnp.pad(c, ((0, 8 - bsz), (0, 0)))
    qks = jnp.concatenate([jnp.ones((1, MA_QK), F32),
                           jnp.full((1, MA_QK), M_DQK ** -0.5, F32)], axis=1)
    row = lambda a: a.reshape(1, -1)
    x2 = x.reshape(bsz * seq, d)
    for l in range(depth):
        mod = _ada(c8, w_ada[l], row(b_ada[l]))[:bsz]
        mod3 = mod.reshape(bsz, N_MOD, d)
        win = _prep_in_weights(jnp.transpose(w_in[l]))
        wuq, wukv = _prep_mla_weights(w_uq[l], w_ukv[l])
        gbias = jnp.pad(row(mlstm_gate_b[l]), ((0, 0), (0, 128 - 2 * M_HEADS)))
        (qk, v_a, og, ga, gb, gates, qnope, qpe, knope, v_b, kpe) = _proj(
            x2, mod3, row(norm_pre_mix[l]), win, mlstm_conv_w[l], row(mlstm_conv_b[l]),
            qks, gbias, row(mla_q_norm[l]), row(mla_kv_norm[l]), wuq, wukv, cos, sin, seq)
        b3 = lambda a: a.reshape(bsz, seq, a.shape[-1])
        y_a = _mlstm(b3(qk), b3(v_a), b3(og), b3(gates), row(mlstm_head_norm[l]))
        y_b = _attn(b3(qnope), b3(qpe), b3(knope), b3(kpe), b3(v_b))
        x2 = _mlp(x2, y_a.reshape(bsz * seq, d), y_b.reshape(bsz * seq, d), ga, gb, mod3,
                  row(norm_post_mix[l]), row(norm_pre_mlp[l]), row(norm_post_mlp[l]),
                  w_out[l].astype(BF16), w_ff1[l].astype(BF16), w_ff2[l].astype(BF16), seq)
    return x2.reshape(bsz, seq, d)
```

```python
import functools

import jax
import jax.numpy as jnp
import numpy as np
from jax import lax
from jax.experimental import pallas as pl
from jax.experimental.pallas import tpu as pltpu

F32 = jnp.float32
BF16 = jnp.bfloat16

D_MODEL = 1024
M_HEADS = 4
M_DQK = 128
M_DV = 256
CONV_W = 4
A_HEADS = 8
A_NOPE = 128
A_ROPE = 64
A_DV = 128
A_QRANK = 384
A_KVRANK = 256
ROPE_THETA = 10000.0
D_FF = 4096
EPS = 1e-6
N_MOD = 6
MA_QK = M_HEADS * M_DQK
MA_V = M_HEADS * M_DV

PROJ_TM = 512
HALO = 16
MLSTM_L = 256
MLSTM_G = 4
ATTN_TQ = 1024
ATTN_TK = 512
MLP_TM = 512
FF_CHUNK = 1024
ADA_TN = 1536
REGROUP_TK = 256
VMEM_LIMIT = 56 * 1024 * 1024

NEG = -1e30
LOG2E = 1.4426950408889634

_NT = (((1,), (1,)), ((), ()))
_TN = (((0,), (0,)), ((), ()))


def _dot(a, b):
    return jnp.dot(a, b, preferred_element_type=F32)


def _sigmoid(x):
    return jax.nn.sigmoid(x)


def _rms(x, w):
    return x * lax.rsqrt(jnp.mean(x * x, axis=-1, keepdims=True) + EPS) * w


def _const_spec(shape):
    nd = len(shape)
    return pl.BlockSpec(shape, lambda *_: (0,) * nd, pipeline_mode=pl.Buffered(1))


def _ada_kernel(c_ref, w_ref, b_ref, o_ref):
    c = c_ref[...]
    a = (c * _sigmoid(c)).astype(BF16)
    o_ref[...] = _dot(a, w_ref[...].astype(BF16)) + b_ref[...]


def _ada(c8, w_ada, b_ada):
    n = w_ada.shape[1]
    return pl.pallas_call(
        _ada_kernel,
        grid=(n // ADA_TN,),
        in_specs=[
            pl.BlockSpec((8, D_MODEL), lambda j: (0, 0)),
            pl.BlockSpec((D_MODEL, ADA_TN), lambda j: (0, j)),
            pl.BlockSpec((1, ADA_TN), lambda j: (0, j)),
        ],
        out_specs=pl.BlockSpec((8, ADA_TN), lambda j: (0, j)),
        out_shape=jax.ShapeDtypeStruct((8, n), F32),
        compiler_params=pltpu.CompilerParams(
            dimension_semantics=("arbitrary",), vmem_limit_bytes=VMEM_LIMIT),
        name="ada",
    )(c8, w_ada, b_ada)


def _proj_kernel(x_ref, xh_ref, mod_ref, npre_ref,
                 win_ref,
                 convw_ref, convb_ref, qks_ref, gbias_ref, qn_ref, kvn_ref,
                 wuq_ref, wukv_ref, cos_ref, sin_ref,
                 qk_out, v_out, og_out, ga_out, gb_out, gates_out,
                 qnope_out, qpe_out, knope_out, vb_out, kpe_out,
                 hext_ref, z_ref, *, tm, tiles_per_seq, q_scale):
    i = pl.program_id(0)
    first = (i % tiles_per_seq) == 0
    wqk_ref, wv_ref, wo_ref, wga_ref, wgb_ref, ws_ref = (
        win_ref.at[n * D_MODEL:(n + 1) * D_MODEL, :] for n in range(6))

    def proj(a, wt_ref):
        return lax.dot_general(a, wt_ref[...], _NT, preferred_element_type=F32)

    shift = mod_ref[0:1, :]
    w_scaled = npre_ref[...] * (1.0 + mod_ref[1:2, :])

    def prenorm(xv):
        return _rms(xv, w_scaled) + shift

    hext_ref[HALO:, :] = prenorm(x_ref[...]).astype(BF16)
    hh = prenorm(xh_ref[...])
    hext_ref[0:HALO, :] = jnp.where(first, 0.0, hh).astype(BF16)

    z_ref[...] = proj(hext_ref[...], wqk_ref)
    zwin = z_ref[HALO - 8:, :]
    n_win = tm + 8
    acc = convb_ref[...]
    for j in range(CONV_W):
        lead = 8 - (CONV_W - 1) + j
        tap = zwin if lead == 0 else pltpu.roll(zwin, n_win - lead, 0)
        acc = acc + convw_ref[j:j + 1, :] * tap[0:tm, :]
    qk_out[...] = (acc * _sigmoid(acc) * qks_ref[...]).astype(BF16)

    h = hext_ref[HALO:, :]
    v_out[...] = proj(h, wv_ref).astype(BF16)
    og_out[...] = _sigmoid(proj(h, wo_ref)).astype(BF16)
    ga_out[...] = _sigmoid(proj(h, wga_ref)).astype(BF16)
    gb_out[...] = _sigmoid(proj(h, wgb_ref)).astype(BF16)

    s = proj(h, ws_ref)
    c_q = s[:, 0:A_QRANK]
    c_kv = s[:, A_QRANK:A_QRANK + A_KVRANK]
    kp = s[:, 640:768]
    kpr = s[:, 768:896]
    g = s[:, 896:1024] + gbias_ref[...]

    lane = lax.broadcasted_iota(jnp.int32, g.shape, 1)
    logsig = jnp.minimum(g, 0.0) - jnp.log1p(jnp.exp(-jnp.abs(g)))
    gates_out[...] = jnp.where(lane < M_HEADS, g, logsig)

    cos = cos_ref[...]
    sin = sin_ref[...]
    kpe_out[...] = (kp * cos + kpr * sin).astype(BF16)

    q = _dot(_rms(c_q, qn_ref[...]).astype(BF16), wuq_ref[...])
    qnope_out[...] = (q[:, 0:1024] * q_scale).astype(BF16)
    cos4 = jnp.concatenate([cos] * 4, axis=1)
    sin4 = jnp.concatenate([sin] * 4, axis=1)
    qpe = q[:, 1024:1536] * cos4 + q[:, 1536:2048] * sin4
    qpe_out[...] = (qpe * q_scale).astype(BF16)

    kv = _dot(_rms(c_kv, kvn_ref[...]).astype(BF16), wukv_ref[...])
    knope_out[...] = kv[:, 0:1024].astype(BF16)
    vb_out[...] = kv[:, 1024:2048].astype(BF16)


def _proj(x2, mod3, npre, win, convw, convb, qks, gbias, qn, kvn, wuq, wukv,
          cos, sin, seq):
    t = x2.shape[0]
    tm = PROJ_TM
    tiles_per_seq = seq // tm
    hb = tm // HALO
    row = lambda i: (i, 0)
    out_w = [1024, 1024, 1024, 1024, 1024, 128, 1024, 512, 1024, 1024, 128]
    out_dt = [BF16] * 5 + [F32] + [BF16] * 5
    kern = functools.partial(
        _proj_kernel, tm=tm, tiles_per_seq=tiles_per_seq,
        q_scale=float((A_NOPE + A_ROPE) ** -0.5 * LOG2E))
    return pl.pallas_call(
        kern,
        grid=(t // tm,),
        in_specs=[
            pl.BlockSpec((tm, D_MODEL), row),
            pl.BlockSpec((HALO, D_MODEL), lambda i: (jnp.maximum(i * hb - 1, 0), 0)),
            pl.BlockSpec((None, N_MOD, D_MODEL), lambda i: (i // tiles_per_seq, 0, 0)),
            _const_spec((1, D_MODEL)),
            _const_spec(win.shape),
            _const_spec(convw.shape), _const_spec(convb.shape), _const_spec(qks.shape),
            _const_spec(gbias.shape), _const_spec(qn.shape), _const_spec(kvn.shape),
            _const_spec(wuq.shape), _const_spec(wukv.shape),
            pl.BlockSpec((tm, 128), lambda i: (i % tiles_per_seq, 0)),
            pl.BlockSpec((tm, 128), lambda i: (i % tiles_per_seq, 0)),
        ],
        out_specs=[pl.BlockSpec((tm, n), row) for n in out_w],
        out_shape=[jax.ShapeDtypeStruct((t, n), d) for n, d in zip(out_w, out_dt)],
        scratch_shapes=[
            pltpu.VMEM((tm + HALO, D_MODEL), BF16),
            pltpu.VMEM((tm + HALO, 2 * MA_QK), F32),
        ],
        compiler_params=pltpu.CompilerParams(
            dimension_semantics=("arbitrary",), vmem_limit_bytes=VMEM_LIMIT),
        name="proj",
    )(x2, x2, mod3, npre, win, convw, convb, qks, gbias,
      qn, kvn, wuq, wukv, cos, sin)


def _split3(x):
    hi = x.astype(BF16)
    r = x - hi.astype(F32)
    mid = r.astype(BF16)
    lo = (r - mid.astype(F32)).astype(BF16)
    return hi, mid, lo


def _mlstm_kernel(qk_ref, v_ref, og_ref, gc_ref, hnw_ref, y_ref,
                  ct_ref, m_ref, *, L):
    @pl.when(pl.program_id(1) == 0)
    def _():
        ct_ref[...] = jnp.zeros_like(ct_ref)
        m_ref[...] = jnp.zeros_like(m_ref)

    n_seq = qk_ref.shape[0]
    nlt = L // 128
    row = lax.broadcasted_iota(jnp.int32, (L, L), 0)
    col = lax.broadcasted_iota(jnp.int32, (L, L), 1)
    causal = col <= row
    tri = causal.astype(BF16)
    ones = jnp.ones((L, 128), BF16)
    chains = [(g, h) for g in range(n_seq) for h in range(M_HEADS)]

    def q_of(g, h):
        return qk_ref[g, :, h * M_DQK:(h + 1) * M_DQK]

    def k_of(g, h):
        return qk_ref[g, :, MA_QK + h * M_DQK:MA_QK + (h + 1) * M_DQK]

    def v_aug_of(g, h):
        return jnp.concatenate([v_ref[g, :, h * M_DV:(h + 1) * M_DV], ones], axis=1)

    gates = []
    for g in range(n_seq):
        gc = gc_ref[g] * LOG2E
        gr = gc.T[0:2 * M_HEADS, :]
        bcol_all = sum(_dot(tri, p) for p in _split3(gc))
        brow_all = sum(lax.dot_general(p, tri, _NT, preferred_element_type=F32)
                       for p in _split3(gr))
        gates.append((gc, gr, bcol_all, brow_all))

    def qk_product(c):
        return lax.dot_general(q_of(*c), k_of(*c), _NT, preferred_element_type=F32)

    def gate_and_mix(c, qk):
        g, h = c
        gc, gr, bcol_all, brow_all = gates[g]
        c_row = gr[h:h + 1, :] - brow_all[M_HEADS + h:M_HEADS + h + 1, :]
        i_rep = jnp.broadcast_to(gc[:, h:h + 1], (L, 128))
        b_rep = jnp.broadcast_to(bcol_all[:, M_HEADS + h:M_HEADS + h + 1], (L, 128))
        m_prev = m_ref[g, h:h + 1, :]
        a = b_rep + m_prev
        m_rows, s_rows = [], []
        for rb in range(nlt):
            rows = slice(rb * 128, (rb + 1) * 128)
            d_row = [b_rep[rows] + c_row[:, t * 128:(t + 1) * 128] for t in range(rb + 1)]
            d_row[rb] = jnp.where(causal[0:128, 0:128], d_row[rb], NEG)
            m_intra = jnp.max(functools.reduce(jnp.maximum, d_row), axis=1, keepdims=True)
            m_blk = jnp.maximum(a[rows], m_intra)
            tiles = [(qk[rows, t * 128:(t + 1) * 128]
                      * jnp.exp2(d_row[t] - m_blk)).astype(BF16) for t in range(rb + 1)]
            tiles += [jnp.zeros((128, 128), BF16)] * (nlt - 1 - rb)
            m_rows.append(m_blk)
            s_rows.append(jnp.concatenate(tiles, axis=1))
        m_out = jnp.concatenate(m_rows, axis=0)
        s_mat = jnp.concatenate(s_rows, axis=0)
        w_inter = jnp.exp2(a - m_out).astype(BF16)
        ct = ct_ref[g, h]
        lhs = jnp.concatenate([s_mat, w_inter * q_of(g, h)], axis=1)
        rhs = jnp.concatenate([v_aug_of(g, h), ct.astype(BF16)], axis=0)
        return _dot(lhs, rhs), (m_out, b_rep, i_rep, m_prev)

    def finish(c, nd, kept):
        g, h = c
        m_out, b_rep, i_rep, m_prev = kept
        num = nd[:, 0:M_DV]
        den = nd[:, M_DV:M_DV + 128]
        inv = 1.0 / jnp.maximum(jnp.abs(den), jnp.exp2(-m_out))
        ms = jnp.mean(num * num, axis=1, keepdims=True)
        f = inv * lax.rsqrt(inv * inv * ms + EPS)
        hn = num * jnp.concatenate([f, f], axis=1) * hnw_ref[:, h * M_DV:(h + 1) * M_DV]
        y_ref[g, :, h * M_DV:(h + 1) * M_DV] = (og_ref[g, :, h * M_DV:(h + 1) * M_DV]
                                                * hn.astype(BF16))

        b_last = b_rep[L - 1:L, :]
        g_prev = b_last + m_prev
        gl = b_last - b_rep + i_rep
        m_new = jnp.maximum(g_prev, jnp.max(gl, axis=0, keepdims=True))
        wk = jnp.exp2(gl - m_new).astype(BF16)
        decay = jnp.exp2(g_prev - m_new)
        wv = jnp.concatenate([wk] * 3, axis=1) * v_aug_of(g, h)
        ct_ref[g, h] = (jnp.concatenate([decay] * 3, axis=1) * ct_ref[g, h]
                        + lax.dot_general(k_of(g, h), wv, _TN, preferred_element_type=F32))
        m_ref[g, h:h + 1, :] = m_new

    pending = []
    for g in range(n_seq):
        mine = [c for c in chains if c[0] == g]
        qk = [qk_product(c) for c in mine]
        mixed = [gate_and_mix(c, s) for c, s in zip(mine, qk)]
        for c, nd, kept in pending:
            finish(c, nd, kept)
        pending = [(c, nd, kept) for c, (nd, kept) in zip(mine, mixed)]
    for c, nd, kept in pending:
        finish(c, nd, kept)


def _mlstm(qk, v, og, gc, hnw):
    b, seq, _ = qk.shape
    L = MLSTM_L
    G = MLSTM_G
    blk = lambda w: pl.BlockSpec((G, L, w), lambda bi, ci: (bi, ci, 0))
    return pl.pallas_call(
        functools.partial(_mlstm_kernel, L=L),
        grid=(b // G, seq // L),
        in_specs=[
            blk(2 * MA_QK), blk(MA_V), blk(MA_V), blk(128),
            _const_spec(hnw.shape),
        ],
        out_specs=blk(MA_V),
        out_shape=jax.ShapeDtypeStruct((b, seq, MA_V), BF16),
        scratch_shapes=[
            pltpu.VMEM((G, M_HEADS, M_DQK, M_DV + 128), F32),
            pltpu.VMEM((G, 8, 128), F32),
        ],
        compiler_params=pltpu.CompilerParams(
            dimension_semantics=("parallel", "arbitrary"), vmem_limit_bytes=VMEM_LIMIT),
        name="mlstm",
    )(qk, v, og, gc, hnw)


def _attn_kernel(qn_ref, qpe_ref, kn_ref, kpe_ref, v_ref, o_ref,
                 kcat_ref, vaug_ref, qcat_ref, m_ref, acc_ref, sa_ref, sb_ref, *, tq, tk, seq):
    assert tq == 2 * tk
    h = pl.program_id(1)
    n_tiles = seq // tq
    nt = tk // 128

    @pl.when(h == 0)
    def _():
        kcat_ref[:, A_NOPE:2 * A_NOPE] = kpe_ref[...]
        vaug_ref[:, A_DV:2 * A_DV] = jnp.ones((seq, A_DV), BF16)

    kcat_ref[:, 0:A_NOPE] = kn_ref[...]
    vaug_ref[:, 0:A_DV] = v_ref[...]
    lane = lax.broadcasted_iota(jnp.int32, (seq, 128), 1)
    own = ((lane // A_ROPE) == (h % 2)).astype(BF16)
    qcat_ref[:, 0:A_NOPE] = qn_ref[...]
    qcat_ref[:, A_NOPE:2 * A_NOPE] = qpe_ref[...] * own

    def logits(qt, kb, r0=0, nrows=tq):
        q0 = pl.multiple_of(qt * tq + r0, tk)
        k0 = pl.multiple_of(kb * tk, tk)
        return lax.dot_general(qcat_ref[pl.ds(q0, nrows), :], kcat_ref[pl.ds(k0, tk), :],
                               _NT, preferred_element_type=F32)

    def softmax_pv(s, kb, r0, nrows, masked):
        k0 = pl.multiple_of(kb * tk, tk)
        if masked:
            r = lax.broadcasted_iota(jnp.int32, (nrows, tk), 0)
            c = lax.broadcasted_iota(jnp.int32, (nrows, tk), 1)
            s = jnp.where(c <= r, s, NEG)
        tiles = [s[:, t * 128:(t + 1) * 128] for t in range(nt)]
        mx = functools.reduce(jnp.maximum, tiles)
        m_prev = m_ref[r0:r0 + nrows, :]
        m_new = jnp.maximum(m_prev, jnp.max(mx, axis=1, keepdims=True))
        alpha = jnp.exp2(m_prev - m_new)
        p = jnp.concatenate([jnp.exp2(t - m_new).astype(BF16) for t in tiles], axis=1)
        pv = _dot(p, vaug_ref[pl.ds(k0, tk), :])
        acc_ref[r0:r0 + nrows, :] = (jnp.concatenate([alpha, alpha], axis=1)
                                     * acc_ref[r0:r0 + nrows, :] + pv)
        m_ref[r0:r0 + nrows, :] = m_new

    sa_ref[...] = logits(0, 0)

    def tile_body(qt, carry):
        m_ref[...] = jnp.full_like(m_ref, NEG)
        acc_ref[...] = jnp.zeros_like(acc_ref)

        def pair_body(i, c):
            j = 2 * i
            sb_ref[...] = logits(qt, j + 1)
            softmax_pv(sa_ref[...], j, 0, tq, False)
            sa_ref[...] = logits(qt, j + 2)
            softmax_pv(sb_ref[...], j + 1, 0, tq, False)
            return c

        lax.fori_loop(0, qt, pair_body, 0)

        d0 = 2 * qt
        sb_ref[0:tk, :] = logits(qt, d0 + 1, tk, tk)
        softmax_pv(sa_ref[...], d0, 0, tq, True)
        sa_ref[...] = logits(jnp.minimum(qt + 1, n_tiles - 1), 0)
        softmax_pv(sb_ref[0:tk, :], d0 + 1, tk, tk, True)
        o0 = pl.multiple_of(qt * tq, tq)
        o_ref[pl.ds(o0, tq), :] = (acc_ref[:, 0:A_DV] / acc_ref[:, A_DV:2 * A_DV]).astype(BF16)
        return carry

    lax.fori_loop(0, n_tiles, tile_body, 0)


def _attn(qn, qpe, kn, kpe, v):
    b, seq, _ = qn.shape
    tq, tk = ATTN_TQ, ATTN_TK
    return pl.pallas_call(
        functools.partial(_attn_kernel, tq=tq, tk=tk, seq=seq),
        grid=(b, A_HEADS),
        in_specs=[
            pl.BlockSpec((None, seq, A_NOPE), lambda bi, h: (bi, 0, h)),
            pl.BlockSpec((None, seq, 128), lambda bi, h: (bi, 0, h // 2)),
            pl.BlockSpec((None, seq, A_NOPE), lambda bi, h: (bi, 0, h)),
            pl.BlockSpec((None, seq, 128), lambda bi, h: (bi, 0, 0)),
            pl.BlockSpec((None, seq, A_DV), lambda bi, h: (bi, 0, h)),
        ],
        out_specs=pl.BlockSpec((None, seq, A_DV), lambda bi, h: (bi, 0, h)),
        out_shape=jax.ShapeDtypeStruct((b, seq, A_HEADS * A_DV), BF16),
        scratch_shapes=[
            pltpu.VMEM((seq, 2 * A_NOPE), BF16),
            pltpu.VMEM((seq, 2 * A_DV), BF16),
            pltpu.VMEM((seq, 2 * A_NOPE), BF16),
            pltpu.VMEM((tq, 128), F32),
            pltpu.VMEM((tq, 2 * A_DV), F32),
            pltpu.VMEM((tq, tk), F32),
            pltpu.VMEM((tq, tk), F32),
        ],
        compiler_params=pltpu.CompilerParams(
            dimension_semantics=("parallel", "arbitrary"),
            vmem_limit_bytes=VMEM_LIMIT),
        name="attn",
    )(qn, qpe, kn, kpe, v)


def _mlp_kernel(x_ref, ya_ref, yb_ref, ga_ref, gb_ref, mod_ref,
                npost_ref, npre2_ref, npost2_ref, wout_ref, w1_ref, w2_ref,
                o_ref, acc_ref):
    post_mix_gain = mod_ref[2:3, :] * npost_ref[...]
    shift_f = mod_ref[3:4, :]
    pre_mlp_gain = (1.0 + mod_ref[4:5, :]) * npre2_ref[...]
    post_mlp_gain = mod_ref[5:6, :] * npost2_ref[...]

    y = (ga_ref[...].astype(F32) * ya_ref[...].astype(F32)
         + gb_ref[...].astype(F32) * yb_ref[...].astype(F32))
    yo = _dot(y.astype(BF16), wout_ref[...])
    x1 = x_ref[...] + _rms(yo, post_mix_gain)

    h2 = (_rms(x1, pre_mlp_gain) + shift_f).astype(BF16)
    for c in range(D_FF // FF_CHUNK):
        u = jnp.maximum(_dot(h2, w1_ref[:, c * FF_CHUNK:(c + 1) * FF_CHUNK]), 0.0)
        part = _dot((u * u).astype(BF16), w2_ref[c * FF_CHUNK:(c + 1) * FF_CHUNK, :])
        if c == 0:
            acc_ref[...] = part
        else:
            acc_ref[...] += part
    o_ref[...] = x1 + _rms(acc_ref[...], post_mlp_gain)


def _mlp(x2, ya, yb, ga, gb, mod3, npost, npre2, npost2, wout, w1, w2, seq):
    t = x2.shape[0]
    tm = MLP_TM
    tiles_per_seq = seq // tm
    row = pl.BlockSpec((tm, D_MODEL), lambda i: (i, 0))
    return pl.pallas_call(
        _mlp_kernel,
        grid=(t // tm,),
        in_specs=[
            row, row, row, row, row,
            pl.BlockSpec((None, N_MOD, D_MODEL), lambda i: (i // tiles_per_seq, 0, 0)),
            _const_spec((1, D_MODEL)), _const_spec((1, D_MODEL)), _const_spec((1, D_MODEL)),
            _const_spec(wout.shape), _const_spec(w1.shape), _const_spec(w2.shape),
        ],
        out_specs=row,
        out_shape=jax.ShapeDtypeStruct((t, D_MODEL), F32),
        scratch_shapes=[pltpu.VMEM((tm, D_MODEL), F32)],
        compiler_params=pltpu.CompilerParams(
            dimension_semantics=("arbitrary",), vmem_limit_bytes=VMEM_LIMIT),
        name="mlp",
    )(x2, ya, yb, ga, gb, mod3, npost, npre2, npost2, wout, w1, w2)


def _rot_half_cols(w):
    half = w.shape[-1] // 2
    return jnp.concatenate([-w[..., half:], w[..., :half]], axis=-1)


def _regroup_kernel(w_ref, o_ref):
    sizes = (MA_QK, MA_QK, MA_V, MA_V, M_HEADS, M_HEADS, A_QRANK, A_KVRANK, A_ROPE,
             D_MODEL, D_MODEL)
    offs = [sum(sizes[:n]) for n in range(len(sizes))]
    o_q, _, _, _, o_i, _, o_cq, _, o_kpe, o_ga, _ = offs
    half = A_ROPE // 2
    cursor = [0]

    def put(src, n, negate=False):
        blk = w_ref[src:src + n, :]
        o_ref[cursor[0]:cursor[0] + n, :] = (-blk if negate else blk).astype(BF16)
        cursor[0] += n

    put(o_q, 2 * MA_QK + 2 * MA_V)
    put(o_ga, 2 * D_MODEL)
    put(o_cq, A_QRANK + A_KVRANK)
    put(o_kpe, A_ROPE)
    put(o_kpe, A_ROPE)
    for _ in range(2):
        put(o_kpe + half, half, negate=True)
        put(o_kpe, half)
    gate_rows = jnp.concatenate(
        [w_ref[o_i:o_i + 2 * M_HEADS, :],
         jnp.zeros((128 - 2 * M_HEADS, w_ref.shape[1]), F32)], axis=0)
    o_ref[cursor[0]:cursor[0] + 128, :] = gate_rows.astype(BF16)


def _prep_in_weights(w_in_t):
    n_in, k = w_in_t.shape
    n_out = 6 * D_MODEL
    return pl.pallas_call(
        _regroup_kernel,
        grid=(k // REGROUP_TK,),
        in_specs=[pl.BlockSpec((n_in, REGROUP_TK), lambda i: (0, i))],
        out_specs=pl.BlockSpec((n_out, REGROUP_TK), lambda i: (0, i)),
        out_shape=jax.ShapeDtypeStruct((n_out, k), BF16),
        compiler_params=pltpu.CompilerParams(
            dimension_semantics=("arbitrary",), vmem_limit_bytes=VMEM_LIMIT),
        name="regroup",
    )(w_in_t)


def _prep_mla_weights(w_uq, w_ukv):
    r = w_uq.reshape(A_QRANK, A_HEADS, A_NOPE + A_ROPE)
    nope = r[:, :, :A_NOPE].reshape(A_QRANK, A_HEADS * A_NOPE)
    pe = r[:, :, A_NOPE:]
    pe_rot = _rot_half_cols(pe).reshape(A_QRANK, A_HEADS * A_ROPE)
    pe = pe.reshape(A_QRANK, A_HEADS * A_ROPE)
    wuq = jnp.concatenate([nope, pe, pe_rot], axis=1).astype(BF16)
    r = w_ukv.reshape(A_KVRANK, A_HEADS, A_NOPE + A_DV)
    wukv = jnp.concatenate([r[:, :, :A_NOPE].reshape(A_KVRANK, -1),
                            r[:, :, A_NOPE:].reshape(A_KVRANK, -1)], axis=1).astype(BF16)
    return wuq, wukv


def _rope_tables(seq):
    half = A_ROPE // 2
    inv_freq = ROPE_THETA ** (-np.arange(half, dtype=np.float64) / half)
    ang = np.arange(seq, dtype=np.float64)[:, None] * inv_freq[None, :]
    reps = 128 // half
    return (jnp.asarray(np.tile(np.cos(ang), (1, reps)), F32),
            jnp.asarray(np.tile(np.sin(ang), (1, reps)), F32))


def kernel(x, c, w_ada, b_ada, norm_pre_mix, norm_post_mix, norm_pre_mlp, norm_post_mlp,
           w_in, mlstm_conv_w, mlstm_conv_b, mlstm_gate_b, mlstm_head_norm,
           mla_q_norm, mla_kv_norm, w_uq, w_ukv, w_out, w_ff1, w_ff2):
    bsz, seq, d = x.shape
    depth = w_ada.shape[0]
    cos, sin = _rope_tables(seq)
    c8 = jnp.pad(c, ((0, 8 - bsz), (0, 0)))
    qks = jnp.concatenate([jnp.ones((1, MA_QK), F32),
                           jnp.full((1, MA_QK), M_DQK ** -0.5, F32)], axis=1)
    row = lambda a: a.reshape(1, -1)
    x2 = x.reshape(bsz * seq, d)
    for l in range(depth):
        mod = _ada(c8, w_ada[l], row(b_ada[l]))[:bsz]
        mod3 = mod.reshape(bsz, N_MOD, d)
        win = _prep_in_weights(jnp.transpose(w_in[l]))
        wuq, wukv = _prep_mla_weights(w_uq[l], w_ukv[l])
        gbias = jnp.pad(row(mlstm_gate_b[l]), ((0, 0), (0, 128 - 2 * M_HEADS)))
        (qk, v_a, og, ga, gb, gates, qnope, qpe, knope, v_b, kpe) = _proj(
            x2, mod3, row(norm_pre_mix[l]), win, mlstm_conv_w[l], row(mlstm_conv_b[l]),
            qks, gbias, row(mla_q_norm[l]), row(mla_kv_norm[l]), wuq, wukv, cos, sin, seq)
        b3 = lambda a: a.reshape(bsz, seq, a.shape[-1])
        y_a = _mlstm(b3(qk), b3(v_a), b3(og), b3(gates), row(mlstm_head_norm[l]))
        y_b = _attn(b3(qnope), b3(qpe), b3(knope), b3(kpe), b3(v_b))
        x2 = _mlp(x2, y_a.reshape(bsz * seq, d), y_b.reshape(bsz * seq, d), ga, gb, mod3,
                  row(norm_post_mix[l]), row(norm_pre_mlp[l]), row(norm_post_mlp[l]),
                  w_out[l].astype(BF16), w_ff1[l].astype(BF16), w_ff2[l].astype(BF16), seq)
    return x2.reshape(bsz, seq, d)
```

```python
import functools

import jax
import jax.numpy as jnp
import numpy as np
from jax import lax
from jax.experimental import pallas as pl
from jax.experimental.pallas import tpu as pltpu

F32 = jnp.float32
BF16 = jnp.bfloat16

D_MODEL = 1024
M_HEADS = 4
M_DQK = 128
M_DV = 256
CONV_W = 4
A_HEADS = 8
A_NOPE = 128
A_ROPE = 64
A_DV = 128
A_QRANK = 384
A_KVRANK = 256
ROPE_THETA = 10000.0
D_FF = 4096
EPS = 1e-6
N_MOD = 6
MA_QK = M_HEADS * M_DQK
MA_V = M_HEADS * M_DV

PROJ_TM = 512
HALO = 16
MLSTM_L = 256
MLSTM_G = 4
ATTN_TQ = 1024
ATTN_TK = 512
MLP_TM = 512
FF_CHUNK = 1024
ADA_TN = 1536
REGROUP_TK = 256
VMEM_LIMIT = 56 * 1024 * 1024

NEG = -1e30
LOG2E = 1.4426950408889634

_NT = (((1,), (1,)), ((), ()))
_TN = (((0,), (0,)), ((), ()))


def _dot(a, b):
    return jnp.dot(a, b, preferred_element_type=F32)


def _sigmoid(x):
    return jax.nn.sigmoid(x)


def _rms(x, w):
    return x * lax.rsqrt(jnp.mean(x * x, axis=-1, keepdims=True) + EPS) * w


def _const_spec(shape):
    nd = len(shape)
    return pl.BlockSpec(shape, lambda *_: (0,) * nd, pipeline_mode=pl.Buffered(1))


def _ada_kernel(c_ref, w_ref, b_ref, o_ref):
    c = c_ref[...]
    a = (c * _sigmoid(c)).astype(BF16)
    o_ref[...] = _dot(a, w_ref[...].astype(BF16)) + b_ref[...]


def _ada(c8, w_ada, b_ada):
    n = w_ada.shape[1]
    return pl.pallas_call(
        _ada_kernel,
        grid=(n // ADA_TN,),
        in_specs=[
            pl.BlockSpec((8, D_MODEL), lambda j: (0, 0)),
            pl.BlockSpec((D_MODEL, ADA_TN), lambda j: (0, j)),
            pl.BlockSpec((1, ADA_TN), lambda j: (0, j)),
        ],
        out_specs=pl.BlockSpec((8, ADA_TN), lambda j: (0, j)),
        out_shape=jax.ShapeDtypeStruct((8, n), F32),
        compiler_params=pltpu.CompilerParams(
            dimension_semantics=("arbitrary",), vmem_limit_bytes=VMEM_LIMIT),
        name="ada",
    )(c8, w_ada, b_ada)


def _proj_kernel(x_ref, xh_ref, mod_ref, npre_ref,
                 win_ref,
                 convw_ref, convb_ref, qks_ref, gbias_ref, qn_ref, kvn_ref,
                 wuq_ref, wukv_ref, cos_ref, sin_ref,
                 qk_out, v_out, og_out, ga_out, gb_out, gates_out,
                 qnope_out, qpe_out, knope_out, vb_out, kpe_out,
                 hext_ref, z_ref, *, tm, tiles_per_seq, q_scale):
    i = pl.program_id(0)
    first = (i % tiles_per_seq) == 0
    wqk_ref, wv_ref, wo_ref, wga_ref, wgb_ref, ws_ref = (
        win_ref.at[n * D_MODEL:(n + 1) * D_MODEL, :] for n in range(6))

    def proj(a, wt_ref):
        return lax.dot_general(a, wt_ref[...], _NT, preferred_element_type=F32)

    shift = mod_ref[0:1, :]
    w_scaled = npre_ref[...] * (1.0 + mod_ref[1:2, :])

    def prenorm(xv):
        return _rms(xv, w_scaled) + shift

    hext_ref[HALO:, :] = prenorm(x_ref[...]).astype(BF16)
    hh = prenorm(xh_ref[...])
    hext_ref[0:HALO, :] = jnp.where(first, 0.0, hh).astype(BF16)

    z_ref[...] = proj(hext_ref[...], wqk_ref)
    zwin = z_ref[HALO - 8:, :]
    n_win = tm + 8
    acc = convb_ref[...]
    for j in range(CONV_W):
        lead = 8 - (CONV_W - 1) + j
        tap = zwin if lead == 0 else pltpu.roll(zwin, n_win - lead, 0)
        acc = acc + convw_ref[j:j + 1, :] * tap[0:tm, :]
    act = acc * _sigmoid(acc)
    qk_out[:, 0:MA_QK] = act[:, 0:MA_QK].astype(BF16)
    qk_out[:, MA_QK:] = (act[:, MA_QK:] * qks_ref[:, MA_QK:]).astype(BF16)

    h = hext_ref[HALO:, :]
    v_out[...] = proj(h, wv_ref).astype(BF16)
    og_out[...] = _sigmoid(proj(h, wo_ref)).astype(BF16)
    ga_out[...] = _sigmoid(proj(h, wga_ref)).astype(BF16)
    gb_out[...] = _sigmoid(proj(h, wgb_ref)).astype(BF16)

    s = proj(h, ws_ref)
    c_q = s[:, 0:A_QRANK]
    c_kv = s[:, A_QRANK:A_QRANK + A_KVRANK]
    kp = s[:, 640:768]
    kpr = s[:, 768:896]
    g = s[:, 896:1024] + gbias_ref[...]

    lane = lax.broadcasted_iota(jnp.int32, g.shape, 1)
    logsig = jnp.minimum(g, 0.0) - jnp.log1p(jnp.exp(-jnp.abs(g)))
    gates_out[...] = jnp.where(lane < M_HEADS, g, logsig)

    cos = cos_ref[...]
    sin = sin_ref[...]
    kpe_out[...] = (kp * cos + kpr * sin).astype(BF16)

    q = _dot(_rms(c_q, qn_ref[...] * q_scale).astype(BF16), wuq_ref[...])
    qnope_out[...] = q[:, 0:1024].astype(BF16)
    cos4 = jnp.concatenate([cos] * 4, axis=1)
    sin4 = jnp.concatenate([sin] * 4, axis=1)
    qpe_out[...] = (q[:, 1024:1536] * cos4 + q[:, 1536:2048] * sin4).astype(BF16)

    kv = _dot(_rms(c_kv, kvn_ref[...]).astype(BF16), wukv_ref[...])
    knope_out[...] = kv[:, 0:1024].astype(BF16)
    vb_out[...] = kv[:, 1024:2048].astype(BF16)


def _proj(x2, mod3, npre, win, convw, convb, qks, gbias, qn, kvn, wuq, wukv,
          cos, sin, seq):
    t = x2.shape[0]
    tm = PROJ_TM
    tiles_per_seq = seq // tm
    hb = tm // HALO
    row = lambda i: (i, 0)
    out_w = [1024, 1024, 1024, 1024, 1024, 128, 1024, 512, 1024, 1024, 128]
    out_dt = [BF16] * 5 + [F32] + [BF16] * 5
    kern = functools.partial(
        _proj_kernel, tm=tm, tiles_per_seq=tiles_per_seq,
        q_scale=float((A_NOPE + A_ROPE) ** -0.5 * LOG2E))
    return pl.pallas_call(
        kern,
        grid=(t // tm,),
        in_specs=[
            pl.BlockSpec((tm, D_MODEL), row),
            pl.BlockSpec((HALO, D_MODEL), lambda i: (jnp.maximum(i * hb - 1, 0), 0)),
            pl.BlockSpec((None, N_MOD, D_MODEL), lambda i: (i // tiles_per_seq, 0, 0)),
            _const_spec((1, D_MODEL)),
            _const_spec(win.shape),
            _const_spec(convw.shape), _const_spec(convb.shape), _const_spec(qks.shape),
            _const_spec(gbias.shape), _const_spec(qn.shape), _const_spec(kvn.shape),
            _const_spec(wuq.shape), _const_spec(wukv.shape),
            pl.BlockSpec((tm, 128), lambda i: (i % tiles_per_seq, 0)),
            pl.BlockSpec((tm, 128), lambda i: (i % tiles_per_seq, 0)),
        ],
        out_specs=[pl.BlockSpec((tm, n), row) for n in out_w],
        out_shape=[jax.ShapeDtypeStruct((t, n), d) for n, d in zip(out_w, out_dt)],
        scratch_shapes=[
            pltpu.VMEM((tm + HALO, D_MODEL), BF16),
            pltpu.VMEM((tm + HALO, 2 * MA_QK), F32),
        ],
        compiler_params=pltpu.CompilerParams(
            dimension_semantics=("arbitrary",), vmem_limit_bytes=VMEM_LIMIT),
        name="proj",
    )(x2, x2, mod3, npre, win, convw, convb, qks, gbias,
      qn, kvn, wuq, wukv, cos, sin)


def _split3(x):
    hi = x.astype(BF16)
    r = x - hi.astype(F32)
    mid = r.astype(BF16)
    lo = (r - mid.astype(F32)).astype(BF16)
    return hi, mid, lo


def _mlstm_kernel(qk_ref, v_ref, og_ref, gc_ref, hnw_ref, y_ref,
                  ct_ref, m_ref, *, L):
    @pl.when(pl.program_id(1) == 0)
    def _():
        ct_ref[...] = jnp.zeros_like(ct_ref)
        m_ref[...] = jnp.zeros_like(m_ref)

    n_seq = qk_ref.shape[0]
    nlt = L // 128
    row = lax.broadcasted_iota(jnp.int32, (L, L), 0)
    col = lax.broadcasted_iota(jnp.int32, (L, L), 1)
    causal = col <= row
    tri = causal.astype(BF16)
    ones = jnp.ones((L, 128), BF16)
    chains = [(g, h) for g in range(n_seq) for h in range(M_HEADS)]

    def q_of(g, h):
        return qk_ref[g, :, h * M_DQK:(h + 1) * M_DQK]

    def k_of(g, h):
        return qk_ref[g, :, MA_QK + h * M_DQK:MA_QK + (h + 1) * M_DQK]

    def v_aug_of(g, h):
        return jnp.concatenate([v_ref[g, :, h * M_DV:(h + 1) * M_DV], ones], axis=1)

    gates = []
    for g in range(n_seq):
        gc = gc_ref[g] * LOG2E
        gr = gc.T[0:2 * M_HEADS, :]
        bcol_all = sum(_dot(tri, p) for p in _split3(gc))
        brow_all = sum(lax.dot_general(p, tri, _NT, preferred_element_type=F32)
                       for p in _split3(gr))
        gates.append((gc, gr, bcol_all, brow_all))

    def qk_product(c):
        return lax.dot_general(q_of(*c), k_of(*c), _NT, preferred_element_type=F32)

    def gate_and_mix(c, qk):
        g, h = c
        gc, gr, bcol_all, brow_all = gates[g]
        c_row = gr[h:h + 1, :] - brow_all[M_HEADS + h:M_HEADS + h + 1, :]
        i_rep = jnp.broadcast_to(gc[:, h:h + 1], (L, 128))
        b_rep = jnp.broadcast_to(bcol_all[:, M_HEADS + h:M_HEADS + h + 1], (L, 128))
        m_prev = m_ref[g, h:h + 1, :]
        a = b_rep + m_prev
        m_rows, s_rows = [], []
        for rb in range(nlt):
            rows = slice(rb * 128, (rb + 1) * 128)
            d_row = [b_rep[rows] + c_row[:, t * 128:(t + 1) * 128] for t in range(rb + 1)]
            d_row[rb] = jnp.where(causal[0:128, 0:128], d_row[rb], NEG)
            m_intra = jnp.max(functools.reduce(jnp.maximum, d_row), axis=1, keepdims=True)
            m_blk = jnp.maximum(a[rows], m_intra)
            tiles = [(qk[rows, t * 128:(t + 1) * 128]
                      * jnp.exp2(d_row[t] - m_blk)).astype(BF16) for t in range(rb + 1)]
            tiles += [jnp.zeros((128, 128), BF16)] * (nlt - 1 - rb)
            m_rows.append(m_blk)
            s_rows.append(jnp.concatenate(tiles, axis=1))
        m_out = jnp.concatenate(m_rows, axis=0)
        s_mat = jnp.concatenate(s_rows, axis=0)
        w_inter = jnp.exp2(a - m_out).astype(BF16)
        ct = ct_ref[g, h]
        lhs = jnp.concatenate([s_mat, w_inter * q_of(g, h)], axis=1)
        rhs = jnp.concatenate([v_aug_of(g, h), ct.astype(BF16)], axis=0)
        return _dot(lhs, rhs), (m_out, b_rep, i_rep, m_prev)

    def finish(c, nd, kept):
        g, h = c
        m_out, b_rep, i_rep, m_prev = kept
        num = nd[:, 0:M_DV]
        den = nd[:, M_DV:M_DV + 128]
        inv = 1.0 / jnp.maximum(jnp.abs(den), jnp.exp2(-m_out))
        ms = jnp.mean(num * num, axis=1, keepdims=True)
        f = inv * lax.rsqrt(inv * inv * ms + EPS)
        hn = num * jnp.concatenate([f, f], axis=1) * hnw_ref[:, h * M_DV:(h + 1) * M_DV]
        y_ref[g, :, h * M_DV:(h + 1) * M_DV] = (og_ref[g, :, h * M_DV:(h + 1) * M_DV]
                                                * hn.astype(BF16))

        b_last = b_rep[L - 1:L, :]
        g_prev = b_last + m_prev
        gl = b_last - b_rep + i_rep
        m_new = jnp.maximum(g_prev, jnp.max(gl, axis=0, keepdims=True))
        wk = jnp.exp2(gl - m_new).astype(BF16)
        decay = jnp.exp2(g_prev - m_new)
        wv = jnp.concatenate([wk] * 3, axis=1) * v_aug_of(g, h)
        ct_ref[g, h] = (jnp.concatenate([decay] * 3, axis=1) * ct_ref[g, h]
                        + lax.dot_general(k_of(g, h), wv, _TN, preferred_element_type=F32))
        m_ref[g, h:h + 1, :] = m_new

    pending = []
    for g in range(n_seq):
        mine = [c for c in chains if c[0] == g]
        qk = [qk_product(c) for c in mine]
        mixed = [gate_and_mix(c, s) for c, s in zip(mine, qk)]
        for c, nd, kept in pending:
            finish(c, nd, kept)
        pending = [(c, nd, kept) for c, (nd, kept) in zip(mine, mixed)]
    for c, nd, kept in pending:
        finish(c, nd, kept)


def _mlstm(qk, v, og, gc, hnw):
    b, seq, _ = qk.shape
    L = MLSTM_L
    G = MLSTM_G
    blk = lambda w: pl.BlockSpec((G, L, w), lambda bi, ci: (bi, ci, 0))
    return pl.pallas_call(
        functools.partial(_mlstm_kernel, L=L),
        grid=(b // G, seq // L),
        in_specs=[
            blk(2 * MA_QK), blk(MA_V), blk(MA_V), blk(128),
            _const_spec(hnw.shape),
        ],
        out_specs=blk(MA_V),
        out_shape=jax.ShapeDtypeStruct((b, seq, MA_V), BF16),
        scratch_shapes=[
            pltpu.VMEM((G, M_HEADS, M_DQK, M_DV + 128), F32),
            pltpu.VMEM((G, 8, 128), F32),
        ],
        compiler_params=pltpu.CompilerParams(
            dimension_semantics=("parallel", "arbitrary"), vmem_limit_bytes=VMEM_LIMIT),
        name="mlstm",
    )(qk, v, og, gc, hnw)


def _attn_kernel(qn_ref, qpe_ref, kn_ref, kpe_ref, v_ref, o_ref,
                 kcat_ref, vaug_ref, qcat_ref, m_ref, acc_ref, sa_ref, sb_ref, *, tq, tk, seq):
    assert tq == 2 * tk
    h = pl.program_id(1)
    n_tiles = seq // tq
    nt = tk // 128

    @pl.when(h == 0)
    def _():
        kcat_ref[:, A_NOPE:2 * A_NOPE] = kpe_ref[...]
        vaug_ref[:, A_DV:2 * A_DV] = jnp.ones((seq, A_DV), BF16)

    kcat_ref[:, 0:A_NOPE] = kn_ref[...]
    vaug_ref[:, 0:A_DV] = v_ref[...]
    lane = lax.broadcasted_iota(jnp.int32, (seq, 128), 1)
    own = ((lane // A_ROPE) == (h % 2)).astype(BF16)
    qcat_ref[:, 0:A_NOPE] = qn_ref[...]
    qcat_ref[:, A_NOPE:2 * A_NOPE] = qpe_ref[...] * own

    def logits(qt, kb, r0=0, nrows=tq):
        q0 = pl.multiple_of(qt * tq + r0, tk)
        k0 = pl.multiple_of(kb * tk, tk)
        return lax.dot_general(qcat_ref[pl.ds(q0, nrows), :], kcat_ref[pl.ds(k0, tk), :],
                               _NT, preferred_element_type=F32)

    def softmax_pv(s, kb, r0, nrows, masked):
        k0 = pl.multiple_of(kb * tk, tk)
        if masked:
            r = lax.broadcasted_iota(jnp.int32, (nrows, tk), 0)
            c = lax.broadcasted_iota(jnp.int32, (nrows, tk), 1)
            s = jnp.where(c <= r, s, NEG)
        tiles = [s[:, t * 128:(t + 1) * 128] for t in range(nt)]
        mx = functools.reduce(jnp.maximum, tiles)
        m_prev = m_ref[r0:r0 + nrows, :]
        m_new = jnp.maximum(m_prev, jnp.max(mx, axis=1, keepdims=True))
        alpha = jnp.exp2(m_prev - m_new)
        p = jnp.concatenate([jnp.exp2(t - m_new).astype(BF16) for t in tiles], axis=1)
        pv = _dot(p, vaug_ref[pl.ds(k0, tk), :])
        acc_ref[r0:r0 + nrows, :] = (jnp.concatenate([alpha, alpha], axis=1)
                                     * acc_ref[r0:r0 + nrows, :] + pv)
        m_ref[r0:r0 + nrows, :] = m_new

    sa_ref[...] = logits(0, 0)

    def tile_body(qt, carry):
        m_ref[...] = jnp.full_like(m_ref, NEG)
        acc_ref[...] = jnp.zeros_like(acc_ref)

        def pair_body(i, c):
            j = 2 * i
            sb_ref[...] = logits(qt, j + 1)
            softmax_pv(sa_ref[...], j, 0, tq, False)
            sa_ref[...] = logits(qt, j + 2)
            softmax_pv(sb_ref[...], j + 1, 0, tq, False)
            return c

        lax.fori_loop(0, qt, pair_body, 0)

        d0 = 2 * qt
        sb_ref[0:tk, :] = logits(qt, d0 + 1, tk, tk)
        softmax_pv(sa_ref[...], d0, 0, tq, True)
        sa_ref[...] = logits(jnp.minimum(qt + 1, n_tiles - 1), 0)
        softmax_pv(sb_ref[0:tk, :], d0 + 1, tk, tk, True)
        o0 = pl.multiple_of(qt * tq, tq)
        o_ref[pl.ds(o0, tq), :] = (acc_ref[:, 0:A_DV] / acc_ref[:, A_DV:2 * A_DV]).astype(BF16)
        return carry

    lax.fori_loop(0, n_tiles, tile_body, 0)


def _attn(qn, qpe, kn, kpe, v):
    b, seq, _ = qn.shape
    tq, tk = ATTN_TQ, ATTN_TK
    return pl.pallas_call(
        functools.partial(_attn_kernel, tq=tq, tk=tk, seq=seq),
        grid=(b, A_HEADS),
        in_specs=[
            pl.BlockSpec((None, seq, A_NOPE), lambda bi, h: (bi, 0, h)),
            pl.BlockSpec((None, seq, 128), lambda bi, h: (bi, 0, h // 2)),
            pl.BlockSpec((None, seq, A_NOPE), lambda bi, h: (bi, 0, h)),
            pl.BlockSpec((None, seq, 128), lambda bi, h: (bi, 0, 0)),
            pl.BlockSpec((None, seq, A_DV), lambda bi, h: (bi, 0, h)),
        ],
        out_specs=pl.BlockSpec((None, seq, A_DV), lambda bi, h: (bi, 0, h)),
        out_shape=jax.ShapeDtypeStruct((b, seq, A_HEADS * A_DV), BF16),
        scratch_shapes=[
            pltpu.VMEM((seq, 2 * A_NOPE), BF16),
            pltpu.VMEM((seq, 2 * A_DV), BF16),
            pltpu.VMEM((seq, 2 * A_NOPE), BF16),
            pltpu.VMEM((tq, 128), F32),
            pltpu.VMEM((tq, 2 * A_DV), F32),
            pltpu.VMEM((tq, tk), F32),
            pltpu.VMEM((tq, tk), F32),
        ],
        compiler_params=pltpu.CompilerParams(
            dimension_semantics=("parallel", "arbitrary"),
            vmem_limit_bytes=VMEM_LIMIT),
        name="attn",
    )(qn, qpe, kn, kpe, v)


def _mlp_kernel(x_ref, ya_ref, yb_ref, ga_ref, gb_ref, mod_ref,
                npost_ref, npre2_ref, npost2_ref, wout_ref, w1_ref, w2_ref,
                o_ref, acc_ref):
    post_mix_gain = mod_ref[2:3, :] * npost_ref[...]
    shift_f = mod_ref[3:4, :]
    pre_mlp_gain = (1.0 + mod_ref[4:5, :]) * npre2_ref[...]
    post_mlp_gain = mod_ref[5:6, :] * npost2_ref[...]

    y = (ga_ref[...].astype(F32) * ya_ref[...].astype(F32)
         + gb_ref[...].astype(F32) * yb_ref[...].astype(F32))
    yo = _dot(y.astype(BF16), wout_ref[...])
    x1 = x_ref[...] + _rms(yo, post_mix_gain)

    h2 = (_rms(x1, pre_mlp_gain) + shift_f).astype(BF16)
    for c in range(D_FF // FF_CHUNK):
        u = jnp.maximum(_dot(h2, w1_ref[:, c * FF_CHUNK:(c + 1) * FF_CHUNK]), 0.0)
        part = _dot((u * u).astype(BF16), w2_ref[c * FF_CHUNK:(c + 1) * FF_CHUNK, :])
        if c == 0:
            acc_ref[...] = part
        else:
            acc_ref[...] += part
    o_ref[...] = x1 + _rms(acc_ref[...], post_mlp_gain)


def _mlp(x2, ya, yb, ga, gb, mod3, npost, npre2, npost2, wout, w1, w2, seq):
    t = x2.shape[0]
    tm = MLP_TM
    tiles_per_seq = seq // tm
    row = pl.BlockSpec((tm, D_MODEL), lambda i: (i, 0))
    return pl.pallas_call(
        _mlp_kernel,
        grid=(t // tm,),
        in_specs=[
            row, row, row, row, row,
            pl.BlockSpec((None, N_MOD, D_MODEL), lambda i: (i // tiles_per_seq, 0, 0)),
            _const_spec((1, D_MODEL)), _const_spec((1, D_MODEL)), _const_spec((1, D_MODEL)),
            _const_spec(wout.shape), _const_spec(w1.shape), _const_spec(w2.shape),
        ],
        out_specs=row,
        out_shape=jax.ShapeDtypeStruct((t, D_MODEL), F32),
        scratch_shapes=[pltpu.VMEM((tm, D_MODEL), F32)],
        compiler_params=pltpu.CompilerParams(
            dimension_semantics=("arbitrary",), vmem_limit_bytes=VMEM_LIMIT),
        name="mlp",
    )(x2, ya, yb, ga, gb, mod3, npost, npre2, npost2, wout, w1, w2)


def _rot_half_cols(w):
    half = w.shape[-1] // 2
    return jnp.concatenate([-w[..., half:], w[..., :half]], axis=-1)


def _regroup_kernel(w_ref, o_ref):
    sizes = (MA_QK, MA_QK, MA_V, MA_V, M_HEADS, M_HEADS, A_QRANK, A_KVRANK, A_ROPE,
             D_MODEL, D_MODEL)
    offs = [sum(sizes[:n]) for n in range(len(sizes))]
    o_q, _, _, _, o_i, _, o_cq, _, o_kpe, o_ga, _ = offs
    half = A_ROPE // 2
    cursor = [0]

    def put(src, n, negate=False):
        blk = w_ref[src:src + n, :]
        o_ref[cursor[0]:cursor[0] + n, :] = (-blk if negate else blk).astype(BF16)
        cursor[0] += n

    put(o_q, 2 * MA_QK + 2 * MA_V)
    put(o_ga, 2 * D_MODEL)
    put(o_cq, A_QRANK + A_KVRANK)
    put(o_kpe, A_ROPE)
    put(o_kpe, A_ROPE)
    for _ in range(2):
        put(o_kpe + half, half, negate=True)
        put(o_kpe, half)
    gate_rows = jnp.concatenate(
        [w_ref[o_i:o_i + 2 * M_HEADS, :],
         jnp.zeros((128 - 2 * M_HEADS, w_ref.shape[1]), F32)], axis=0)
    o_ref[cursor[0]:cursor[0] + 128, :] = gate_rows.astype(BF16)


def _prep_in_weights(w_in_t):
    n_in, k = w_in_t.shape
    n_out = 6 * D_MODEL
    return pl.pallas_call(
        _regroup_kernel,
        grid=(k // REGROUP_TK,),
        in_specs=[pl.BlockSpec((n_in, REGROUP_TK), lambda i: (0, i))],
        out_specs=pl.BlockSpec((n_out, REGROUP_TK), lambda i: (0, i)),
        out_shape=jax.ShapeDtypeStruct((n_out, k), BF16),
        compiler_params=pltpu.CompilerParams(
            dimension_semantics=("arbitrary",), vmem_limit_bytes=VMEM_LIMIT),
        name="regroup",
    )(w_in_t)


def _prep_mla_weights(w_uq, w_ukv):
    r = w_uq.reshape(A_QRANK, A_HEADS, A_NOPE + A_ROPE)
    nope = r[:, :, :A_NOPE].reshape(A_QRANK, A_HEADS * A_NOPE)
    pe = r[:, :, A_NOPE:]
    pe_rot = _rot_half_cols(pe).reshape(A_QRANK, A_HEADS * A_ROPE)
    pe = pe.reshape(A_QRANK, A_HEADS * A_ROPE)
    wuq = jnp.concatenate([nope, pe, pe_rot], axis=1).astype(BF16)
    r = w_ukv.reshape(A_KVRANK, A_HEADS, A_NOPE + A_DV)
    wukv = jnp.concatenate([r[:, :, :A_NOPE].reshape(A_KVRANK, -1),
                            r[:, :, A_NOPE:].reshape(A_KVRANK, -1)], axis=1).astype(BF16)
    return wuq, wukv


def _rope_tables(seq):
    half = A_ROPE // 2
    inv_freq = ROPE_THETA ** (-np.arange(half, dtype=np.float64) / half)
    ang = np.arange(seq, dtype=np.float64)[:, None] * inv_freq[None, :]
    reps = 128 // half
    return (jnp.asarray(np.tile(np.cos(ang), (1, reps)), F32),
            jnp.asarray(np.tile(np.sin(ang), (1, reps)), F32))


def kernel(x, c, w_ada, b_ada, norm_pre_mix, norm_post_mix, norm_pre_mlp, norm_post_mlp,
           w_in, mlstm_conv_w, mlstm_conv_b, mlstm_gate_b, mlstm_head_norm,
           mla_q_norm, mla_kv_norm, w_uq, w_ukv, w_out, w_ff1, w_ff2):
    bsz, seq, d = x.shape
    depth = w_ada.shape[0]
    cos, sin = _rope_tables(seq)
    c8 = jnp.pad(c, ((0, 8 - bsz), (0, 0)))
    qks = jnp.concatenate([jnp.ones((1, MA_QK), F32),
                           jnp.full((1, MA_QK), M_DQK ** -0.5, F32)], axis=1)
    row = lambda a: a.reshape(1, -1)
    x2 = x.reshape(bsz * seq, d)
    for l in range(depth):
        mod = _ada(c8, w_ada[l], row(b_ada[l]))[:bsz]
        mod3 = mod.reshape(bsz, N_MOD, d)
        win = _prep_in_weights(jnp.transpose(w_in[l]))
        wuq, wukv = _prep_mla_weights(w_uq[l], w_ukv[l])
        gbias = jnp.pad(row(mlstm_gate_b[l]), ((0, 0), (0, 128 - 2 * M_HEADS)))
        (qk, v_a, og, ga, gb, gates, qnope, qpe, knope, v_b, kpe) = _proj(
            x2, mod3, row(norm_pre_mix[l]), win, mlstm_conv_w[l], row(mlstm_conv_b[l]),
            qks, gbias, row(mla_q_norm[l]), row(mla_kv_norm[l]), wuq, wukv, cos, sin, seq)
        b3 = lambda a: a.reshape(bsz, seq, a.shape[-1])
        y_a = _mlstm(b3(qk), b3(v_a), b3(og), b3(gates), row(mlstm_head_norm[l]))
        y_b = _attn(b3(qnope), b3(qpe), b3(knope), b3(kpe), b3(v_b))
        x2 = _mlp(x2, y_a.reshape(bsz * seq, d), y_b.reshape(bsz * seq, d), ga, gb, mod3,
                  row(norm_post_mix[l]), row(norm_pre_mlp[l]), row(norm_post_mlp[l]),
                  w_out[l].astype(BF16), w_ff1[l].astype(BF16), w_ff2[l].astype(BF16), seq)
    return x2.reshape(bsz, seq, d)
```

```python
import functools

import jax
import jax.numpy as jnp
import numpy as np
from jax import lax
from jax.experimental import pallas as pl
from jax.experimental.pallas import tpu as pltpu

F32 = jnp.float32
BF16 = jnp.bfloat16

D_MODEL = 1024
M_HEADS = 4
M_DQK = 128
M_DV = 256
CONV_W = 4
A_HEADS = 8
A_NOPE = 128
A_ROPE = 64
A_DV = 128
A_QRANK = 384
A_KVRANK = 256
ROPE_THETA = 10000.0
D_FF = 4096
EPS = 1e-6
N_MOD = 6
MA_QK = M_HEADS * M_DQK
MA_V = M_HEADS * M_DV

PROJ_TM = 512
HALO = 16
MLSTM_L = 256
MLSTM_G = 4
ATTN_TQ = 1024
ATTN_TK = 512
MLP_TM = 512
FF_CHUNK = 1024
ADA_TN = 1536
REGROUP_TK = 256
VMEM_LIMIT = 56 * 1024 * 1024

NEG = -1e30
LOG2E = 1.4426950408889634

_NT = (((1,), (1,)), ((), ()))
_TN = (((0,), (0,)), ((), ()))


def _dot(a, b):
    return jnp.dot(a, b, preferred_element_type=F32)


def _sigmoid(x):
    return jax.nn.sigmoid(x)


def _rms(x, w):
    return x * lax.rsqrt(jnp.mean(x * x, axis=-1, keepdims=True) + EPS) * w


def _const_spec(shape):
    nd = len(shape)
    return pl.BlockSpec(shape, lambda *_: (0,) * nd, pipeline_mode=pl.Buffered(1))


def _ada_kernel(c_ref, w_ref, b_ref, o_ref):
    c = c_ref[...]
    a = (c * _sigmoid(c)).astype(BF16)
    o_ref[...] = _dot(a, w_ref[...].astype(BF16)) + b_ref[...]


def _ada(c8, w_ada, b_ada):
    n = w_ada.shape[1]
    return pl.pallas_call(
        _ada_kernel,
        grid=(n // ADA_TN,),
        in_specs=[
            pl.BlockSpec((8, D_MODEL), lambda j: (0, 0)),
            pl.BlockSpec((D_MODEL, ADA_TN), lambda j: (0, j)),
            pl.BlockSpec((1, ADA_TN), lambda j: (0, j)),
        ],
        out_specs=pl.BlockSpec((8, ADA_TN), lambda j: (0, j)),
        out_shape=jax.ShapeDtypeStruct((8, n), F32),
        compiler_params=pltpu.CompilerParams(
            dimension_semantics=("arbitrary",), vmem_limit_bytes=VMEM_LIMIT),
        name="ada",
    )(c8, w_ada, b_ada)


def _proj_kernel(x_ref, xh_ref, mod_ref, npre_ref,
                 win_ref,
                 convw_ref, convb_ref, qks_ref, gbias_ref, qn_ref, kvn_ref,
                 wuq_ref, wukv_ref, cos_ref, sin_ref,
                 qk_out, v_out, og_out, ga_out, gb_out, gates_out,
                 qnope_out, qpe_out, knope_out, vb_out, kpe_out,
                 hext_ref, z_ref, *, tm, tiles_per_seq, q_scale):
    i = pl.program_id(0)
    first = (i % tiles_per_seq) == 0
    wqk_ref, wv_ref, wo_ref, wga_ref, wgb_ref, ws_ref = (
        win_ref.at[n * D_MODEL:(n + 1) * D_MODEL, :] for n in range(6))

    def proj(a, wt_ref):
        return lax.dot_general(a, wt_ref[...], _NT, preferred_element_type=F32)

    shift = mod_ref[0:1, :]
    w_scaled = npre_ref[...] * (1.0 + mod_ref[1:2, :])

    def prenorm(xv):
        return _rms(xv, w_scaled) + shift

    hext_ref[HALO:, :] = prenorm(x_ref[...]).astype(BF16)
    hh = prenorm(xh_ref[...])
    hext_ref[0:HALO, :] = jnp.where(first, 0.0, hh).astype(BF16)

    z_ref[...] = proj(hext_ref[...], wqk_ref)
    zwin = z_ref[HALO - 8:, :]
    n_win = tm + 8
    acc = convb_ref[...]
    for j in range(CONV_W):
        lead = 8 - (CONV_W - 1) + j
        tap = zwin if lead == 0 else pltpu.roll(zwin, n_win - lead, 0)
        acc = acc + convw_ref[j:j + 1, :] * tap[0:tm, :]
    qk_out[...] = (acc * _sigmoid(acc) * qks_ref[...]).astype(BF16)

    h = hext_ref[HALO:, :]
    v_out[...] = proj(h, wv_ref).astype(BF16)
    og_out[...] = _sigmoid(proj(h, wo_ref)).astype(BF16)
    ga_out[...] = _sigmoid(proj(h, wga_ref)).astype(BF16)
    gb_out[...] = _sigmoid(proj(h, wgb_ref)).astype(BF16)

    s = proj(h, ws_ref)
    c_q = s[:, 0:A_QRANK]
    c_kv = s[:, A_QRANK:A_QRANK + A_KVRANK]
    kp = s[:, 640:768]
    kpr = s[:, 768:896]
    g = s[:, 896:1024] + gbias_ref[...]

    lane = lax.broadcasted_iota(jnp.int32, g.shape, 1)
    logsig = jnp.minimum(g, 0.0) - jnp.log1p(jnp.exp(-jnp.abs(g)))
    gates_out[...] = jnp.where(lane < M_HEADS, g, logsig)

    cos = cos_ref[...]
    sin = sin_ref[...]
    kpe_out[...] = (kp * cos + kpr * sin).astype(BF16)

    q = _dot(_rms(c_q, qn_ref[...]).astype(BF16), wuq_ref[...])
    qnope_out[...] = (q[:, 0:1024] * q_scale).astype(BF16)
    cos4 = jnp.concatenate([cos] * 4, axis=1)
    sin4 = jnp.concatenate([sin] * 4, axis=1)
    qpe = q[:, 1024:1536] * cos4 + q[:, 1536:2048] * sin4
    qpe_out[...] = (qpe * q_scale).astype(BF16)

    kv = _dot(_rms(c_kv, kvn_ref[...]).astype(BF16), wukv_ref[...])
    knope_out[...] = kv[:, 0:1024].astype(BF16)
    vb_out[...] = kv[:, 1024:2048].astype(BF16)


def _proj(x2, mod3, npre, win, convw, convb, qks, gbias, qn, kvn, wuq, wukv,
          cos, sin, seq):
    t = x2.shape[0]
    tm = PROJ_TM
    tiles_per_seq = seq // tm
    hb = tm // HALO
    row = lambda i: (i, 0)
    out_w = [1024, 1024, 1024, 1024, 1024, 128, 1024, 512, 1024, 1024, 128]
    out_dt = [BF16] * 5 + [F32] + [BF16] * 5
    kern = functools.partial(
        _proj_kernel, tm=tm, tiles_per_seq=tiles_per_seq,
        q_scale=float((A_NOPE + A_ROPE) ** -0.5 * LOG2E))
    return pl.pallas_call(
        kern,
        grid=(t // tm,),
        in_specs=[
            pl.BlockSpec((tm, D_MODEL), row),
            pl.BlockSpec((HALO, D_MODEL), lambda i: (jnp.maximum(i * hb - 1, 0), 0)),
            pl.BlockSpec((None, N_MOD, D_MODEL), lambda i: (i // tiles_per_seq, 0, 0)),
            _const_spec((1, D_MODEL)),
            _const_spec(win.shape),
            _const_spec(convw.shape), _const_spec(convb.shape), _const_spec(qks.shape),
            _const_spec(gbias.shape), _const_spec(qn.shape), _const_spec(kvn.shape),
            _const_spec(wuq.shape), _const_spec(wukv.shape),
            pl.BlockSpec((tm, 128), lambda i: (i % tiles_per_seq, 0)),
            pl.BlockSpec((tm, 128), lambda i: (i % tiles_per_seq, 0)),
        ],
        out_specs=[pl.BlockSpec((tm, n), row) for n in out_w],
        out_shape=[jax.ShapeDtypeStruct((t, n), d) for n, d in zip(out_w, out_dt)],
        scratch_shapes=[
            pltpu.VMEM((tm + HALO, D_MODEL), BF16),
            pltpu.VMEM((tm + HALO, 2 * MA_QK), F32),
        ],
        compiler_params=pltpu.CompilerParams(
            dimension_semantics=("arbitrary",), vmem_limit_bytes=VMEM_LIMIT),
        name="proj",
    )(x2, x2, mod3, npre, win, convw, convb, qks, gbias,
      qn, kvn, wuq, wukv, cos, sin)


def _split3(x):
    hi = x.astype(BF16)
    r = x - hi.astype(F32)
    mid = r.astype(BF16)
    lo = (r - mid.astype(F32)).astype(BF16)
    return hi, mid, lo


def _mlstm_kernel(qk_ref, v_ref, og_ref, gc_ref, hnw_ref, y_ref,
                  ct_ref, m_ref, *, L):
    @pl.when(pl.program_id(1) == 0)
    def _():
        ct_ref[...] = jnp.zeros_like(ct_ref)
        m_ref[...] = jnp.zeros_like(m_ref)

    n_seq = qk_ref.shape[0]
    nlt = L // 128
    row = lax.broadcasted_iota(jnp.int32, (L, L), 0)
    col = lax.broadcasted_iota(jnp.int32, (L, L), 1)
    causal = col <= row
    tri = causal.astype(BF16)
    ones = jnp.ones((L, 128), BF16)
    chains = [(g, h) for g in range(n_seq) for h in range(M_HEADS)]

    def q_of(g, h):
        return qk_ref[g, :, h * M_DQK:(h + 1) * M_DQK]

    def k_of(g, h):
        return qk_ref[g, :, MA_QK + h * M_DQK:MA_QK + (h + 1) * M_DQK]

    def v_aug_of(g, h):
        return jnp.concatenate([v_ref[g, :, h * M_DV:(h + 1) * M_DV], ones], axis=1)

    gates = []
    for g in range(n_seq):
        gc = gc_ref[g] * LOG2E
        gr = gc.T[0:2 * M_HEADS, :]
        bcol_all = sum(_dot(tri, p) for p in _split3(gc))
        brow_all = sum(lax.dot_general(p, tri, _NT, preferred_element_type=F32)
                       for p in _split3(gr))
        gates.append((gc, gr, bcol_all, brow_all))

    def qk_product(c):
        return lax.dot_general(q_of(*c), k_of(*c), _NT, preferred_element_type=F32)

    def gate_and_mix(c, qk):
        g, h = c
        gc, gr, bcol_all, brow_all = gates[g]
        c_row = gr[h:h + 1, :] - brow_all[M_HEADS + h:M_HEADS + h + 1, :]
        i_rep = jnp.broadcast_to(gc[:, h:h + 1], (L, 128))
        b_rep = jnp.broadcast_to(bcol_all[:, M_HEADS + h:M_HEADS + h + 1], (L, 128))
        m_prev = m_ref[g, h:h + 1, :]
        a = b_rep + m_prev
        m_rows, s_rows = [], []
        for rb in range(nlt):
            rows = slice(rb * 128, (rb + 1) * 128)
            d_row = [b_rep[rows] + c_row[:, t * 128:(t + 1) * 128] for t in range(rb + 1)]
            d_row[rb] = jnp.where(causal[0:128, 0:128], d_row[rb], NEG)
            m_intra = jnp.max(functools.reduce(jnp.maximum, d_row), axis=1, keepdims=True)
            m_blk = jnp.maximum(a[rows], m_intra)
            tiles = [(qk[rows, t * 128:(t + 1) * 128]
                      * jnp.exp2(d_row[t] - m_blk)).astype(BF16) for t in range(rb + 1)]
            tiles += [jnp.zeros((128, 128), BF16)] * (nlt - 1 - rb)
            m_rows.append(m_blk)
            s_rows.append(jnp.concatenate(tiles, axis=1))
        m_out = jnp.concatenate(m_rows, axis=0)
        s_mat = jnp.concatenate(s_rows, axis=0)
        w_inter = jnp.exp2(a - m_out).astype(BF16)
        ct = ct_ref[g, h]
        lhs = jnp.concatenate([s_mat, w_inter * q_of(g, h)], axis=1)
        rhs = jnp.concatenate([v_aug_of(g, h), ct.astype(BF16)], axis=0)
        return _dot(lhs, rhs), (m_out, b_rep, i_rep, m_prev)

    def finish(c, nd, kept):
        g, h = c
        m_out, b_rep, i_rep, m_prev = kept
        num = nd[:, 0:M_DV]
        den = nd[:, M_DV:M_DV + 128]
        inv = 1.0 / jnp.maximum(jnp.abs(den), jnp.exp2(-m_out))
        ms = jnp.mean(num * num, axis=1, keepdims=True)
        f = inv * lax.rsqrt(inv * inv * ms + EPS)
        hn = num * jnp.concatenate([f, f], axis=1) * hnw_ref[:, h * M_DV:(h + 1) * M_DV]
        y_ref[g, :, h * M_DV:(h + 1) * M_DV] = (og_ref[g, :, h * M_DV:(h + 1) * M_DV]
                                                * hn.astype(BF16))

        b_last = b_rep[L - 1:L, :]
        g_prev = b_last + m_prev
        gl = b_last - b_rep + i_rep
        m_new = jnp.maximum(g_prev, jnp.max(gl, axis=0, keepdims=True))
        wk = jnp.exp2(gl - m_new).astype(BF16)
        decay = jnp.exp2(g_prev - m_new)
        wv = jnp.concatenate([wk] * 3, axis=1) * v_aug_of(g, h)
        ct_ref[g, h] = (jnp.concatenate([decay] * 3, axis=1) * ct_ref[g, h]
                        + lax.dot_general(k_of(g, h), wv, _TN, preferred_element_type=F32))
        m_ref[g, h:h + 1, :] = m_new

    pending = []
    for g in range(n_seq):
        mine = [c for c in chains if c[0] == g]
        qk = [qk_product(c) for c in mine]
        mixed = [gate_and_mix(c, s) for c, s in zip(mine, qk)]
        for c, nd, kept in pending:
            finish(c, nd, kept)
        pending = [(c, nd, kept) for c, (nd, kept) in zip(mine, mixed)]
    for c, nd, kept in pending:
        finish(c, nd, kept)


def _mlstm(qk, v, og, gc, hnw):
    b, seq, _ = qk.shape
    L = MLSTM_L
    G = MLSTM_G
    blk = lambda w: pl.BlockSpec((G, L, w), lambda bi, ci: (bi, ci, 0))
    return pl.pallas_call(
        functools.partial(_mlstm_kernel, L=L),
        grid=(b // G, seq // L),
        in_specs=[
            blk(2 * MA_QK), blk(MA_V), blk(MA_V), blk(128),
            _const_spec(hnw.shape),
        ],
        out_specs=blk(MA_V),
        out_shape=jax.ShapeDtypeStruct((b, seq, MA_V), BF16),
        scratch_shapes=[
            pltpu.VMEM((G, M_HEADS, M_DQK, M_DV + 128), F32),
            pltpu.VMEM((G, 8, 128), F32),
        ],
        compiler_params=pltpu.CompilerParams(
            dimension_semantics=("parallel", "arbitrary"), vmem_limit_bytes=VMEM_LIMIT),
        name="mlstm",
    )(qk, v, og, gc, hnw)


def _attn_kernel(qn_ref, qpe_ref, kn_ref, kpe_ref, v_ref, o_ref,
                 kcat_ref, vaug_ref, qcat_ref, m_ref, acc_ref, sa_ref, sb_ref, *, tq, tk, seq):
    assert tq == 2 * tk
    h = pl.program_id(1)
    n_tiles = seq // tq
    nt = tk // 128

    @pl.when(h == 0)
    def _():
        kcat_ref[:, A_NOPE:2 * A_NOPE] = kpe_ref[...]
        vaug_ref[:, A_DV:2 * A_DV] = jnp.ones((seq, A_DV), BF16)

    lane = lax.broadcasted_iota(jnp.int32, (tq, 128), 1)
    own = ((lane // A_ROPE) == (h % 2)).astype(BF16)

    def stage_rows(tile):
        rows = pl.ds(pl.multiple_of(tile * tq, tq), tq)
        kcat_ref[rows, 0:A_NOPE] = kn_ref[rows, :]
        vaug_ref[rows, 0:A_DV] = v_ref[rows, :]
        qcat_ref[rows, 0:A_NOPE] = qn_ref[rows, :]
        qcat_ref[rows, A_NOPE:2 * A_NOPE] = qpe_ref[rows, :] * own

    stage_rows(0)

    def logits(qt, kb, r0=0, nrows=tq):
        q0 = pl.multiple_of(qt * tq + r0, tk)
        k0 = pl.multiple_of(kb * tk, tk)
        return lax.dot_general(qcat_ref[pl.ds(q0, nrows), :], kcat_ref[pl.ds(k0, tk), :],
                               _NT, preferred_element_type=F32)

    def softmax_pv(s, kb, r0, nrows, masked):
        k0 = pl.multiple_of(kb * tk, tk)
        if masked:
            r = lax.broadcasted_iota(jnp.int32, (nrows, tk), 0)
            c = lax.broadcasted_iota(jnp.int32, (nrows, tk), 1)
            s = jnp.where(c <= r, s, NEG)
        tiles = [s[:, t * 128:(t + 1) * 128] for t in range(nt)]
        mx = functools.reduce(jnp.maximum, tiles)
        m_prev = m_ref[r0:r0 + nrows, :]
        m_new = jnp.maximum(m_prev, jnp.max(mx, axis=1, keepdims=True))
        alpha = jnp.exp2(m_prev - m_new)
        p = jnp.concatenate([jnp.exp2(t - m_new).astype(BF16) for t in tiles], axis=1)
        pv = _dot(p, vaug_ref[pl.ds(k0, tk), :])
        acc_ref[r0:r0 + nrows, :] = (jnp.concatenate([alpha, alpha], axis=1)
                                     * acc_ref[r0:r0 + nrows, :] + pv)
        m_ref[r0:r0 + nrows, :] = m_new

    sa_ref[...] = logits(0, 0)

    def tile_body(qt, carry):
        m_ref[...] = jnp.full_like(m_ref, NEG)
        acc_ref[...] = jnp.zeros_like(acc_ref)

        def pair_body(i, c):
            j = 2 * i
            sb_ref[...] = logits(qt, j + 1)
            softmax_pv(sa_ref[...], j, 0, tq, False)
            sa_ref[...] = logits(qt, j + 2)
            softmax_pv(sb_ref[...], j + 1, 0, tq, False)
            return c

        lax.fori_loop(0, qt, pair_body, 0)

        d0 = 2 * qt
        nxt = jnp.minimum(qt + 1, n_tiles - 1)
        sb_ref[0:tk, :] = logits(qt, d0 + 1, tk, tk)
        stage_rows(nxt)
        softmax_pv(sa_ref[...], d0, 0, tq, True)
        sa_ref[...] = logits(nxt, 0)
        softmax_pv(sb_ref[0:tk, :], d0 + 1, tk, tk, True)
        o0 = pl.multiple_of(qt * tq, tq)
        o_ref[pl.ds(o0, tq), :] = (acc_ref[:, 0:A_DV] / acc_ref[:, A_DV:2 * A_DV]).astype(BF16)
        return carry

    lax.fori_loop(0, n_tiles, tile_body, 0)


def _attn(qn, qpe, kn, kpe, v):
    b, seq, _ = qn.shape
    tq, tk = ATTN_TQ, ATTN_TK
    return pl.pallas_call(
        functools.partial(_attn_kernel, tq=tq, tk=tk, seq=seq),
        grid=(b, A_HEADS),
        in_specs=[
            pl.BlockSpec((None, seq, A_NOPE), lambda bi, h: (bi, 0, h)),
            pl.BlockSpec((None, seq, 128), lambda bi, h: (bi, 0, h // 2)),
            pl.BlockSpec((None, seq, A_NOPE), lambda bi, h: (bi, 0, h)),
            pl.BlockSpec((None, seq, 128), lambda bi, h: (bi, 0, 0)),
            pl.BlockSpec((None, seq, A_DV), lambda bi, h: (bi, 0, h)),
        ],
        out_specs=pl.BlockSpec((None, seq, A_DV), lambda bi, h: (bi, 0, h)),
        out_shape=jax.ShapeDtypeStruct((b, seq, A_HEADS * A_DV), BF16),
        scratch_shapes=[
            pltpu.VMEM((seq, 2 * A_NOPE), BF16),
            pltpu.VMEM((seq, 2 * A_DV), BF16),
            pltpu.VMEM((seq, 2 * A_NOPE), BF16),
            pltpu.VMEM((tq, 128), F32),
            pltpu.VMEM((tq, 2 * A_DV), F32),
            pltpu.VMEM((tq, tk), F32),
            pltpu.VMEM((tq, tk), F32),
        ],
        compiler_params=pltpu.CompilerParams(
            dimension_semantics=("parallel", "arbitrary"),
            vmem_limit_bytes=VMEM_LIMIT),
        name="attn",
    )(qn, qpe, kn, kpe, v)


def _mlp_kernel(x_ref, ya_ref, yb_ref, ga_ref, gb_ref, mod_ref,
                npost_ref, npre2_ref, npost2_ref, wout_ref, w1_ref, w2_ref,
                o_ref, acc_ref):
    post_mix_gain = mod_ref[2:3, :] * npost_ref[...]
    shift_f = mod_ref[3:4, :]
    pre_mlp_gain = (1.0 + mod_ref[4:5, :]) * npre2_ref[...]
    post_mlp_gain = mod_ref[5:6, :] * npost2_ref[...]

    y = (ga_ref[...].astype(F32) * ya_ref[...].astype(F32)
         + gb_ref[...].astype(F32) * yb_ref[...].astype(F32))
    yo = _dot(y.astype(BF16), wout_ref[...])
    x1 = x_ref[...] + _rms(yo, post_mix_gain)

    h2 = (_rms(x1, pre_mlp_gain) + shift_f).astype(BF16)
    for c in range(D_FF // FF_CHUNK):
        u = jnp.maximum(_dot(h2, w1_ref[:, c * FF_CHUNK:(c + 1) * FF_CHUNK]), 0.0)
        part = _dot((u * u).astype(BF16), w2_ref[c * FF_CHUNK:(c + 1) * FF_CHUNK, :])
        if c == 0:
            acc_ref[...] = part
        else:
            acc_ref[...] += part
    o_ref[...] = x1 + _rms(acc_ref[...], post_mlp_gain)


def _mlp(x2, ya, yb, ga, gb, mod3, npost, npre2, npost2, wout, w1, w2, seq):
    t = x2.shape[0]
    tm = MLP_TM
    tiles_per_seq = seq // tm
    row = pl.BlockSpec((tm, D_MODEL), lambda i: (i, 0))
    return pl.pallas_call(
        _mlp_kernel,
        grid=(t // tm,),
        in_specs=[
            row, row, row, row, row,
            pl.BlockSpec((None, N_MOD, D_MODEL), lambda i: (i // tiles_per_seq, 0, 0)),
            _const_spec((1, D_MODEL)), _const_spec((1, D_MODEL)), _const_spec((1, D_MODEL)),
            _const_spec(wout.shape), _const_spec(w1.shape), _const_spec(w2.shape),
        ],
        out_specs=row,
        out_shape=jax.ShapeDtypeStruct((t, D_MODEL), F32),
        scratch_shapes=[pltpu.VMEM((tm, D_MODEL), F32)],
        compiler_params=pltpu.CompilerParams(
            dimension_semantics=("arbitrary",), vmem_limit_bytes=VMEM_LIMIT),
        name="mlp",
    )(x2, ya, yb, ga, gb, mod3, npost, npre2, npost2, wout, w1, w2)


def _rot_half_cols(w):
    half = w.shape[-1] // 2
    return jnp.concatenate([-w[..., half:], w[..., :half]], axis=-1)


def _regroup_kernel(w_ref, o_ref):
    sizes = (MA_QK, MA_QK, MA_V, MA_V, M_HEADS, M_HEADS, A_QRANK, A_KVRANK, A_ROPE,
             D_MODEL, D_MODEL)
    offs = [sum(sizes[:n]) for n in range(len(sizes))]
    o_q, _, _, _, o_i, _, o_cq, _, o_kpe, o_ga, _ = offs
    half = A_ROPE // 2
    cursor = [0]

    def put(src, n, negate=False):
        blk = w_ref[src:src + n, :]
        o_ref[cursor[0]:cursor[0] + n, :] = (-blk if negate else blk).astype(BF16)
        cursor[0] += n

    put(o_q, 2 * MA_QK + 2 * MA_V)
    put(o_ga, 2 * D_MODEL)
    put(o_cq, A_QRANK + A_KVRANK)
    put(o_kpe, A_ROPE)
    put(o_kpe, A_ROPE)
    for _ in range(2):
        put(o_kpe + half, half, negate=True)
        put(o_kpe, half)
    gate_rows = jnp.concatenate(
        [w_ref[o_i:o_i + 2 * M_HEADS, :],
         jnp.zeros((128 - 2 * M_HEADS, w_ref.shape[1]), F32)], axis=0)
    o_ref[cursor[0]:cursor[0] + 128, :] = gate_rows.astype(BF16)


def _prep_in_weights(w_in_t):
    n_in, k = w_in_t.shape
    n_out = 6 * D_MODEL
    return pl.pallas_call(
        _regroup_kernel,
        grid=(k // REGROUP_TK,),
        in_specs=[pl.BlockSpec((n_in, REGROUP_TK), lambda i: (0, i))],
        out_specs=pl.BlockSpec((n_out, REGROUP_TK), lambda i: (0, i)),
        out_shape=jax.ShapeDtypeStruct((n_out, k), BF16),
        compiler_params=pltpu.CompilerParams(
            dimension_semantics=("arbitrary",), vmem_limit_bytes=VMEM_LIMIT),
        name="regroup",
    )(w_in_t)


def _prep_mla_weights(w_uq, w_ukv):
    r = w_uq.reshape(A_QRANK, A_HEADS, A_NOPE + A_ROPE)
    nope = r[:, :, :A_NOPE].reshape(A_QRANK, A_HEADS * A_NOPE)
    pe = r[:, :, A_NOPE:]
    pe_rot = _rot_half_cols(pe).reshape(A_QRANK, A_HEADS * A_ROPE)
    pe = pe.reshape(A_QRANK, A_HEADS * A_ROPE)
    wuq = jnp.concatenate([nope, pe, pe_rot], axis=1).astype(BF16)
    r = w_ukv.reshape(A_KVRANK, A_HEADS, A_NOPE + A_DV)
    wukv = jnp.concatenate([r[:, :, :A_NOPE].reshape(A_KVRANK, -1),
                            r[:, :, A_NOPE:].reshape(A_KVRANK, -1)], axis=1).astype(BF16)
    return wuq, wukv


def _rope_tables(seq):
    half = A_ROPE // 2
    inv_freq = ROPE_THETA ** (-np.arange(half, dtype=np.float64) / half)
    ang = np.arange(seq, dtype=np.float64)[:, None] * inv_freq[None, :]
    reps = 128 // half
    return (jnp.asarray(np.tile(np.cos(ang), (1, reps)), F32),
            jnp.asarray(np.tile(np.sin(ang), (1, reps)), F32))


def kernel(x, c, w_ada, b_ada, norm_pre_mix, norm_post_mix, norm_pre_mlp, norm_post_mlp,
           w_in, mlstm_conv_w, mlstm_conv_b, mlstm_gate_b, mlstm_head_norm,
           mla_q_norm, mla_kv_norm, w_uq, w_ukv, w_out, w_ff1, w_ff2):
    bsz, seq, d = x.shape
    depth = w_ada.shape[0]
    cos, sin = _rope_tables(seq)
    c8 = jnp.pad(c, ((0, 8 - bsz), (0, 0)))
    qks = jnp.concatenate([jnp.ones((1, MA_QK), F32),
                           jnp.full((1, MA_QK), M_DQK ** -0.5, F32)], axis=1)
    row = lambda a: a.reshape(1, -1)
    x2 = x.reshape(bsz * seq, d)
    for l in range(depth):
        mod = _ada(c8, w_ada[l], row(b_ada[l]))[:bsz]
        mod3 = mod.reshape(bsz, N_MOD, d)
        win = _prep_in_weights(jnp.transpose(w_in[l]))
        wuq, wukv = _prep_mla_weights(w_uq[l], w_ukv[l])
        gbias = jnp.pad(row(mlstm_gate_b[l]), ((0, 0), (0, 128 - 2 * M_HEADS)))
        (qk, v_a, og, ga, gb, gates, qnope, qpe, knope, v_b, kpe) = _proj(
            x2, mod3, row(norm_pre_mix[l]), win, mlstm_conv_w[l], row(mlstm_conv_b[l]),
            qks, gbias, row(mla_q_norm[l]), row(mla_kv_norm[l]), wuq, wukv, cos, sin, seq)
        b3 = lambda a: a.reshape(bsz, seq, a.shape[-1])
        y_a = _mlstm(b3(qk), b3(v_a), b3(og), b3(gates), row(mlstm_head_norm[l]))
        y_b = _attn(b3(qnope), b3(qpe), b3(knope), b3(kpe), b3(v_b))
        x2 = _mlp(x2, y_a.reshape(bsz * seq, d), y_b.reshape(bsz * seq, d), ga, gb, mod3,
                  row(norm_post_mix[l]), row(norm_pre_mlp[l]), row(norm_post_mlp[l]),
                  w_out[l].astype(BF16), w_ff1[l].astype(BF16), w_ff2[l].astype(BF16), seq)
    return x2.reshape(bsz, seq, d)
```

```python
import functools

import jax
import jax.numpy as jnp
import numpy as np
from jax import lax
from jax.experimental import pallas as pl
from jax.experimental.pallas import tpu as pltpu

F32 = jnp.float32
BF16 = jnp.bfloat16

D_MODEL = 1024
M_HEADS = 4
M_DQK = 128
M_DV = 256
CONV_W = 4
A_HEADS = 8
A_NOPE = 128
A_ROPE = 64
A_DV = 128
A_QRANK = 384
A_KVRANK = 256
ROPE_THETA = 10000.0
D_FF = 4096
EPS = 1e-6
N_MOD = 6
MA_QK = M_HEADS * M_DQK
MA_V = M_HEADS * M_DV

PROJ_TM = 512
HALO = 16
MLSTM_L = 256
MLSTM_G = 4
ATTN_TQ = 1024
ATTN_TK = 512
MLP_TM = 512
FF_CHUNK = 1024
ADA_TN = 1536
REGROUP_TK = 256
VMEM_LIMIT = 56 * 1024 * 1024

NEG = -1e30
LOG2E = 1.4426950408889634

_NT = (((1,), (1,)), ((), ()))
_TN = (((0,), (0,)), ((), ()))


def _dot(a, b):
    return jnp.dot(a, b, preferred_element_type=F32)


def _sigmoid(x):
    return jax.nn.sigmoid(x)


def _rms(x, w):
    return x * lax.rsqrt(jnp.mean(x * x, axis=-1, keepdims=True) + EPS) * w


def _const_spec(shape):
    nd = len(shape)
    return pl.BlockSpec(shape, lambda *_: (0,) * nd, pipeline_mode=pl.Buffered(1))


def _ada_kernel(c_ref, w_ref, b_ref, o_ref):
    c = c_ref[...]
    a = (c * _sigmoid(c)).astype(BF16)
    o_ref[...] = _dot(a, w_ref[...].astype(BF16)) + b_ref[...]


def _ada(c8, w_ada, b_ada):
    n = w_ada.shape[1]
    return pl.pallas_call(
        _ada_kernel,
        grid=(n // ADA_TN,),
        in_specs=[
            pl.BlockSpec((8, D_MODEL), lambda j: (0, 0)),
            pl.BlockSpec((D_MODEL, ADA_TN), lambda j: (0, j)),
            pl.BlockSpec((1, ADA_TN), lambda j: (0, j)),
        ],
        out_specs=pl.BlockSpec((8, ADA_TN), lambda j: (0, j)),
        out_shape=jax.ShapeDtypeStruct((8, n), F32),
        compiler_params=pltpu.CompilerParams(
            dimension_semantics=("arbitrary",), vmem_limit_bytes=VMEM_LIMIT),
        name="ada",
    )(c8, w_ada, b_ada)


def _proj_kernel(x_ref, xh_ref, mod_ref, npre_ref,
                 win_ref,
                 convw_ref, convb_ref, qks_ref, gbias_ref, qn_ref, kvn_ref,
                 wuq_ref, wukv_ref, cos_ref, sin_ref,
                 qk_out, v_out, og_out, ga_out, gb_out, gates_out,
                 qnope_out, qpe_out, knope_out, vb_out, kpe_out,
                 hext_ref, z_ref, *, tm, tiles_per_seq, q_scale):
    i = pl.program_id(0)
    first = (i % tiles_per_seq) == 0
    wqk_ref, wv_ref, wo_ref, wga_ref, wgb_ref, ws_ref = (
        win_ref.at[n * D_MODEL:(n + 1) * D_MODEL, :] for n in range(6))

    def proj(a, wt_ref):
        return lax.dot_general(a, wt_ref[...], _NT, preferred_element_type=F32)

    shift = mod_ref[0:1, :]
    w_scaled = npre_ref[...] * (1.0 + mod_ref[1:2, :])

    def prenorm(xv):
        return _rms(xv, w_scaled) + shift

    hext_ref[HALO:, :] = prenorm(x_ref[...]).astype(BF16)
    hh = prenorm(xh_ref[...])
    hext_ref[0:HALO, :] = jnp.where(first, 0.0, hh).astype(BF16)

    z_ref[...] = proj(hext_ref[...], wqk_ref)
    zwin = z_ref[HALO - 8:, :]
    n_win = tm + 8
    acc = convb_ref[...]
    for j in range(CONV_W):
        lead = 8 - (CONV_W - 1) + j
        tap = zwin if lead == 0 else pltpu.roll(zwin, n_win - lead, 0)
        acc = acc + convw_ref[j:j + 1, :] * tap[0:tm, :]
    qk_out[...] = (acc * _sigmoid(acc) * qks_ref[...]).astype(BF16)

    h = hext_ref[HALO:, :]
    v_out[...] = proj(h, wv_ref).astype(BF16)
    og_out[...] = _sigmoid(proj(h, wo_ref)).astype(BF16)
    ga_out[...] = _sigmoid(proj(h, wga_ref)).astype(BF16)
    gb_out[...] = _sigmoid(proj(h, wgb_ref)).astype(BF16)

    s = proj(h, ws_ref)
    c_q = s[:, 0:A_QRANK]
    c_kv = s[:, A_QRANK:A_QRANK + A_KVRANK]
    kp = s[:, 640:768]
    kpr = s[:, 768:896]
    g = s[:, 896:1024] + gbias_ref[...]

    lane = lax.broadcasted_iota(jnp.int32, g.shape, 1)
    logsig = jnp.minimum(g, 0.0) - jnp.log1p(jnp.exp(-jnp.abs(g)))
    gates_out[...] = jnp.where(lane < M_HEADS, g, logsig)

    cos = cos_ref[...]
    sin = sin_ref[...]
    kpe_out[...] = (kp * cos + kpr * sin).astype(BF16)

    q = _dot(_rms(c_q, qn_ref[...]).astype(BF16), wuq_ref[...])
    qnope_out[...] = (q[:, 0:1024] * q_scale).astype(BF16)
    cos4 = jnp.concatenate([cos] * 4, axis=1)
    sin4 = jnp.concatenate([sin] * 4, axis=1)
    qpe = q[:, 1024:1536] * cos4 + q[:, 1536:2048] * sin4
    qpe_out[...] = (qpe * q_scale).astype(BF16)

    kv = _dot(_rms(c_kv, kvn_ref[...]).astype(BF16), wukv_ref[...])
    knope_out[...] = kv[:, 0:1024].astype(BF16)
    vb_out[...] = kv[:, 1024:2048].astype(BF16)


def _proj(x2, mod3, npre, win, convw, convb, qks, gbias, qn, kvn, wuq, wukv,
          cos, sin, seq):
    t = x2.shape[0]
    tm = PROJ_TM
    tiles_per_seq = seq // tm
    hb = tm // HALO
    row = lambda i: (i, 0)
    out_w = [1024, 1024, 1024, 1024, 1024, 128, 1024, 512, 1024, 1024, 128]
    out_dt = [BF16] * 5 + [F32] + [BF16] * 5
    kern = functools.partial(
        _proj_kernel, tm=tm, tiles_per_seq=tiles_per_seq,
        q_scale=float((A_NOPE + A_ROPE) ** -0.5 * LOG2E))
    return pl.pallas_call(
        kern,
        grid=(t // tm,),
        in_specs=[
            pl.BlockSpec((tm, D_MODEL), row),
            pl.BlockSpec((HALO, D_MODEL), lambda i: (jnp.maximum(i * hb - 1, 0), 0)),
            pl.BlockSpec((None, N_MOD, D_MODEL), lambda i: (i // tiles_per_seq, 0, 0)),
            _const_spec((1, D_MODEL)),
            _const_spec(win.shape),
            _const_spec(convw.shape), _const_spec(convb.shape), _const_spec(qks.shape),
            _const_spec(gbias.shape), _const_spec(qn.shape), _const_spec(kvn.shape),
            _const_spec(wuq.shape), _const_spec(wukv.shape),
            pl.BlockSpec((tm, 128), lambda i: (i % tiles_per_seq, 0)),
            pl.BlockSpec((tm, 128), lambda i: (i % tiles_per_seq, 0)),
        ],
        out_specs=[pl.BlockSpec((tm, n), row) for n in out_w],
        out_shape=[jax.ShapeDtypeStruct((t, n), d) for n, d in zip(out_w, out_dt)],
        scratch_shapes=[
            pltpu.VMEM((tm + HALO, D_MODEL), BF16),
            pltpu.VMEM((tm + HALO, 2 * MA_QK), F32),
        ],
        compiler_params=pltpu.CompilerParams(
            dimension_semantics=("arbitrary",), vmem_limit_bytes=VMEM_LIMIT),
        name="proj",
    )(x2, x2, mod3, npre, win, convw, convb, qks, gbias,
      qn, kvn, wuq, wukv, cos, sin)


def _split3(x):
    hi = x.astype(BF16)
    r = x - hi.astype(F32)
    mid = r.astype(BF16)
    lo = (r - mid.astype(F32)).astype(BF16)
    return hi, mid, lo


def _mlstm_kernel(qk_ref, v_ref, og_ref, gc_ref, hnw_ref, y_ref,
                  ct_ref, m_ref, *, L):
    @pl.when(pl.program_id(1) == 0)
    def _():
        ct_ref[...] = jnp.zeros_like(ct_ref)
        m_ref[...] = jnp.zeros_like(m_ref)

    n_seq = qk_ref.shape[0]
    nlt = L // 128
    row = lax.broadcasted_iota(jnp.int32, (L, L), 0)
    col = lax.broadcasted_iota(jnp.int32, (L, L), 1)
    causal = col <= row
    tri = causal.astype(BF16)
    ones = jnp.ones((L, 128), BF16)
    chains = [(g, h) for g in range(n_seq) for h in range(M_HEADS)]

    def q_of(g, h):
        return qk_ref[g, :, h * M_DQK:(h + 1) * M_DQK]

    def k_of(g, h):
        return qk_ref[g, :, MA_QK + h * M_DQK:MA_QK + (h + 1) * M_DQK]

    def v_aug_of(g, h):
        return jnp.concatenate([v_ref[g, :, h * M_DV:(h + 1) * M_DV], ones], axis=1)

    gates = []
    for g in range(n_seq):
        gc = gc_ref[g] * LOG2E
        gr = gc.T[0:2 * M_HEADS, :]
        bcol_all = sum(_dot(tri, p) for p in _split3(gc))
        brow_all = sum(lax.dot_general(p, tri, _NT, preferred_element_type=F32)
                       for p in _split3(gr))
        gates.append((gc, gr, bcol_all, brow_all))

    def qk_product(c):
        return lax.dot_general(q_of(*c), k_of(*c), _NT, preferred_element_type=F32)

    def gate_and_mix(c, qk):
        g, h = c
        gc, gr, bcol_all, brow_all = gates[g]
        c_row = gr[h:h + 1, :] - brow_all[M_HEADS + h:M_HEADS + h + 1, :]
        i_rep = jnp.broadcast_to(gc[:, h:h + 1], (L, 128))
        b_rep = jnp.broadcast_to(bcol_all[:, M_HEADS + h:M_HEADS + h + 1], (L, 128))
        m_prev = m_ref[g, h:h + 1, :]
        a = b_rep + m_prev
        m_rows, s_rows = [], []
        for rb in range(nlt):
            rows = slice(rb * 128, (rb + 1) * 128)
            d_row = [b_rep[rows] + c_row[:, t * 128:(t + 1) * 128] for t in range(rb + 1)]
            d_row[rb] = jnp.where(causal[0:128, 0:128], d_row[rb], NEG)
            m_intra = jnp.max(functools.reduce(jnp.maximum, d_row), axis=1, keepdims=True)
            m_blk = jnp.maximum(a[rows], m_intra)
            tiles = [(qk[rows, t * 128:(t + 1) * 128]
                      * jnp.exp2(d_row[t] - m_blk)).astype(BF16) for t in range(rb + 1)]
            tiles += [jnp.zeros((128, 128), BF16)] * (nlt - 1 - rb)
            m_rows.append(m_blk)
            s_rows.append(jnp.concatenate(tiles, axis=1))
        m_out = jnp.concatenate(m_rows, axis=0)
        s_mat = jnp.concatenate(s_rows, axis=0)
        w_inter = jnp.exp2(a - m_out).astype(BF16)
        ct = ct_ref[g, h]
        lhs = jnp.concatenate([s_mat, w_inter * q_of(g, h)], axis=1)
        rhs = jnp.concatenate([v_aug_of(g, h), ct.astype(BF16)], axis=0)
        return _dot(lhs, rhs), (m_out, b_rep, i_rep, m_prev)

    def finish(c, nd, kept):
        g, h = c
        m_out, b_rep, i_rep, m_prev = kept
        num = nd[:, 0:M_DV]
        den = nd[:, M_DV:M_DV + 128]
        inv = 1.0 / jnp.maximum(jnp.abs(den), jnp.exp2(-m_out))
        ms = jnp.mean(num * num, axis=1, keepdims=True)
        f = inv * lax.rsqrt(inv * inv * ms + EPS)
        hn = num * jnp.concatenate([f, f], axis=1) * hnw_ref[:, h * M_DV:(h + 1) * M_DV]
        y_ref[g, :, h * M_DV:(h + 1) * M_DV] = (og_ref[g, :, h * M_DV:(h + 1) * M_DV]
                                                * hn.astype(BF16))

        b_last = b_rep[L - 1:L, :]
        g_prev = b_last + m_prev
        gl = b_last - b_rep + i_rep
        m_new = jnp.maximum(g_prev, jnp.max(gl, axis=0, keepdims=True))
        wk = jnp.exp2(gl - m_new).astype(BF16)
        decay = jnp.exp2(g_prev - m_new)
        wv = jnp.concatenate([wk] * 3, axis=1) * v_aug_of(g, h)
        ct_ref[g, h] = (jnp.concatenate([decay] * 3, axis=1) * ct_ref[g, h]
                        + lax.dot_general(k_of(g, h), wv, _TN, preferred_element_type=F32))
        m_ref[g, h:h + 1, :] = m_new

    pending = []
    for g in range(n_seq):
        mine = [c for c in chains if c[0] == g]
        qk = [qk_product(c) for c in mine]
        mixed = [gate_and_mix(c, s) for c, s in zip(mine, qk)]
        for c, nd, kept in pending:
            finish(c, nd, kept)
        pending = [(c, nd, kept) for c, (nd, kept) in zip(mine, mixed)]
    for c, nd, kept in pending:
        finish(c, nd, kept)


def _mlstm(qk, v, og, gc, hnw):
    b, seq, _ = qk.shape
    L = MLSTM_L
    G = MLSTM_G
    blk = lambda w: pl.BlockSpec((G, L, w), lambda bi, ci: (bi, ci, 0))
    return pl.pallas_call(
        functools.partial(_mlstm_kernel, L=L),
        grid=(b // G, seq // L),
        in_specs=[
            blk(2 * MA_QK), blk(MA_V), blk(MA_V), blk(128),
            _const_spec(hnw.shape),
        ],
        out_specs=blk(MA_V),
        out_shape=jax.ShapeDtypeStruct((b, seq, MA_V), BF16),
        scratch_shapes=[
            pltpu.VMEM((G, M_HEADS, M_DQK, M_DV + 128), F32),
            pltpu.VMEM((G, 8, 128), F32),
        ],
        compiler_params=pltpu.CompilerParams(
            dimension_semantics=("parallel", "arbitrary"), vmem_limit_bytes=VMEM_LIMIT),
        name="mlstm",
    )(qk, v, og, gc, hnw)


def _attn_kernel(qn_ref, qpe_ref, kn_ref, kpe_ref, v_ref, o_ref,
                 kcat_ref, vaug_ref, qcat_ref, m_ref, acc_ref, sa_ref, sb_ref, *, tq, tk, seq):
    assert tq == 2 * tk
    h = pl.program_id(1)
    n_tiles = seq // tq
    nt = tk // 128

    @pl.when(h == 0)
    def _():
        kcat_ref[:, A_NOPE:2 * A_NOPE] = kpe_ref[...]
        vaug_ref[:, A_DV:2 * A_DV] = jnp.ones((seq, A_DV), BF16)

    kcat_ref[:, 0:A_NOPE] = kn_ref[...]
    vaug_ref[:, 0:A_DV] = v_ref[...]
    lane = lax.broadcasted_iota(jnp.int32, (seq, 128), 1)
    own = ((lane // A_ROPE) == (h % 2)).astype(BF16)
    qcat_ref[:, 0:A_NOPE] = qn_ref[...]
    qcat_ref[:, A_NOPE:2 * A_NOPE] = qpe_ref[...] * own

    def logits(qt, kb, r0=0, nrows=tq):
        q0 = pl.multiple_of(qt * tq + r0, tk)
        k0 = pl.multiple_of(kb * tk, tk)
        return lax.dot_general(qcat_ref[pl.ds(q0, nrows), :], kcat_ref[pl.ds(k0, tk), :],
                               _NT, preferred_element_type=F32)

    def softmax_pv(s, kb, r0, nrows, masked):
        k0 = pl.multiple_of(kb * tk, tk)
        if masked:
            r = lax.broadcasted_iota(jnp.int32, (nrows, tk), 0)
            c = lax.broadcasted_iota(jnp.int32, (nrows, tk), 1)
            s = jnp.where(c <= r, s, NEG)
        tiles = [s[:, t * 128:(t + 1) * 128] for t in range(nt)]
        mx = functools.reduce(jnp.maximum, tiles)
        m_prev = m_ref[r0:r0 + nrows, :]
        m_new = jnp.maximum(m_prev, jnp.max(mx, axis=1, keepdims=True))
        alpha = jnp.exp2(m_prev - m_new)
        p = jnp.concatenate([jnp.exp2(t - m_new).astype(BF16) for t in tiles], axis=1)
        pv = _dot(p, vaug_ref[pl.ds(k0, tk), :])
        acc_ref[r0:r0 + nrows, :] = (jnp.concatenate([alpha, alpha], axis=1)
                                     * acc_ref[r0:r0 + nrows, :] + pv)
        m_ref[r0:r0 + nrows, :] = m_new

    sa_ref[...] = logits(0, 0)

    def tile_body(qt, carry):
        m_ref[...] = jnp.full_like(m_ref, NEG)
        acc_ref[...] = jnp.zeros_like(acc_ref)

        def pair(j):
            sb_ref[...] = logits(qt, j + 1)
            softmax_pv(sa_ref[...], j, 0, tq, False)
            sa_ref[...] = logits(qt, j + 2)
            softmax_pv(sb_ref[...], j + 1, 0, tq, False)

        def two_pairs(i, c):
            pair(4 * i)
            pair(4 * i + 2)
            return c

        lax.fori_loop(0, qt // 2, two_pairs, 0)

        @pl.when(qt % 2 == 1)
        def _():
            pair(2 * (qt - 1))

        d0 = 2 * qt
        sb_ref[0:tk, :] = logits(qt, d0 + 1, tk, tk)
        softmax_pv(sa_ref[...], d0, 0, tq, True)
        sa_ref[...] = logits(jnp.minimum(qt + 1, n_tiles - 1), 0)
        softmax_pv(sb_ref[0:tk, :], d0 + 1, tk, tk, True)
        o0 = pl.multiple_of(qt * tq, tq)
        o_ref[pl.ds(o0, tq), :] = (acc_ref[:, 0:A_DV] / acc_ref[:, A_DV:2 * A_DV]).astype(BF16)
        return carry

    lax.fori_loop(0, n_tiles, tile_body, 0)


def _attn(qn, qpe, kn, kpe, v):
    b, seq, _ = qn.shape
    tq, tk = ATTN_TQ, ATTN_TK
    return pl.pallas_call(
        functools.partial(_attn_kernel, tq=tq, tk=tk, seq=seq),
        grid=(b, A_HEADS),
        in_specs=[
            pl.BlockSpec((None, seq, A_NOPE), lambda bi, h: (bi, 0, h)),
            pl.BlockSpec((None, seq, 128), lambda bi, h: (bi, 0, h // 2)),
            pl.BlockSpec((None, seq, A_NOPE), lambda bi, h: (bi, 0, h)),
            pl.BlockSpec((None, seq, 128), lambda bi, h: (bi, 0, 0)),
            pl.BlockSpec((None, seq, A_DV), lambda bi, h: (bi, 0, h)),
        ],
        out_specs=pl.BlockSpec((None, seq, A_DV), lambda bi, h: (bi, 0, h)),
        out_shape=jax.ShapeDtypeStruct((b, seq, A_HEADS * A_DV), BF16),
        scratch_shapes=[
            pltpu.VMEM((seq, 2 * A_NOPE), BF16),
            pltpu.VMEM((seq, 2 * A_DV), BF16),
            pltpu.VMEM((seq, 2 * A_NOPE), BF16),
            pltpu.VMEM((tq, 128), F32),
            pltpu.VMEM((tq, 2 * A_DV), F32),
            pltpu.VMEM((tq, tk), F32),
            pltpu.VMEM((tq, tk), F32),
        ],
        compiler_params=pltpu.CompilerParams(
            dimension_semantics=("parallel", "arbitrary"),
            vmem_limit_bytes=VMEM_LIMIT),
        name="attn",
    )(qn, qpe, kn, kpe, v)


def _mlp_kernel(x_ref, ya_ref, yb_ref, ga_ref, gb_ref, mod_ref,
                npost_ref, npre2_ref, npost2_ref, wout_ref, w1_ref, w2_ref,
                o_ref, acc_ref):
    post_mix_gain = mod_ref[2:3, :] * npost_ref[...]
    shift_f = mod_ref[3:4, :]
    pre_mlp_gain = (1.0 + mod_ref[4:5, :]) * npre2_ref[...]
    post_mlp_gain = mod_ref[5:6, :] * npost2_ref[...]

    y = (ga_ref[...].astype(F32) * ya_ref[...].astype(F32)
         + gb_ref[...].astype(F32) * yb_ref[...].astype(F32))
    yo = _dot(y.astype(BF16), wout_ref[...])
    x1 = x_ref[...] + _rms(yo, post_mix_gain)

    h2 = (_rms(x1, pre_mlp_gain) + shift_f).astype(BF16)
    for c in range(D_FF // FF_CHUNK):
        u = jnp.maximum(_dot(h2, w1_ref[:, c * FF_CHUNK:(c + 1) * FF_CHUNK]), 0.0)
        part = _dot((u * u).astype(BF16), w2_ref[c * FF_CHUNK:(c + 1) * FF_CHUNK, :])
        if c == 0:
            acc_ref[...] = part
        else:
            acc_ref[...] += part
    o_ref[...] = x1 + _rms(acc_ref[...], post_mlp_gain)


def _mlp(x2, ya, yb, ga, gb, mod3, npost, npre2, npost2, wout, w1, w2, seq):
    t = x2.shape[0]
    tm = MLP_TM
    tiles_per_seq = seq // tm
    row = pl.BlockSpec((tm, D_MODEL), lambda i: (i, 0))
    return pl.pallas_call(
        _mlp_kernel,
        grid=(t // tm,),
        in_specs=[
            row, row, row, row, row,
            pl.BlockSpec((None, N_MOD, D_MODEL), lambda i: (i // tiles_per_seq, 0, 0)),
            _const_spec((1, D_MODEL)), _const_spec((1, D_MODEL)), _const_spec((1, D_MODEL)),
            _const_spec(wout.shape), _const_spec(w1.shape), _const_spec(w2.shape),
        ],
        out_specs=row,
        out_shape=jax.ShapeDtypeStruct((t, D_MODEL), F32),
        scratch_shapes=[pltpu.VMEM((tm, D_MODEL), F32)],
        compiler_params=pltpu.CompilerParams(
            dimension_semantics=("arbitrary",), vmem_limit_bytes=VMEM_LIMIT),
        name="mlp",
    )(x2, ya, yb, ga, gb, mod3, npost, npre2, npost2, wout, w1, w2)


def _rot_half_cols(w):
    half = w.shape[-1] // 2
    return jnp.concatenate([-w[..., half:], w[..., :half]], axis=-1)


def _regroup_kernel(w_ref, o_ref):
    sizes = (MA_QK, MA_QK, MA_V, MA_V, M_HEADS, M_HEADS, A_QRANK, A_KVRANK, A_ROPE,
             D_MODEL, D_MODEL)
    offs = [sum(sizes[:n]) for n in range(len(sizes))]
    o_q, _, _, _, o_i, _, o_cq, _, o_kpe, o_ga, _ = offs
    half = A_ROPE // 2
    cursor = [0]

    def put(src, n, negate=False):
        blk = w_ref[src:src + n, :]
        o_ref[cursor[0]:cursor[0] + n, :] = (-blk if negate else blk).astype(BF16)
        cursor[0] += n

    put(o_q, 2 * MA_QK + 2 * MA_V)
    put(o_ga, 2 * D_MODEL)
    put(o_cq, A_QRANK + A_KVRANK)
    put(o_kpe, A_ROPE)
    put(o_kpe, A_ROPE)
    for _ in range(2):
        put(o_kpe + half, half, negate=True)
        put(o_kpe, half)
    gate_rows = jnp.concatenate(
        [w_ref[o_i:o_i + 2 * M_HEADS, :],
         jnp.zeros((128 - 2 * M_HEADS, w_ref.shape[1]), F32)], axis=0)
    o_ref[cursor[0]:cursor[0] + 128, :] = gate_rows.astype(BF16)


def _prep_in_weights(w_in_t):
    n_in, k = w_in_t.shape
    n_out = 6 * D_MODEL
    return pl.pallas_call(
        _regroup_kernel,
        grid=(k // REGROUP_TK,),
        in_specs=[pl.BlockSpec((n_in, REGROUP_TK), lambda i: (0, i))],
        out_specs=pl.BlockSpec((n_out, REGROUP_TK), lambda i: (0, i)),
        out_shape=jax.ShapeDtypeStruct((n_out, k), BF16),
        compiler_params=pltpu.CompilerParams(
            dimension_semantics=("arbitrary",), vmem_limit_bytes=VMEM_LIMIT),
        name="regroup",
    )(w_in_t)


def _prep_mla_weights(w_uq, w_ukv):
    r = w_uq.reshape(A_QRANK, A_HEADS, A_NOPE + A_ROPE)
    nope = r[:, :, :A_NOPE].reshape(A_QRANK, A_HEADS * A_NOPE)
    pe = r[:, :, A_NOPE:]
    pe_rot = _rot_half_cols(pe).reshape(A_QRANK, A_HEADS * A_ROPE)
    pe = pe.reshape(A_QRANK, A_HEADS * A_ROPE)
    wuq = jnp.concatenate([nope, pe, pe_rot], axis=1).astype(BF16)
    r = w_ukv.reshape(A_KVRANK, A_HEADS, A_NOPE + A_DV)
    wukv = jnp.concatenate([r[:, :, :A_NOPE].reshape(A_KVRANK, -1),
                            r[:, :, A_NOPE:].reshape(A_KVRANK, -1)], axis=1).astype(BF16)
    return wuq, wukv


def _rope_tables(seq):
    half = A_ROPE // 2
    inv_freq = ROPE_THETA ** (-np.arange(half, dtype=np.float64) / half)
    ang = np.arange(seq, dtype=np.float64)[:, None] * inv_freq[None, :]
    reps = 128 // half
    return (jnp.asarray(np.tile(np.cos(ang), (1, reps)), F32),
            jnp.asarray(np.tile(np.sin(ang), (1, reps)), F32))


def kernel(x, c, w_ada, b_ada, norm_pre_mix, norm_post_mix, norm_pre_mlp, norm_post_mlp,
           w_in, mlstm_conv_w, mlstm_conv_b, mlstm_gate_b, mlstm_head_norm,
           mla_q_norm, mla_kv_norm, w_uq, w_ukv, w_out, w_ff1, w_ff2):
    bsz, seq, d = x.shape
    depth = w_ada.shape[0]
    cos, sin = _rope_tables(seq)
    c8 = jnp.pad(c, ((0, 8 - bsz), (0, 0)))
    qks = jnp.concatenate([jnp.ones((1, MA_QK), F32),
                           jnp.full((1, MA_QK), M_DQK ** -0.5, F32)], axis=1)
    row = lambda a: a.reshape(1, -1)
    x2 = x.reshape(bsz * seq, d)
    for l in range(depth):
        mod = _ada(c8, w_ada[l], row(b_ada[l]))[:bsz]
        mod3 = mod.reshape(bsz, N_MOD, d)
        win = _prep_in_weights(jnp.transpose(w_in[l]))
        wuq, wukv = _prep_mla_weights(w_uq[l], w_ukv[l])
        gbias = jnp.pad(row(mlstm_gate_b[l]), ((0, 0), (0, 128 - 2 * M_HEADS)))
        (qk, v_a, og, ga, gb, gates, qnope, qpe, knope, v_b, kpe) = _proj(
            x2, mod3, row(norm_pre_mix[l]), win, mlstm_conv_w[l], row(mlstm_conv_b[l]),
            qks, gbias, row(mla_q_norm[l]), row(mla_kv_norm[l]), wuq, wukv, cos, sin, seq)
        b3 = lambda a: a.reshape(bsz, seq, a.shape[-1])
        y_a = _mlstm(b3(qk), b3(v_a), b3(og), b3(gates), row(mlstm_head_norm[l]))
        y_b = _attn(b3(qnope), b3(qpe), b3(knope), b3(kpe), b3(v_b))
        x2 = _mlp(x2, y_a.reshape(bsz * seq, d), y_b.reshape(bsz * seq, d), ga, gb, mod3,
                  row(norm_post_mix[l]), row(norm_pre_mlp[l]), row(norm_post_mlp[l]),
                  w_out[l].astype(BF16), w_ff1[l].astype(BF16), w_ff2[l].astype(BF16), seq)
    return x2.reshape(bsz, seq, d)
```

```python
import functools

import jax
import jax.numpy as jnp
import numpy as np
from jax import lax
from jax.experimental import pallas as pl
from jax.experimental.pallas import tpu as pltpu

F32 = jnp.float32
BF16 = jnp.bfloat16

D_MODEL = 1024
M_HEADS = 4
M_DQK = 128
M_DV = 256
CONV_W = 4
A_HEADS = 8
A_NOPE = 128
A_ROPE = 64
A_DV = 128
A_QRANK = 384
A_KVRANK = 256
ROPE_THETA = 10000.0
D_FF = 4096
EPS = 1e-6
N_MOD = 6
MA_QK = M_HEADS * M_DQK
MA_V = M_HEADS * M_DV

PROJ_TM = 512
HALO = 16
MLSTM_L = 256
MLSTM_G = 4
ATTN_TQ = 1024
ATTN_TK = 512
MLP_TM = 512
FF_CHUNK = 1024
ADA_TN = 1536
REGROUP_TK = 256
VMEM_LIMIT = 56 * 1024 * 1024

NEG = -1e30
LOG2E = 1.4426950408889634

_NT = (((1,), (1,)), ((), ()))
_TN = (((0,), (0,)), ((), ()))


def _dot(a, b):
    return jnp.dot(a, b, preferred_element_type=F32)


def _sigmoid(x):
    return jax.nn.sigmoid(x)


def _rms(x, w):
    return x * lax.rsqrt(jnp.mean(x * x, axis=-1, keepdims=True) + EPS) * w


def _const_spec(shape):
    nd = len(shape)
    return pl.BlockSpec(shape, lambda *_: (0,) * nd, pipeline_mode=pl.Buffered(1))


def _ada_kernel(c_ref, w_ref, b_ref, o_ref):
    c = c_ref[...]
    a = (c * _sigmoid(c)).astype(BF16)
    o_ref[...] = _dot(a, w_ref[...].astype(BF16)) + b_ref[...]


def _ada(c8, w_ada, b_ada):
    n = w_ada.shape[1]
    return pl.pallas_call(
        _ada_kernel,
        grid=(n // ADA_TN,),
        in_specs=[
            pl.BlockSpec((8, D_MODEL), lambda j: (0, 0)),
            pl.BlockSpec((D_MODEL, ADA_TN), lambda j: (0, j)),
            pl.BlockSpec((1, ADA_TN), lambda j: (0, j)),
        ],
        out_specs=pl.BlockSpec((8, ADA_TN), lambda j: (0, j)),
        out_shape=jax.ShapeDtypeStruct((8, n), F32),
        compiler_params=pltpu.CompilerParams(
            dimension_semantics=("arbitrary",), vmem_limit_bytes=VMEM_LIMIT),
        name="ada",
    )(c8, w_ada, b_ada)


def _proj_kernel(x_ref, xh_ref, mod_ref, npre_ref,
                 win_ref,
                 convw_ref, convb_ref, qks_ref, gbias_ref, qn_ref, kvn_ref,
                 wuq_ref, wukv_ref, cos_ref, sin_ref,
                 qk_out, v_out, og_out, ga_out, gb_out, gates_out,
                 qnope_out, qpe_out, knope_out, vb_out, kpe_out,
                 hext_ref, z_ref, *, tm, tiles_per_seq, q_scale):
    i = pl.program_id(0)
    first = (i % tiles_per_seq) == 0
    wqk_ref, wv_ref, wo_ref, wga_ref, wgb_ref, ws_ref = (
        win_ref.at[n * D_MODEL:(n + 1) * D_MODEL, :] for n in range(6))

    def proj(a, wt_ref):
        return lax.dot_general(a, wt_ref[...], _NT, preferred_element_type=F32)

    shift = mod_ref[0:1, :]
    w_scaled = npre_ref[...] * (1.0 + mod_ref[1:2, :])

    def prenorm(xv):
        return _rms(xv, w_scaled) + shift

    hext_ref[HALO:, :] = prenorm(x_ref[...]).astype(BF16)
    hh = prenorm(xh_ref[...])
    hext_ref[0:HALO, :] = jnp.where(first, 0.0, hh).astype(BF16)

    z_ref[...] = proj(hext_ref[...], wqk_ref)
    zwin = z_ref[HALO - 8:, :]
    n_win = tm + 8
    acc = convb_ref[...]
    for j in range(CONV_W):
        lead = 8 - (CONV_W - 1) + j
        tap = zwin if lead == 0 else pltpu.roll(zwin, n_win - lead, 0)
        acc = acc + convw_ref[j:j + 1, :] * tap[0:tm, :]
    qk_out[...] = (acc * _sigmoid(acc) * qks_ref[...]).astype(BF16)

    h = hext_ref[HALO:, :]
    v_out[...] = proj(h, wv_ref).astype(BF16)
    og_out[...] = _sigmoid(proj(h, wo_ref)).astype(BF16)
    ga_out[...] = _sigmoid(proj(h, wga_ref)).astype(BF16)
    gb_out[...] = _sigmoid(proj(h, wgb_ref)).astype(BF16)

    s = proj(h, ws_ref)
    c_q = s[:, 0:A_QRANK]
    c_kv = s[:, A_QRANK:A_QRANK + A_KVRANK]
    kp = s[:, 640:768]
    kpr = s[:, 768:896]
    g = s[:, 896:1024] + gbias_ref[...]

    lane = lax.broadcasted_iota(jnp.int32, g.shape, 1)
    logsig = jnp.minimum(g, 0.0) - jnp.log1p(jnp.exp(-jnp.abs(g)))
    gates_out[...] = jnp.where(lane < M_HEADS, g, logsig)

    cos = cos_ref[...]
    sin = sin_ref[...]
    kpe_out[...] = (kp * cos + kpr * sin).astype(BF16)

    q = _dot(_rms(c_q, qn_ref[...]).astype(BF16), wuq_ref[...])
    qnope_out[...] = (q[:, 0:1024] * q_scale).astype(BF16)
    cos4 = jnp.concatenate([cos] * 4, axis=1)
    sin4 = jnp.concatenate([sin] * 4, axis=1)
    qpe = q[:, 1024:1536] * cos4 + q[:, 1536:2048] * sin4
    qpe_out[...] = (qpe * q_scale).astype(BF16)

    kv = _dot(_rms(c_kv, kvn_ref[...]).astype(BF16), wukv_ref[...])
    knope_out[...] = kv[:, 0:1024].astype(BF16)
    vb_out[...] = kv[:, 1024:2048].astype(BF16)


def _proj(x2, mod3, npre, win, convw, convb, qks, gbias, qn, kvn, wuq, wukv,
          cos, sin, seq):
    t = x2.shape[0]
    tm = PROJ_TM
    tiles_per_seq = seq // tm
    hb = tm // HALO
    row = lambda i: (i, 0)
    out_w = [1024, 1024, 1024, 1024, 1024, 128, 1024, 512, 1024, 1024, 128]
    out_dt = [BF16] * 5 + [F32] + [BF16] * 5
    kern = functools.partial(
        _proj_kernel, tm=tm, tiles_per_seq=tiles_per_seq,
        q_scale=float((A_NOPE + A_ROPE) ** -0.5 * LOG2E))
    return pl.pallas_call(
        kern,
        grid=(t // tm,),
        in_specs=[
            pl.BlockSpec((tm, D_MODEL), row),
            pl.BlockSpec((HALO, D_MODEL), lambda i: (jnp.maximum(i * hb - 1, 0), 0)),
            pl.BlockSpec((None, N_MOD, D_MODEL), lambda i: (i // tiles_per_seq, 0, 0)),
            _const_spec((1, D_MODEL)),
            _const_spec(win.shape),
            _const_spec(convw.shape), _const_spec(convb.shape), _const_spec(qks.shape),
            _const_spec(gbias.shape), _const_spec(qn.shape), _const_spec(kvn.shape),
            _const_spec(wuq.shape), _const_spec(wukv.shape),
            pl.BlockSpec((tm, 128), lambda i: (i % tiles_per_seq, 0)),
            pl.BlockSpec((tm, 128), lambda i: (i % tiles_per_seq, 0)),
        ],
        out_specs=[pl.BlockSpec((tm, n), row) for n in out_w],
        out_shape=[jax.ShapeDtypeStruct((t, n), d) for n, d in zip(out_w, out_dt)],
        scratch_shapes=[
            pltpu.VMEM((tm + HALO, D_MODEL), BF16),
            pltpu.VMEM((tm + HALO, 2 * MA_QK), F32),
        ],
        compiler_params=pltpu.CompilerParams(
            dimension_semantics=("arbitrary",), vmem_limit_bytes=VMEM_LIMIT),
        name="proj",
    )(x2, x2, mod3, npre, win, convw, convb, qks, gbias,
      qn, kvn, wuq, wukv, cos, sin)


def _split3(x):
    hi = x.astype(BF16)
    r = x - hi.astype(F32)
    mid = r.astype(BF16)
    lo = (r - mid.astype(F32)).astype(BF16)
    return hi, mid, lo


def _mlstm_kernel(qk_ref, v_ref, og_ref, gc_ref, hnw_ref, y_ref,
                  ct_ref, m_ref, *, L):
    @pl.when(pl.program_id(1) == 0)
    def _():
        ct_ref[...] = jnp.zeros_like(ct_ref)
        m_ref[...] = jnp.zeros_like(m_ref)

    n_seq = qk_ref.shape[0]
    nlt = L // 128
    row = lax.broadcasted_iota(jnp.int32, (L, L), 0)
    col = lax.broadcasted_iota(jnp.int32, (L, L), 1)
    causal = col <= row
    tri = causal.astype(BF16)
    ones = jnp.ones((L, 128), BF16)
    chains = [(g, h) for g in range(n_seq) for h in range(M_HEADS)]

    def q_of(g, h):
        return qk_ref[g, :, h * M_DQK:(h + 1) * M_DQK]

    def k_of(g, h):
        return qk_ref[g, :, MA_QK + h * M_DQK:MA_QK + (h + 1) * M_DQK]

    def v_aug_of(g, h):
        return jnp.concatenate([v_ref[g, :, h * M_DV:(h + 1) * M_DV], ones], axis=1)

    gates = []
    for g in range(n_seq):
        gc = gc_ref[g] * LOG2E
        gr = gc.T[0:2 * M_HEADS, :]
        bcol_all = sum(_dot(tri, p) for p in _split3(gc))
        brow_all = sum(lax.dot_general(p, tri, _NT, preferred_element_type=F32)
                       for p in _split3(gr))
        gates.append((gc, gr, bcol_all, brow_all))

    def qk_product(c):
        return lax.dot_general(q_of(*c), k_of(*c), _NT, preferred_element_type=F32)

    def gate_and_mix(c, qk):
        g, h = c
        gc, gr, bcol_all, brow_all = gates[g]
        c_row = gr[h:h + 1, :] - brow_all[M_HEADS + h:M_HEADS + h + 1, :]
        i_rep = jnp.broadcast_to(gc[:, h:h + 1], (L, 128))
        b_rep = jnp.broadcast_to(bcol_all[:, M_HEADS + h:M_HEADS + h + 1], (L, 128))
        m_prev = m_ref[g, h:h + 1, :]
        a = b_rep + m_prev
        m_rows, s_rows = [], []
        for rb in range(nlt):
            rows = slice(rb * 128, (rb + 1) * 128)
            d_row = [b_rep[rows] + c_row[:, t * 128:(t + 1) * 128] for t in range(rb + 1)]
            d_row[rb] = jnp.where(causal[0:128, 0:128], d_row[rb], NEG)
            m_intra = jnp.max(functools.reduce(jnp.maximum, d_row), axis=1, keepdims=True)
            m_blk = jnp.maximum(a[rows], m_intra)
            tiles = [(qk[rows, t * 128:(t + 1) * 128]
                      * jnp.exp2(d_row[t] - m_blk)).astype(BF16) for t in range(rb + 1)]
            tiles += [jnp.zeros((128, 128), BF16)] * (nlt - 1 - rb)
            m_rows.append(m_blk)
            s_rows.append(jnp.concatenate(tiles, axis=1))
        m_out = jnp.concatenate(m_rows, axis=0)
        s_mat = jnp.concatenate(s_rows, axis=0)
        w_inter = jnp.exp2(a - m_out).astype(BF16)
        ct = ct_ref[g, h]
        lhs = jnp.concatenate([s_mat, w_inter * q_of(g, h)], axis=1)
        rhs = jnp.concatenate([v_aug_of(g, h), ct.astype(BF16)], axis=0)
        return _dot(lhs, rhs), (m_out, b_rep, i_rep, m_prev)

    def finish(c, nd, kept):
        g, h = c
        m_out, b_rep, i_rep, m_prev = kept
        num = nd[:, 0:M_DV]
        den = nd[:, M_DV:M_DV + 128]
        inv = 1.0 / jnp.maximum(jnp.abs(den), jnp.exp2(-m_out))
        ms = jnp.mean(num * num, axis=1, keepdims=True)
        f = inv * lax.rsqrt(inv * inv * ms + EPS)
        hn = num * jnp.concatenate([f, f], axis=1) * hnw_ref[:, h * M_DV:(h + 1) * M_DV]
        y_ref[g, :, h * M_DV:(h + 1) * M_DV] = (og_ref[g, :, h * M_DV:(h + 1) * M_DV]
                                                * hn.astype(BF16))

        b_last = b_rep[L - 1:L, :]
        g_prev = b_last + m_prev
        gl = b_last - b_rep + i_rep
        m_new = jnp.maximum(g_prev, jnp.max(gl, axis=0, keepdims=True))
        wk = jnp.exp2(gl - m_new).astype(BF16)
        decay = jnp.exp2(g_prev - m_new)
        wv = jnp.concatenate([wk] * 3, axis=1) * v_aug_of(g, h)
        ct_ref[g, h] = (jnp.concatenate([decay] * 3, axis=1) * ct_ref[g, h]
                        + lax.dot_general(k_of(g, h), wv, _TN, preferred_element_type=F32))
        m_ref[g, h:h + 1, :] = m_new

    pending = []
    for g in range(n_seq):
        mine = [c for c in chains if c[0] == g]
        qk = [qk_product(c) for c in mine]
        mixed = [gate_and_mix(c, s) for c, s in zip(mine, qk)]
        for c, nd, kept in pending:
            finish(c, nd, kept)
        pending = [(c, nd, kept) for c, (nd, kept) in zip(mine, mixed)]
    for c, nd, kept in pending:
        finish(c, nd, kept)


def _mlstm(qk, v, og, gc, hnw):
    b, seq, _ = qk.shape
    L = MLSTM_L
    G = MLSTM_G
    blk = lambda w: pl.BlockSpec((G, L, w), lambda bi, ci: (bi, ci, 0))
    return pl.pallas_call(
        functools.partial(_mlstm_kernel, L=L),
        grid=(b // G, seq // L),
        in_specs=[
            blk(2 * MA_QK), blk(MA_V), blk(MA_V), blk(128),
            _const_spec(hnw.shape),
        ],
        out_specs=blk(MA_V),
        out_shape=jax.ShapeDtypeStruct((b, seq, MA_V), BF16),
        scratch_shapes=[
            pltpu.VMEM((G, M_HEADS, M_DQK, M_DV + 128), F32),
            pltpu.VMEM((G, 8, 128), F32),
        ],
        compiler_params=pltpu.CompilerParams(
            dimension_semantics=("parallel", "arbitrary"), vmem_limit_bytes=VMEM_LIMIT),
        name="mlstm",
    )(qk, v, og, gc, hnw)


def _attn_kernel(qn_ref, qpe_ref, kn_ref, kpe_ref, v_ref, o_ref,
                 kcat_ref, vaug_ref, qcat_ref, m_ref, acc_ref, sa_ref, sb_ref, *, tq, tk, seq):
    assert tq == 2 * tk
    h = pl.program_id(1)
    n_tiles = seq // tq
    nt = tk // 128

    @pl.when(h == 0)
    def _():
        kcat_ref[:, A_NOPE:2 * A_NOPE] = kpe_ref[...]
        vaug_ref[:, A_DV:2 * A_DV] = jnp.ones((seq, A_DV), BF16)

    kcat_ref[:, 0:A_NOPE] = kn_ref[...]
    vaug_ref[:, 0:A_DV] = v_ref[...]
    lane = lax.broadcasted_iota(jnp.int32, (seq, 128), 1)
    own = ((lane // A_ROPE) == (h % 2)).astype(BF16)
    qcat_ref[:, 0:A_NOPE] = qn_ref[...]
    qcat_ref[:, A_NOPE:2 * A_NOPE] = qpe_ref[...] * own

    def logits(qt, kb, r0=0, nrows=tq):
        q0 = qt * tq + r0
        k0 = kb * tk
        return lax.dot_general(qcat_ref[q0:q0 + nrows, :], kcat_ref[k0:k0 + tk, :],
                               _NT, preferred_element_type=F32)

    def softmax_pv(s, kb, r0, nrows, masked):
        k0 = kb * tk
        if masked:
            r = lax.broadcasted_iota(jnp.int32, (nrows, tk), 0)
            c = lax.broadcasted_iota(jnp.int32, (nrows, tk), 1)
            s = jnp.where(c <= r, s, NEG)
        tiles = [s[:, t * 128:(t + 1) * 128] for t in range(nt)]
        mx = functools.reduce(jnp.maximum, tiles)
        m_prev = m_ref[r0:r0 + nrows, :]
        m_new = jnp.maximum(m_prev, jnp.max(mx, axis=1, keepdims=True))
        alpha = jnp.exp2(m_prev - m_new)
        p = jnp.concatenate([jnp.exp2(t - m_new).astype(BF16) for t in tiles], axis=1)
        pv = _dot(p, vaug_ref[k0:k0 + tk, :])
        acc_ref[r0:r0 + nrows, :] = (jnp.concatenate([alpha, alpha], axis=1)
                                     * acc_ref[r0:r0 + nrows, :] + pv)
        m_ref[r0:r0 + nrows, :] = m_new

    sa_ref[...] = logits(0, 0)

    for qt in range(n_tiles):
        m_ref[...] = jnp.full_like(m_ref, NEG)
        acc_ref[...] = jnp.zeros_like(acc_ref)

        for j in range(0, 2 * qt, 2):
            sb_ref[...] = logits(qt, j + 1)
            softmax_pv(sa_ref[...], j, 0, tq, False)
            sa_ref[...] = logits(qt, j + 2)
            softmax_pv(sb_ref[...], j + 1, 0, tq, False)

        d0 = 2 * qt
        sb_ref[0:tk, :] = logits(qt, d0 + 1, tk, tk)
        softmax_pv(sa_ref[...], d0, 0, tq, True)
        if qt + 1 < n_tiles:
            sa_ref[...] = logits(qt + 1, 0)
        softmax_pv(sb_ref[0:tk, :], d0 + 1, tk, tk, True)
        o_ref[qt * tq:(qt + 1) * tq, :] = (
            acc_ref[:, 0:A_DV] / acc_ref[:, A_DV:2 * A_DV]).astype(BF16)


def _attn(qn, qpe, kn, kpe, v):
    b, seq, _ = qn.shape
    tq, tk = ATTN_TQ, ATTN_TK
    return pl.pallas_call(
        functools.partial(_attn_kernel, tq=tq, tk=tk, seq=seq),
        grid=(b, A_HEADS),
        in_specs=[
            pl.BlockSpec((None, seq, A_NOPE), lambda bi, h: (bi, 0, h)),
            pl.BlockSpec((None, seq, 128), lambda bi, h: (bi, 0, h // 2)),
            pl.BlockSpec((None, seq, A_NOPE), lambda bi, h: (bi, 0, h)),
            pl.BlockSpec((None, seq, 128), lambda bi, h: (bi, 0, 0)),
            pl.BlockSpec((None, seq, A_DV), lambda bi, h: (bi, 0, h)),
        ],
        out_specs=pl.BlockSpec((None, seq, A_DV), lambda bi, h: (bi, 0, h)),
        out_shape=jax.ShapeDtypeStruct((b, seq, A_HEADS * A_DV), BF16),
        scratch_shapes=[
            pltpu.VMEM((seq, 2 * A_NOPE), BF16),
            pltpu.VMEM((seq, 2 * A_DV), BF16),
            pltpu.VMEM((seq, 2 * A_NOPE), BF16),
            pltpu.VMEM((tq, 128), F32),
            pltpu.VMEM((tq, 2 * A_DV), F32),
            pltpu.VMEM((tq, tk), F32),
            pltpu.VMEM((tq, tk), F32),
        ],
        compiler_params=pltpu.CompilerParams(
            dimension_semantics=("parallel", "arbitrary"),
            vmem_limit_bytes=VMEM_LIMIT),
        name="attn",
    )(qn, qpe, kn, kpe, v)


def _mlp_kernel(x_ref, ya_ref, yb_ref, ga_ref, gb_ref, mod_ref,
                npost_ref, npre2_ref, npost2_ref, wout_ref, w1_ref, w2_ref,
                o_ref, acc_ref):
    post_mix_gain = mod_ref[2:3, :] * npost_ref[...]
    shift_f = mod_ref[3:4, :]
    pre_mlp_gain = (1.0 + mod_ref[4:5, :]) * npre2_ref[...]
    post_mlp_gain = mod_ref[5:6, :] * npost2_ref[...]

    y = (ga_ref[...].astype(F32) * ya_ref[...].astype(F32)
         + gb_ref[...].astype(F32) * yb_ref[...].astype(F32))
    yo = _dot(y.astype(BF16), wout_ref[...])
    x1 = x_ref[...] + _rms(yo, post_mix_gain)

    h2 = (_rms(x1, pre_mlp_gain) + shift_f).astype(BF16)
    for c in range(D_FF // FF_CHUNK):
        u = jnp.maximum(_dot(h2, w1_ref[:, c * FF_CHUNK:(c + 1) * FF_CHUNK]), 0.0)
        part = _dot((u * u).astype(BF16), w2_ref[c * FF_CHUNK:(c + 1) * FF_CHUNK, :])
        if c == 0:
            acc_ref[...] = part
        else:
            acc_ref[...] += part
    o_ref[...] = x1 + _rms(acc_ref[...], post_mlp_gain)


def _mlp(x2, ya, yb, ga, gb, mod3, npost, npre2, npost2, wout, w1, w2, seq):
    t = x2.shape[0]
    tm = MLP_TM
    tiles_per_seq = seq // tm
    row = pl.BlockSpec((tm, D_MODEL), lambda i: (i, 0))
    return pl.pallas_call(
        _mlp_kernel,
        grid=(t // tm,),
        in_specs=[
            row, row, row, row, row,
            pl.BlockSpec((None, N_MOD, D_MODEL), lambda i: (i // tiles_per_seq, 0, 0)),
            _const_spec((1, D_MODEL)), _const_spec((1, D_MODEL)), _const_spec((1, D_MODEL)),
            _const_spec(wout.shape), _const_spec(w1.shape), _const_spec(w2.shape),
        ],
        out_specs=row,
        out_shape=jax.ShapeDtypeStruct((t, D_MODEL), F32),
        scratch_shapes=[pltpu.VMEM((tm, D_MODEL), F32)],
        compiler_params=pltpu.CompilerParams(
            dimension_semantics=("arbitrary",), vmem_limit_bytes=VMEM_LIMIT),
        name="mlp",
    )(x2, ya, yb, ga, gb, mod3, npost, npre2, npost2, wout, w1, w2)


def _rot_half_cols(w):
    half = w.shape[-1] // 2
    return jnp.concatenate([-w[..., half:], w[..., :half]], axis=-1)


def _regroup_kernel(w_ref, o_ref):
    sizes = (MA_QK, MA_QK, MA_V, MA_V, M_HEADS, M_HEADS, A_QRANK, A_KVRANK, A_ROPE,
             D_MODEL, D_MODEL)
    offs = [sum(sizes[:n]) for n in range(len(sizes))]
    o_q, _, _, _, o_i, _, o_cq, _, o_kpe, o_ga, _ = offs
    half = A_ROPE // 2
    cursor = [0]

    def put(src, n, negate=False):
        blk = w_ref[src:src + n, :]
        o_ref[cursor[0]:cursor[0] + n, :] = (-blk if negate else blk).astype(BF16)
        cursor[0] += n

    put(o_q, 2 * MA_QK + 2 * MA_V)
    put(o_ga, 2 * D_MODEL)
    put(o_cq, A_QRANK + A_KVRANK)
    put(o_kpe, A_ROPE)
    put(o_kpe, A_ROPE)
    for _ in range(2):
        put(o_kpe + half, half, negate=True)
        put(o_kpe, half)
    gate_rows = jnp.concatenate(
        [w_ref[o_i:o_i + 2 * M_HEADS, :],
         jnp.zeros((128 - 2 * M_HEADS, w_ref.shape[1]), F32)], axis=0)
    o_ref[cursor[0]:cursor[0] + 128, :] = gate_rows.astype(BF16)


def _prep_in_weights(w_in_t):
    n_in, k = w_in_t.shape
    n_out = 6 * D_MODEL
    return pl.pallas_call(
        _regroup_kernel,
        grid=(k // REGROUP_TK,),
        in_specs=[pl.BlockSpec((n_in, REGROUP_TK), lambda i: (0, i))],
        out_specs=pl.BlockSpec((n_out, REGROUP_TK), lambda i: (0, i)),
        out_shape=jax.ShapeDtypeStruct((n_out, k), BF16),
        compiler_params=pltpu.CompilerParams(
            dimension_semantics=("arbitrary",), vmem_limit_bytes=VMEM_LIMIT),
        name="regroup",
    )(w_in_t)


def _prep_mla_weights(w_uq, w_ukv):
    r = w_uq.reshape(A_QRANK, A_HEADS, A_NOPE + A_ROPE)
    nope = r[:, :, :A_NOPE].reshape(A_QRANK, A_HEADS * A_NOPE)
    pe = r[:, :, A_NOPE:]
    pe_rot = _rot_half_cols(pe).reshape(A_QRANK, A_HEADS * A_ROPE)
    pe = pe.reshape(A_QRANK, A_HEADS * A_ROPE)
    wuq = jnp.concatenate([nope, pe, pe_rot], axis=1).astype(BF16)
    r = w_ukv.reshape(A_KVRANK, A_HEADS, A_NOPE + A_DV)
    wukv = jnp.concatenate([r[:, :, :A_NOPE].reshape(A_KVRANK, -1),
                            r[:, :, A_NOPE:].reshape(A_KVRANK, -1)], axis=1).astype(BF16)
    return wuq, wukv


def _rope_tables(seq):
    half = A_ROPE // 2
    inv_freq = ROPE_THETA ** (-np.arange(half, dtype=np.float64) / half)
    ang = np.arange(seq, dtype=np.float64)[:, None] * inv_freq[None, :]
    reps = 128 // half
    return (jnp.asarray(np.tile(np.cos(ang), (1, reps)), F32),
            jnp.asarray(np.tile(np.sin(ang), (1, reps)), F32))


def kernel(x, c, w_ada, b_ada, norm_pre_mix, norm_post_mix, norm_pre_mlp, norm_post_mlp,
           w_in, mlstm_conv_w, mlstm_conv_b, mlstm_gate_b, mlstm_head_norm,
           mla_q_norm, mla_kv_norm, w_uq, w_ukv, w_out, w_ff1, w_ff2):
    bsz, seq, d = x.shape
    depth = w_ada.shape[0]
    cos, sin = _rope_tables(seq)
    c8 = jnp.pad(c, ((0, 8 - bsz), (0, 0)))
    qks = jnp.concatenate([jnp.ones((1, MA_QK), F32),
                           jnp.full((1, MA_QK), M_DQK ** -0.5, F32)], axis=1)
    row = lambda a: a.reshape(1, -1)
    x2 = x.reshape(bsz * seq, d)
    for l in range(depth):
        mod = _ada(c8, w_ada[l], row(b_ada[l]))[:bsz]
        mod3 = mod.reshape(bsz, N_MOD, d)
        win = _prep_in_weights(jnp.transpose(w_in[l]))
        wuq, wukv = _prep_mla_weights(w_uq[l], w_ukv[l])
        gbias = jnp.pad(row(mlstm_gate_b[l]), ((0, 0), (0, 128 - 2 * M_HEADS)))
        (qk, v_a, og, ga, gb, gates, qnope, qpe, knope, v_b, kpe) = _proj(
            x2, mod3, row(norm_pre_mix[l]), win, mlstm_conv_w[l], row(mlstm_conv_b[l]),
            qks, gbias, row(mla_q_norm[l]), row(mla_kv_norm[l]), wuq, wukv, cos, sin, seq)
        b3 = lambda a: a.reshape(bsz, seq, a.shape[-1])
        y_a = _mlstm(b3(qk), b3(v_a), b3(og), b3(gates), row(mlstm_head_norm[l]))
        y_b = _attn(b3(qnope), b3(qpe), b3(knope), b3(kpe), b3(v_b))
        x2 = _mlp(x2, y_a.reshape(bsz * seq, d), y_b.reshape(bsz * seq, d), ga, gb, mod3,
                  row(norm_post_mix[l]), row(norm_pre_mlp[l]), row(norm_post_mlp[l]),
                  w_out[l].astype(BF16), w_ff1[l].astype(BF16), w_ff2[l].astype(BF16), seq)
    return x2.reshape(bsz, seq, d)
```

```python
import functools

import jax
import jax.numpy as jnp
import numpy as np
from jax import lax
from jax.experimental import pallas as pl
from jax.experimental.pallas import tpu as pltpu

F32 = jnp.float32
BF16 = jnp.bfloat16

D_MODEL = 1024
M_HEADS = 4
M_DQK = 128
M_DV = 256
CONV_W = 4
A_HEADS = 8
A_NOPE = 128
A_ROPE = 64
A_DV = 128
A_QRANK = 384
A_KVRANK = 256
ROPE_THETA = 10000.0
D_FF = 4096
EPS = 1e-6
N_MOD = 6
MA_QK = M_HEADS * M_DQK
MA_V = M_HEADS * M_DV

PROJ_TM = 512
HALO = 16
MLSTM_L = 256
MLSTM_G = 4
ATTN_TQ = 512
ATTN_TK = 256
MLP_TM = 512
FF_CHUNK = 1024
ADA_TN = 1536
REGROUP_TK = 256
VMEM_LIMIT = 56 * 1024 * 1024

NEG = -1e30
LOG2E = 1.4426950408889634

_NT = (((1,), (1,)), ((), ()))
_TN = (((0,), (0,)), ((), ()))


def _dot(a, b):
    return jnp.dot(a, b, preferred_element_type=F32)


def _sigmoid(x):
    return jax.nn.sigmoid(x)


def _rms(x, w):
    return x * lax.rsqrt(jnp.mean(x * x, axis=-1, keepdims=True) + EPS) * w


def _const_spec(shape):
    nd = len(shape)
    return pl.BlockSpec(shape, lambda *_: (0,) * nd, pipeline_mode=pl.Buffered(1))


def _ada_kernel(c_ref, w_ref, b_ref, o_ref):
    c = c_ref[...]
    a = (c * _sigmoid(c)).astype(BF16)
    o_ref[...] = _dot(a, w_ref[...].astype(BF16)) + b_ref[...]


def _ada(c8, w_ada, b_ada):
    n = w_ada.shape[1]
    return pl.pallas_call(
        _ada_kernel,
        grid=(n // ADA_TN,),
        in_specs=[
            pl.BlockSpec((8, D_MODEL), lambda j: (0, 0)),
            pl.BlockSpec((D_MODEL, ADA_TN), lambda j: (0, j)),
            pl.BlockSpec((1, ADA_TN), lambda j: (0, j)),
        ],
        out_specs=pl.BlockSpec((8, ADA_TN), lambda j: (0, j)),
        out_shape=jax.ShapeDtypeStruct((8, n), F32),
        compiler_params=pltpu.CompilerParams(
            dimension_semantics=("arbitrary",), vmem_limit_bytes=VMEM_LIMIT),
        name="ada",
    )(c8, w_ada, b_ada)


def _proj_kernel(x_ref, xh_ref, mod_ref, npre_ref,
                 win_ref,
                 convw_ref, convb_ref, qks_ref, gbias_ref, qn_ref, kvn_ref,
                 wuq_ref, wukv_ref, cos_ref, sin_ref,
                 qk_out, v_out, og_out, ga_out, gb_out, gates_out,
                 qnope_out, qpe_out, knope_out, vb_out, kpe_out,
                 hext_ref, z_ref, *, tm, tiles_per_seq, q_scale):
    i = pl.program_id(0)
    first = (i % tiles_per_seq) == 0
    wqk_ref, wv_ref, wo_ref, wga_ref, wgb_ref, ws_ref = (
        win_ref.at[n * D_MODEL:(n + 1) * D_MODEL, :] for n in range(6))

    def proj(a, wt_ref):
        return lax.dot_general(a, wt_ref[...], _NT, preferred_element_type=F32)

    shift = mod_ref[0:1, :]
    w_scaled = npre_ref[...] * (1.0 + mod_ref[1:2, :])

    def prenorm(xv):
        return _rms(xv, w_scaled) + shift

    hext_ref[HALO:, :] = prenorm(x_ref[...]).astype(BF16)
    hh = prenorm(xh_ref[...])
    hext_ref[0:HALO, :] = jnp.where(first, 0.0, hh).astype(BF16)

    z_ref[...] = proj(hext_ref[...], wqk_ref)
    zwin = z_ref[HALO - 8:, :]
    n_win = tm + 8
    acc = convb_ref[...]
    for j in range(CONV_W):
        lead = 8 - (CONV_W - 1) + j
        tap = zwin if lead == 0 else pltpu.roll(zwin, n_win - lead, 0)
        acc = acc + convw_ref[j:j + 1, :] * tap[0:tm, :]
    qk_out[...] = (acc * _sigmoid(acc) * qks_ref[...]).astype(BF16)

    h = hext_ref[HALO:, :]
    v_out[...] = proj(h, wv_ref).astype(BF16)
    og_out[...] = _sigmoid(proj(h, wo_ref)).astype(BF16)
    ga_out[...] = _sigmoid(proj(h, wga_ref)).astype(BF16)
    gb_out[...] = _sigmoid(proj(h, wgb_ref)).astype(BF16)

    s = proj(h, ws_ref)
    c_q = s[:, 0:A_QRANK]
    c_kv = s[:, A_QRANK:A_QRANK + A_KVRANK]
    kp = s[:, 640:768]
    kpr = s[:, 768:896]
    g = s[:, 896:1024] + gbias_ref[...]

    lane = lax.broadcasted_iota(jnp.int32, g.shape, 1)
    logsig = jnp.minimum(g, 0.0) - jnp.log1p(jnp.exp(-jnp.abs(g)))
    gates_out[...] = jnp.where(lane < M_HEADS, g, logsig)

    cos = cos_ref[...]
    sin = sin_ref[...]
    kpe_out[...] = (kp * cos + kpr * sin).astype(BF16)

    q = _dot(_rms(c_q, qn_ref[...]).astype(BF16), wuq_ref[...])
    qnope_out[...] = (q[:, 0:1024] * q_scale).astype(BF16)
    cos4 = jnp.concatenate([cos] * 4, axis=1)
    sin4 = jnp.concatenate([sin] * 4, axis=1)
    qpe = q[:, 1024:1536] * cos4 + q[:, 1536:2048] * sin4
    qpe_out[...] = (qpe * q_scale).astype(BF16)

    kv = _dot(_rms(c_kv, kvn_ref[...]).astype(BF16), wukv_ref[...])
    knope_out[...] = kv[:, 0:1024].astype(BF16)
    vb_out[...] = kv[:, 1024:2048].astype(BF16)


def _proj(x2, mod3, npre, win, convw, convb, qks, gbias, qn, kvn, wuq, wukv,
          cos, sin, seq):
    t = x2.shape[0]
    tm = PROJ_TM
    tiles_per_seq = seq // tm
    hb = tm // HALO
    row = lambda i: (i, 0)
    out_w = [1024, 1024, 1024, 1024, 1024, 128, 1024, 512, 1024, 1024, 128]
    out_dt = [BF16] * 5 + [F32] + [BF16] * 5
    kern = functools.partial(
        _proj_kernel, tm=tm, tiles_per_seq=tiles_per_seq,
        q_scale=float((A_NOPE + A_ROPE) ** -0.5 * LOG2E))
    return pl.pallas_call(
        kern,
        grid=(t // tm,),
        in_specs=[
            pl.BlockSpec((tm, D_MODEL), row),
            pl.BlockSpec((HALO, D_MODEL), lambda i: (jnp.maximum(i * hb - 1, 0), 0)),
            pl.BlockSpec((None, N_MOD, D_MODEL), lambda i: (i // tiles_per_seq, 0, 0)),
            _const_spec((1, D_MODEL)),
            _const_spec(win.shape),
            _const_spec(convw.shape), _const_spec(convb.shape), _const_spec(qks.shape),
            _const_spec(gbias.shape), _const_spec(qn.shape), _const_spec(kvn.shape),
            _const_spec(wuq.shape), _const_spec(wukv.shape),
            pl.BlockSpec((tm, 128), lambda i: (i % tiles_per_seq, 0)),
            pl.BlockSpec((tm, 128), lambda i: (i % tiles_per_seq, 0)),
        ],
        out_specs=[pl.BlockSpec((tm, n), row) for n in out_w],
        out_shape=[jax.ShapeDtypeStruct((t, n), d) for n, d in zip(out_w, out_dt)],
        scratch_shapes=[
            pltpu.VMEM((tm + HALO, D_MODEL), BF16),
            pltpu.VMEM((tm + HALO, 2 * MA_QK), F32),
        ],
        compiler_params=pltpu.CompilerParams(
            dimension_semantics=("arbitrary",), vmem_limit_bytes=VMEM_LIMIT),
        name="proj",
    )(x2, x2, mod3, npre, win, convw, convb, qks, gbias,
      qn, kvn, wuq, wukv, cos, sin)


def _split3(x):
    hi = x.astype(BF16)
    r = x - hi.astype(F32)
    mid = r.astype(BF16)
    lo = (r - mid.astype(F32)).astype(BF16)
    return hi, mid, lo


def _mlstm_kernel(qk_ref, v_ref, og_ref, gc_ref, hnw_ref, y_ref,
                  ct_ref, m_ref, *, L):
    @pl.when(pl.program_id(1) == 0)
    def _():
        ct_ref[...] = jnp.zeros_like(ct_ref)
        m_ref[...] = jnp.zeros_like(m_ref)

    n_seq = qk_ref.shape[0]
    nlt = L // 128
    row = lax.broadcasted_iota(jnp.int32, (L, L), 0)
    col = lax.broadcasted_iota(jnp.int32, (L, L), 1)
    causal = col <= row
    tri = causal.astype(BF16)
    ones = jnp.ones((L, 128), BF16)
    chains = [(g, h) for g in range(n_seq) for h in range(M_HEADS)]

    def q_of(g, h):
        return qk_ref[g, :, h * M_DQK:(h + 1) * M_DQK]

    def k_of(g, h):
        return qk_ref[g, :, MA_QK + h * M_DQK:MA_QK + (h + 1) * M_DQK]

    def v_aug_of(g, h):
        return jnp.concatenate([v_ref[g, :, h * M_DV:(h + 1) * M_DV], ones], axis=1)

    gates = []
    for g in range(n_seq):
        gc = gc_ref[g] * LOG2E
        gr = gc.T[0:2 * M_HEADS, :]
        bcol_all = sum(_dot(tri, p) for p in _split3(gc))
        brow_all = sum(lax.dot_general(p, tri, _NT, preferred_element_type=F32)
                       for p in _split3(gr))
        gates.append((gc, gr, bcol_all, brow_all))

    def qk_product(c):
        return lax.dot_general(q_of(*c), k_of(*c), _NT, preferred_element_type=F32)

    def gate_and_mix(c, qk):
        g, h = c
        gc, gr, bcol_all, brow_all = gates[g]
        c_row = gr[h:h + 1, :] - brow_all[M_HEADS + h:M_HEADS + h + 1, :]
        i_rep = jnp.broadcast_to(gc[:, h:h + 1], (L, 128))
        b_rep = jnp.broadcast_to(bcol_all[:, M_HEADS + h:M_HEADS + h + 1], (L, 128))
        m_prev = m_ref[g, h:h + 1, :]
        a = b_rep + m_prev
        m_rows, s_rows = [], []
        for rb in range(nlt):
            rows = slice(rb * 128, (rb + 1) * 128)
            d_row = [b_rep[rows] + c_row[:, t * 128:(t + 1) * 128] for t in range(rb + 1)]
            d_row[rb] = jnp.where(causal[0:128, 0:128], d_row[rb], NEG)
            m_intra = jnp.max(functools.reduce(jnp.maximum, d_row), axis=1, keepdims=True)
            m_blk = jnp.maximum(a[rows], m_intra)
            tiles = [(qk[rows, t * 128:(t + 1) * 128]
                      * jnp.exp2(d_row[t] - m_blk)).astype(BF16) for t in range(rb + 1)]
            tiles += [jnp.zeros((128, 128), BF16)] * (nlt - 1 - rb)
            m_rows.append(m_blk)
            s_rows.append(jnp.concatenate(tiles, axis=1))
        m_out = jnp.concatenate(m_rows, axis=0)
        s_mat = jnp.concatenate(s_rows, axis=0)
        w_inter = jnp.exp2(a - m_out).astype(BF16)
        ct = ct_ref[g, h]
        lhs = jnp.concatenate([s_mat, w_inter * q_of(g, h)], axis=1)
        rhs = jnp.concatenate([v_aug_of(g, h), ct.astype(BF16)], axis=0)
        return _dot(lhs, rhs), (m_out, b_rep, i_rep, m_prev)

    def finish(c, nd, kept):
        g, h = c
        m_out, b_rep, i_rep, m_prev = kept
        num = nd[:, 0:M_DV]
        den = nd[:, M_DV:M_DV + 128]
        inv = 1.0 / jnp.maximum(jnp.abs(den), jnp.exp2(-m_out))
        ms = jnp.mean(num * num, axis=1, keepdims=True)
        f = inv * lax.rsqrt(inv * inv * ms + EPS)
        hn = num * jnp.concatenate([f, f], axis=1) * hnw_ref[:, h * M_DV:(h + 1) * M_DV]
        y_ref[g, :, h * M_DV:(h + 1) * M_DV] = (og_ref[g, :, h * M_DV:(h + 1) * M_DV]
                                                * hn.astype(BF16))

        b_last = b_rep[L - 1:L, :]
        g_prev = b_last + m_prev
        gl = b_last - b_rep + i_rep
        m_new = jnp.maximum(g_prev, jnp.max(gl, axis=0, keepdims=True))
        wk = jnp.exp2(gl - m_new).astype(BF16)
        decay = jnp.exp2(g_prev - m_new)
        wv = jnp.concatenate([wk] * 3, axis=1) * v_aug_of(g, h)
        ct_ref[g, h] = (jnp.concatenate([decay] * 3, axis=1) * ct_ref[g, h]
                        + lax.dot_general(k_of(g, h), wv, _TN, preferred_element_type=F32))
        m_ref[g, h:h + 1, :] = m_new

    pending = []
    for g in range(n_seq):
        mine = [c for c in chains if c[0] == g]
        qk = [qk_product(c) for c in mine]
        mixed = [gate_and_mix(c, s) for c, s in zip(mine, qk)]
        for c, nd, kept in pending:
            finish(c, nd, kept)
        pending = [(c, nd, kept) for c, (nd, kept) in zip(mine, mixed)]
    for c, nd, kept in pending:
        finish(c, nd, kept)


def _mlstm(qk, v, og, gc, hnw):
    b, seq, _ = qk.shape
    L = MLSTM_L
    G = MLSTM_G
    blk = lambda w: pl.BlockSpec((G, L, w), lambda bi, ci: (bi, ci, 0))
    return pl.pallas_call(
        functools.partial(_mlstm_kernel, L=L),
        grid=(b // G, seq // L),
        in_specs=[
            blk(2 * MA_QK), blk(MA_V), blk(MA_V), blk(128),
            _const_spec(hnw.shape),
        ],
        out_specs=blk(MA_V),
        out_shape=jax.ShapeDtypeStruct((b, seq, MA_V), BF16),
        scratch_shapes=[
            pltpu.VMEM((G, M_HEADS, M_DQK, M_DV + 128), F32),
            pltpu.VMEM((G, 8, 128), F32),
        ],
        compiler_params=pltpu.CompilerParams(
            dimension_semantics=("parallel", "arbitrary"), vmem_limit_bytes=VMEM_LIMIT),
        name="mlstm",
    )(qk, v, og, gc, hnw)


def _attn_kernel(qn_ref, qpe_ref, kn_ref, kpe_ref, v_ref, o_ref,
                 kcat_ref, vaug_ref, qcat_ref, m_ref, acc_ref, sa_ref, sb_ref, *, tq, tk, seq):
    assert tq == 2 * tk
    h = pl.program_id(1)
    n_tiles = seq // tq
    nt = tk // 128

    @pl.when(h == 0)
    def _():
        kcat_ref[:, A_NOPE:2 * A_NOPE] = kpe_ref[...]
        vaug_ref[:, A_DV:2 * A_DV] = jnp.ones((seq, A_DV), BF16)

    kcat_ref[:, 0:A_NOPE] = kn_ref[...]
    vaug_ref[:, 0:A_DV] = v_ref[...]
    lane = lax.broadcasted_iota(jnp.int32, (seq, 128), 1)
    own = ((lane // A_ROPE) == (h % 2)).astype(BF16)
    qcat_ref[:, 0:A_NOPE] = qn_ref[...]
    qcat_ref[:, A_NOPE:2 * A_NOPE] = qpe_ref[...] * own

    def logits(qt, kb, r0=0, nrows=tq):
        q0 = qt * tq + r0
        k0 = kb * tk
        return lax.dot_general(qcat_ref[q0:q0 + nrows, :], kcat_ref[k0:k0 + tk, :],
                               _NT, preferred_element_type=F32)

    def softmax_pv(s, kb, r0, nrows, masked):
        k0 = kb * tk
        if masked:
            r = lax.broadcasted_iota(jnp.int32, (nrows, tk), 0)
            c = lax.broadcasted_iota(jnp.int32, (nrows, tk), 1)
            s = jnp.where(c <= r, s, NEG)
        tiles = [s[:, t * 128:(t + 1) * 128] for t in range(nt)]
        mx = functools.reduce(jnp.maximum, tiles)
        m_prev = m_ref[r0:r0 + nrows, :]
        m_new = jnp.maximum(m_prev, jnp.max(mx, axis=1, keepdims=True))
        alpha = jnp.exp2(m_prev - m_new)
        p = jnp.concatenate([jnp.exp2(t - m_new).astype(BF16) for t in tiles], axis=1)
        pv = _dot(p, vaug_ref[k0:k0 + tk, :])
        acc_ref[r0:r0 + nrows, :] = (jnp.concatenate([alpha, alpha], axis=1)
                                     * acc_ref[r0:r0 + nrows, :] + pv)
        m_ref[r0:r0 + nrows, :] = m_new

    sa_ref[...] = logits(0, 0)

    for qt in range(n_tiles):
        m_ref[...] = jnp.full_like(m_ref, NEG)
        acc_ref[...] = jnp.zeros_like(acc_ref)

        for j in range(0, 2 * qt, 2):
            sb_ref[...] = logits(qt, j + 1)
            softmax_pv(sa_ref[...], j, 0, tq, False)
            sa_ref[...] = logits(qt, j + 2)
            softmax_pv(sb_ref[...], j + 1, 0, tq, False)

        d0 = 2 * qt
        sb_ref[0:tk, :] = logits(qt, d0 + 1, tk, tk)
        softmax_pv(sa_ref[...], d0, 0, tq, True)
        if qt + 1 < n_tiles:
            sa_ref[...] = logits(qt + 1, 0)
        softmax_pv(sb_ref[0:tk, :], d0 + 1, tk, tk, True)
        o_ref[qt * tq:(qt + 1) * tq, :] = (
            acc_ref[:, 0:A_DV] / acc_ref[:, A_DV:2 * A_DV]).astype(BF16)


def _attn(qn, qpe, kn, kpe, v):
    b, seq, _ = qn.shape
    tq, tk = ATTN_TQ, ATTN_TK
    return pl.pallas_call(
        functools.partial(_attn_kernel, tq=tq, tk=tk, seq=seq),
        grid=(b, A_HEADS),
        in_specs=[
            pl.BlockSpec((None, seq, A_NOPE), lambda bi, h: (bi, 0, h)),
            pl.BlockSpec((None, seq, 128), lambda bi, h: (bi, 0, h // 2)),
            pl.BlockSpec((None, seq, A_NOPE), lambda bi, h: (bi, 0, h)),
            pl.BlockSpec((None, seq, 128), lambda bi, h: (bi, 0, 0)),
            pl.BlockSpec((None, seq, A_DV), lambda bi, h: (bi, 0, h)),
        ],
        out_specs=pl.BlockSpec((None, seq, A_DV), lambda bi, h: (bi, 0, h)),
        out_shape=jax.ShapeDtypeStruct((b, seq, A_HEADS * A_DV), BF16),
        scratch_shapes=[
            pltpu.VMEM((seq, 2 * A_NOPE), BF16),
            pltpu.VMEM((seq, 2 * A_DV), BF16),
            pltpu.VMEM((seq, 2 * A_NOPE), BF16),
            pltpu.VMEM((tq, 128), F32),
            pltpu.VMEM((tq, 2 * A_DV), F32),
            pltpu.VMEM((tq, tk), F32),
            pltpu.VMEM((tq, tk), F32),
        ],
        compiler_params=pltpu.CompilerParams(
            dimension_semantics=("parallel", "arbitrary"),
            vmem_limit_bytes=VMEM_LIMIT),
        name="attn",
    )(qn, qpe, kn, kpe, v)


def _mlp_kernel(x_ref, ya_ref, yb_ref, ga_ref, gb_ref, mod_ref,
                npost_ref, npre2_ref, npost2_ref, wout_ref, w1_ref, w2_ref,
                o_ref, acc_ref):
    post_mix_gain = mod_ref[2:3, :] * npost_ref[...]
    shift_f = mod_ref[3:4, :]
    pre_mlp_gain = (1.0 + mod_ref[4:5, :]) * npre2_ref[...]
    post_mlp_gain = mod_ref[5:6, :] * npost2_ref[...]

    y = (ga_ref[...].astype(F32) * ya_ref[...].astype(F32)
         + gb_ref[...].astype(F32) * yb_ref[...].astype(F32))
    yo = _dot(y.astype(BF16), wout_ref[...])
    x1 = x_ref[...] + _rms(yo, post_mix_gain)

    h2 = (_rms(x1, pre_mlp_gain) + shift_f).astype(BF16)
    for c in range(D_FF // FF_CHUNK):
        u = jnp.maximum(_dot(h2, w1_ref[:, c * FF_CHUNK:(c + 1) * FF_CHUNK]), 0.0)
        part = _dot((u * u).astype(BF16), w2_ref[c * FF_CHUNK:(c + 1) * FF_CHUNK, :])
        if c == 0:
            acc_ref[...] = part
        else:
            acc_ref[...] += part
    o_ref[...] = x1 + _rms(acc_ref[...], post_mlp_gain)


def _mlp(x2, ya, yb, ga, gb, mod3, npost, npre2, npost2, wout, w1, w2, seq):
    t = x2.shape[0]
    tm = MLP_TM
    tiles_per_seq = seq // tm
    row = pl.BlockSpec((tm, D_MODEL), lambda i: (i, 0))
    return pl.pallas_call(
        _mlp_kernel,
        grid=(t // tm,),
        in_specs=[
            row, row, row, row, row,
            pl.BlockSpec((None, N_MOD, D_MODEL), lambda i: (i // tiles_per_seq, 0, 0)),
            _const_spec((1, D_MODEL)), _const_spec((1, D_MODEL)), _const_spec((1, D_MODEL)),
            _const_spec(wout.shape), _const_spec(w1.shape), _const_spec(w2.shape),
        ],
        out_specs=row,
        out_shape=jax.ShapeDtypeStruct((t, D_MODEL), F32),
        scratch_shapes=[pltpu.VMEM((tm, D_MODEL), F32)],
        compiler_params=pltpu.CompilerParams(
            dimension_semantics=("arbitrary",), vmem_limit_bytes=VMEM_LIMIT),
        name="mlp",
    )(x2, ya, yb, ga, gb, mod3, npost, npre2, npost2, wout, w1, w2)


def _rot_half_cols(w):
    half = w.shape[-1] // 2
    return jnp.concatenate([-w[..., half:], w[..., :half]], axis=-1)


def _regroup_kernel(w_ref, o_ref):
    sizes = (MA_QK, MA_QK, MA_V, MA_V, M_HEADS, M_HEADS, A_QRANK, A_KVRANK, A_ROPE,
             D_MODEL, D_MODEL)
    offs = [sum(sizes[:n]) for n in range(len(sizes))]
    o_q, _, _, _, o_i, _, o_cq, _, o_kpe, o_ga, _ = offs
    half = A_ROPE // 2
    cursor = [0]

    def put(src, n, negate=False):
        blk = w_ref[src:src + n, :]
        o_ref[cursor[0]:cursor[0] + n, :] = (-blk if negate else blk).astype(BF16)
        cursor[0] += n

    put(o_q, 2 * MA_QK + 2 * MA_V)
    put(o_ga, 2 * D_MODEL)
    put(o_cq, A_QRANK + A_KVRANK)
    put(o_kpe, A_ROPE)
    put(o_kpe, A_ROPE)
    for _ in range(2):
        put(o_kpe + half, half, negate=True)
        put(o_kpe, half)
    gate_rows = jnp.concatenate(
        [w_ref[o_i:o_i + 2 * M_HEADS, :],
         jnp.zeros((128 - 2 * M_HEADS, w_ref.shape[1]), F32)], axis=0)
    o_ref[cursor[0]:cursor[0] + 128, :] = gate_rows.astype(BF16)


def _prep_in_weights(w_in_t):
    n_in, k = w_in_t.shape
    n_out = 6 * D_MODEL
    return pl.pallas_call(
        _regroup_kernel,
        grid=(k // REGROUP_TK,),
        in_specs=[pl.BlockSpec((n_in, REGROUP_TK), lambda i: (0, i))],
        out_specs=pl.BlockSpec((n_out, REGROUP_TK), lambda i: (0, i)),
        out_shape=jax.ShapeDtypeStruct((n_out, k), BF16),
        compiler_params=pltpu.CompilerParams(
            dimension_semantics=("arbitrary",), vmem_limit_bytes=VMEM_LIMIT),
        name="regroup",
    )(w_in_t)


def _prep_mla_weights(w_uq, w_ukv):
    r = w_uq.reshape(A_QRANK, A_HEADS, A_NOPE + A_ROPE)
    nope = r[:, :, :A_NOPE].reshape(A_QRANK, A_HEADS * A_NOPE)
    pe = r[:, :, A_NOPE:]
    pe_rot = _rot_half_cols(pe).reshape(A_QRANK, A_HEADS * A_ROPE)
    pe = pe.reshape(A_QRANK, A_HEADS * A_ROPE)
    wuq = jnp.concatenate([nope, pe, pe_rot], axis=1).astype(BF16)
    r = w_ukv.reshape(A_KVRANK, A_HEADS, A_NOPE + A_DV)
    wukv = jnp.concatenate([r[:, :, :A_NOPE].reshape(A_KVRANK, -1),
                            r[:, :, A_NOPE:].reshape(A_KVRANK, -1)], axis=1).astype(BF16)
    return wuq, wukv


def _rope_tables(seq):
    half = A_ROPE // 2
    inv_freq = ROPE_THETA ** (-np.arange(half, dtype=np.float64) / half)
    ang = np.arange(seq, dtype=np.float64)[:, None] * inv_freq[None, :]
    reps = 128 // half
    return (jnp.asarray(np.tile(np.cos(ang), (1, reps)), F32),
            jnp.asarray(np.tile(np.sin(ang), (1, reps)), F32))


def kernel(x, c, w_ada, b_ada, norm_pre_mix, norm_post_mix, norm_pre_mlp, norm_post_mlp,
           w_in, mlstm_conv_w, mlstm_conv_b, mlstm_gate_b, mlstm_head_norm,
           mla_q_norm, mla_kv_norm, w_uq, w_ukv, w_out, w_ff1, w_ff2):
    bsz, seq, d = x.shape
    depth = w_ada.shape[0]
    cos, sin = _rope_tables(seq)
    c8 = jnp.pad(c, ((0, 8 - bsz), (0, 0)))
    qks = jnp.concatenate([jnp.ones((1, MA_QK), F32),
                           jnp.full((1, MA_QK), M_DQK ** -0.5, F32)], axis=1)
    row = lambda a: a.reshape(1, -1)
    x2 = x.reshape(bsz * seq, d)
    for l in range(depth):
        mod = _ada(c8, w_ada[l], row(b_ada[l]))[:bsz]
        mod3 = mod.reshape(bsz, N_MOD, d)
        win = _prep_in_weights(jnp.transpose(w_in[l]))
        wuq, wukv = _prep_mla_weights(w_uq[l], w_ukv[l])
        gbias = jnp.pad(row(mlstm_gate_b[l]), ((0, 0), (0, 128 - 2 * M_HEADS)))
        (qk, v_a, og, ga, gb, gates, qnope, qpe, knope, v_b, kpe) = _proj(
            x2, mod3, row(norm_pre_mix[l]), win, mlstm_conv_w[l], row(mlstm_conv_b[l]),
            qks, gbias, row(mla_q_norm[l]), row(mla_kv_norm[l]), wuq, wukv, cos, sin, seq)
        b3 = lambda a: a.reshape(bsz, seq, a.shape[-1])
        y_a = _mlstm(b3(qk), b3(v_a), b3(og), b3(gates), row(mlstm_head_norm[l]))
        y_b = _attn(b3(qnope), b3(qpe), b3(knope), b3(kpe), b3(v_b))
        x2 = _mlp(x2, y_a.reshape(bsz * seq, d), y_b.reshape(bsz * seq, d), ga, gb, mod3,
                  row(norm_post_mix[l]), row(norm_pre_mlp[l]), row(norm_post_mlp[l]),
                  w_out[l].astype(BF16), w_ff1[l].astype(BF16), w_ff2[l].astype(BF16), seq)
    return x2.reshape(bsz, seq, d)
```
